```python
import math
import jax, jax.numpy as jnp
from jax import lax
import numpy as np

D_MODEL = 2048
BATCH = 2
SEQ = 4096
DEPTH = 2

GRID_W = 64
CTX_LEN = 256
NA_HEADS = 16
NA_HEAD_DIM = 64
NA_WIN_R = 8
NA_WIN_C = 16
GLA_HEADS = 4
GLA_DK = 64
GLA_DV = 128
GLA_GATE_RANK = 16
GLA_GATE_TAU = 16.0
GLA_CHUNK = 64
GDN_HEADS = 4
GDN_DK = 128
GDN_DV = 128
GDN_CONV = 5
GDN_CHUNK = 64
NA_W = NA_HEADS * NA_HEAD_DIM
GLA_W = GLA_HEADS * GLA_DV
GDN_W = GDN_HEADS * GDN_DV
GDN_QKV_W = 2 * GDN_HEADS * GDN_DK + GDN_W
MIX_W = NA_W + GLA_W + GDN_W
IN_SPLITS = (
    NA_W, NA_W, NA_W,
    GLA_HEADS * GLA_DK, GLA_HEADS * GLA_DK, GLA_W, GLA_W,
    2 * GLA_GATE_RANK,
    GDN_QKV_W,
    GDN_W,
    2 * GDN_HEADS, 2 * GDN_HEADS,
)
IN_W = sum(IN_SPLITS)
N_EXPERTS = 16
EXPERT_FF = 2048
EC_CAPACITY = 2
ROPE_BASE = 10000.0
NORM_EPS = 1e-6

kernel_name = "hybrid_na_gla_gdn_ec_moe_prefix_dit"


def rmsnorm(x, w):
    xf = x.astype(jnp.float32)
    y = xf * lax.rsqrt(jnp.mean(xf * xf, axis=-1, keepdims=True) + NORM_EPS)
    return (y * w.astype(jnp.float32)).astype(x.dtype)


def l2norm(x):
    return x * lax.rsqrt(jnp.sum(x * x, axis=-1, keepdims=True) + 1e-6)


def split_proj(p):
    return jnp.split(p, np.cumsum(IN_SPLITS)[:-1].tolist(), axis=-1)


def to_heads(a, n_heads):
    b, t, _ = a.shape
    return jnp.swapaxes(a.reshape(b, t, n_heads, -1), 1, 2).astype(jnp.float32)


def axial_rope(x, row_pos, col_pos):
    half = x.shape[-1] // 2
    nf = half // 2
    freqs = ROPE_BASE ** (-jnp.arange(nf, dtype=jnp.float32) / nf)

    def rot(xp, pos):
        ang = jnp.asarray(pos, jnp.float32)[:, None] * freqs[None, :]
        cos = jnp.cos(ang).astype(x.dtype)[None, :, None, :]
        sin = jnp.sin(ang).astype(x.dtype)[None, :, None, :]
        x1, x2 = xp[..., :nf], xp[..., nf:]
        return jnp.concatenate([x1 * cos - x2 * sin, x2 * cos + x1 * sin], axis=-1)

    return jnp.concatenate([rot(x[..., :half], row_pos), rot(x[..., half:], col_pos)], axis=-1)


def na_latent(q, k, v, k_ctx, v_ctx, rpb):
    b, s, h, dh = q.shape
    rows = s // GRID_W
    wr = min(NA_WIN_R, rows)
    r = np.arange(rows)
    ridx = np.clip(r - wr // 2, 0, rows - wr)[:, None] + np.arange(wr)[None, :]
    cq = np.arange(GRID_W)
    cstart = np.clip(cq - NA_WIN_C // 2, 0, GRID_W - NA_WIN_C)
    colmask = (cq[None, :] >= cstart[:, None]) & (cq[None, :] < cstart[:, None] + NA_WIN_C)
    dr = ridx - r[:, None] + NA_WIN_R - 1
    dc = np.clip(cq[None, :] - cq[:, None] + NA_WIN_C - 1, 0, 2 * NA_WIN_C - 2)
    bias = rpb[:, dr[:, None, :, None], dc[None, :, None, :]].astype(jnp.float32)
    bias = jnp.where(colmask[None, None, :, None, :], bias, -jnp.inf).reshape(h, rows, GRID_W, wr * GRID_W)
    qg = (q * dh ** -0.5).reshape(b, rows, GRID_W, h, dh)
    kg = k.reshape(b, rows, GRID_W, h, dh)[:, ridx]
    vg = v.reshape(b, rows, GRID_W, h, dh)[:, ridx]
    n_loc = wr * GRID_W
    s_loc = jnp.einsum('brqhd,brjkhd->bhrqjk', qg, kg).astype(jnp.float32).reshape(b, h, rows, GRID_W, n_loc) + bias
    s_ctx = jnp.einsum('brqhd,blhd->bhrql', qg, k_ctx).astype(jnp.float32)
    p = jax.nn.softmax(jnp.concatenate([s_loc, s_ctx], axis=-1), axis=-1).astype(v.dtype)
    p_loc = p[..., :n_loc].reshape(b, h, rows, GRID_W, wr, GRID_W)
    o = (jnp.einsum('bhrqjk,brjkhd->brqhd', p_loc, vg)
         + jnp.einsum('bhrql,blhd->brqhd', p[..., n_loc:], v_ctx))
    return o.reshape(b, s, h * dh)


def na_context(q, k, v):
    b, l, h, dh = q.shape
    s = jnp.einsum('blhd,bmhd->bhlm', q * dh ** -0.5, k).astype(jnp.float32)
    p = jax.nn.softmax(s, axis=-1).astype(v.dtype)
    return jnp.einsum('bhlm,bmhd->blhd', p, v).reshape(b, l, h * dh)


def gla_chunked(q, k, v, log_a, s0):
    b, h, t, dk = q.shape
    dv = v.shape[-1]
    n = t // GLA_CHUNK
    ar = np.arange(GLA_CHUNK)
    incl = ar[:, None] >= ar[None, :]

    def chunks(a):
        return jnp.moveaxis(a.reshape(b, h, n, GLA_CHUNK, a.shape[-1]), 2, 0)

    def step(state, xs):
        qc, kc, vc, lac = xs
        cum = jnp.cumsum(lac, axis=2)
        diff = jnp.where(incl[None, None, :, :, None], cum[:, :, :, None, :] - cum[:, :, None, :, :], -jnp.inf)
        att = jnp.einsum('bhtd,bhsd,bhtsd->bhts', qc, kc, jnp.exp(diff))
        o = jnp.einsum('bhtd,bhde->bhte', qc * jnp.exp(cum), state) + jnp.einsum('bhts,bhse->bhte', att, vc)
        last = cum[:, :, -1:, :]
        state = (jnp.exp(last[:, :, 0, :])[..., None] * state
                 + jnp.einsum('bhsd,bhse->bhde', kc * jnp.exp(last - cum), vc))
        return state, o

    state, o = lax.scan(step, s0, tuple(chunks(a) for a in (q, k, v, log_a)))
    return state, jnp.moveaxis(o, 0, 2).reshape(b, h, t, dv)


def gdn_chunked(q, k, v, g, beta, s0):
    b, h, t, dk = q.shape
    dv = v.shape[-1]
    cs = GDN_CHUNK
    n = t // cs
    q = q.reshape(b, h, n, cs, dk)
    k = k.reshape(b, h, n, cs, dk)
    v = v.reshape(b, h, n, cs, dv)
    gc = jnp.cumsum(g.reshape(b, h, n, cs), axis=-1)
    beta = beta.reshape(b, h, n, cs)
    ar = np.arange(cs)
    incl = ar[:, None] >= ar[None, :]
    strict = ar[:, None] > ar[None, :]
    decay = jnp.exp(jnp.where(incl, gc[..., :, None] - gc[..., None, :], -jnp.inf))
    kb = k * beta[..., None]
    m = jnp.where(strict, jnp.einsum('bhntd,bhnsd->bhnts', kb, k) * decay, 0.0)
    lower = m + jnp.eye(cs, dtype=m.dtype)
    rhs = jnp.concatenate([v * beta[..., None], kb * jnp.exp(gc)[..., None]], axis=-1)
    sol = lax.linalg.triangular_solve(lower, rhs, left_side=True, lower=True)
    u, w = sol[..., :dv], sol[..., dv:]
    a_qk = jnp.einsum('bhntd,bhnsd->bhnts', q, k) * decay
    q_dec = q * jnp.exp(gc)[..., None]
    k_dec = k * jnp.exp(gc[..., -1:] - gc)[..., None]
    g_last = jnp.exp(gc[..., -1])

    def step(state, xs):
        u_c, w_c, qd_c, a_c, kd_c, gl_c = xs
        v_new = u_c - jnp.einsum('bhtd,bhde->bhte', w_c, state)
        o = jnp.einsum('bhtd,bhde->bhte', qd_c, state) + jnp.einsum('bhts,bhse->bhte', a_c, v_new)
        state = state * gl_c[..., None, None] + jnp.einsum('bhsd,bhse->bhde', kd_c, v_new)
        return state, o

    xs = tuple(jnp.moveaxis(a, 2, 0) for a in (u, w, q_dec, a_qk, k_dec, g_last))
    state, o = lax.scan(step, s0, xs)
    return state, jnp.moveaxis(o, 0, 2).reshape(b, h, t, dv)


def bidir_scan(scan_fn, ctx_f, ctx_b, lat_f, lat_b, s0):
    def flip(args):
        return tuple(jnp.flip(a, axis=2) for a in args)
    s_cf, o_cf = scan_fn(*ctx_f, s0)
    s_cb, o_cb = scan_fn(*flip(ctx_b), s0)
    _, o_xf = scan_fn(*lat_f, s_cf)
    _, o_xb = scan_fn(*flip(lat_b), s_cb)
    return o_xf + jnp.flip(o_xb, axis=2), o_cf + jnp.flip(o_cb, axis=2)


def gated_head_norm(o, gate, norm_w, dtype):
    b, h, t, dv = o.shape
    o = rmsnorm(jnp.swapaxes(o, 1, 2), norm_w).reshape(b, t, h * dv)
    return (o * jax.nn.silu(gate.astype(jnp.float32))).astype(dtype)


def short_conv(u, w):
    kw = w.shape[0]
    y = lax.conv_general_dilated(u, w[:, None, :], window_strides=(1,), padding=((kw // 2, kw // 2),),
                                 dimension_numbers=('NWC', 'WIO', 'NWC'), feature_group_count=u.shape[-1])
    return jax.nn.silu(y)


def gla_prep(p, gate_w, gate_b, row_pos, col_pos, rope):
    q, k, v, g, lr = p[3:8]
    b, t, _ = q.shape
    if rope:
        q = axial_rope(q.reshape(b, t, GLA_HEADS, GLA_DK), row_pos, col_pos).reshape(b, t, -1)
        k = axial_rope(k.reshape(b, t, GLA_HEADS, GLA_DK), row_pos, col_pos).reshape(b, t, -1)
    z = jnp.einsum('btnr,nrk->nbtk', lr.reshape(b, t, 2, GLA_GATE_RANK), gate_w) + gate_b[:, None, None, :]
    log_a = jax.nn.log_sigmoid(z.astype(jnp.float32)) / GLA_GATE_TAU
    qh = to_heads(q, GLA_HEADS) * GLA_DK ** -0.5
    return (qh, to_heads(k, GLA_HEADS), to_heads(v, GLA_HEADS),
            to_heads(log_a[0], GLA_HEADS), to_heads(log_a[1], GLA_HEADS), g)


def gdn_prep(p, conv_w, a_log, dt_bias):
    qkv, z, a, bb = p[8:12]
    b, t, _ = qkv.shape
    qkv = short_conv(qkv, conv_w)
    q, k, v = jnp.split(qkv, [GDN_HEADS * GDN_DK, 2 * GDN_HEADS * GDN_DK], axis=-1)
    q = l2norm(to_heads(q, GDN_HEADS)) * GDN_DK ** -0.5
    k = l2norm(to_heads(k, GDN_HEADS))
    v = to_heads(v, GDN_HEADS)
    a = jnp.transpose(a.reshape(b, t, 2, GDN_HEADS).astype(jnp.float32), (2, 0, 3, 1))
    bb = jnp.transpose(bb.reshape(b, t, 2, GDN_HEADS).astype(jnp.float32), (2, 0, 3, 1))
    a_log = a_log.astype(jnp.float32)[:, None, :, None]
    dt_bias = dt_bias.astype(jnp.float32)[:, None, :, None]
    g = -jnp.exp(a_log) * jax.nn.softplus(a + dt_bias)
    beta = jax.nn.sigmoid(bb)
    return q, k, v, g, beta, z


def token_mixers(nx, nc, w_in, w_out, na_rpb, gla_gate_w, gla_gate_b, gla_norm_w,
                 gdn_conv_w, gdn_a_log, gdn_dt_bias, gdn_norm_w, row_pos, col_pos, with_ctx_out):
    b = nx.shape[0]
    dt = nx.dtype
    px = split_proj(nx @ w_in)
    pc = split_proj(nc @ w_in)

    def heads(a, nh):
        return a.reshape(a.shape[0], a.shape[1], nh, -1)

    qx, kx, vx = (heads(a, NA_HEADS) for a in px[0:3])
    qc, kc, vc = (heads(a, NA_HEADS) for a in pc[0:3])
    ox_na = na_latent(qx, kx, vx, kc, vc, na_rpb)

    xq, xk, xv, xaf, xab, xg = gla_prep(px, gla_gate_w, gla_gate_b, row_pos, col_pos, True)
    cq, ck, cv, caf, cab, cg = gla_prep(pc, gla_gate_w, gla_gate_b, row_pos, col_pos, False)
    s0 = jnp.zeros((b, GLA_HEADS, GLA_DK, GLA_DV), jnp.float32)
    ox_gla, oc_gla = bidir_scan(gla_chunked, (cq, ck, cv, caf), (cq, ck, cv, cab),
                                (xq, xk, xv, xaf), (xq, xk, xv, xab), s0)

    yq, yk, yv, yg, yb, yz = gdn_prep(px, gdn_conv_w, gdn_a_log, gdn_dt_bias)
    dq, dk, dv, dg, db, dz = gdn_prep(pc, gdn_conv_w, gdn_a_log, gdn_dt_bias)
    s0d = jnp.zeros((b, GDN_HEADS, GDN_DK, GDN_DV), jnp.float32)
    ox_gdn, oc_gdn = bidir_scan(gdn_chunked, (dq, dk, dv, dg[0], db[0]), (dq, dk, dv, dg[1], db[1]),
                                (yq, yk, yv, yg[0], yb[0]), (yq, yk, yv, yg[1], yb[1]), s0d)

    out_x = jnp.concatenate([ox_na, gated_head_norm(ox_gla, xg, gla_norm_w, dt),
                             gated_head_norm(ox_gdn, yz, gdn_norm_w, dt)], axis=-1) @ w_out
    if not with_ctx_out:
        return out_x, None
    out_c = jnp.concatenate([na_context(qc, kc, vc), gated_head_norm(oc_gla, cg, gla_norm_w, dt),
                             gated_head_norm(oc_gdn, dz, gdn_norm_w, dt)], axis=-1) @ w_out
    return out_x, out_c


def expert_choice_ffn(h, w_router, w_gate, w_up, w_down):
    b, n, _ = h.shape
    cap = EC_CAPACITY * n // N_EXPERTS
    aff = jax.nn.softmax((h @ w_router).astype(jnp.float32), axis=-1)
    gate, idx = lax.top_k(jnp.swapaxes(aff, 1, 2), cap)
    bidx = jnp.arange(b)[:, None, None]
    xs = h[bidx, idx]
    hid = jax.nn.silu(jnp.einsum('becd,edf->becf', xs, w_gate)) * jnp.einsum('becd,edf->becf', xs, w_up)
    y = jnp.einsum('becf,efd->becd', hid, w_down) * gate[..., None].astype(h.dtype)
    return jnp.zeros_like(h).at[bidx, idx].add(y)


def setup_inputs(seed: int = 0) -> dict:
    key = jax.random.key(seed)
    ks = jax.random.split(key, 24)
    f32 = jnp.float32
    D = D_MODEL

    def nrm(k, shape, scale):
        return jax.random.normal(k, shape, f32) * scale

    dt = jnp.exp(jax.random.uniform(ks[16], (DEPTH, 2, GDN_HEADS), f32, math.log(1e-3), math.log(1e-1)))
    return {
        "x": nrm(ks[0], (BATCH, SEQ, D), 1.0),
        "c": nrm(ks[1], (BATCH, D), 1.0),
        "ctx": nrm(ks[2], (BATCH, CTX_LEN, D), 1.0),
        "c_ctx": nrm(ks[3], (D,), 1.0),
        "w_ada": nrm(ks[4], (DEPTH, D, 6 * D), 0.5 * D ** -0.5),
        "b_ada": nrm(ks[5], (DEPTH, 6 * D), 0.02),
        "norm_mix_w": 1.0 + nrm(ks[6], (DEPTH, D), 0.02),
        "norm_ffn_w": 1.0 + nrm(ks[7], (DEPTH, D), 0.02),
        "w_in": nrm(ks[8], (DEPTH, D, IN_W), D ** -0.5),
        "w_out": nrm(ks[9], (DEPTH, MIX_W, D), MIX_W ** -0.5),
        "na_rpb": nrm(ks[10], (DEPTH, NA_HEADS, 2 * NA_WIN_R - 1, 2 * NA_WIN_C - 1), 0.1),
        "gla_gate_w": nrm(ks[11], (DEPTH, 2, GLA_GATE_RANK, GLA_HEADS * GLA_DK), GLA_GATE_RANK ** -0.5),
        "gla_gate_b": nrm(ks[12], (DEPTH, 2, GLA_HEADS * GLA_DK), 0.1),
        "gla_norm_w": 1.0 + nrm(ks[13], (DEPTH, GLA_DV), 0.02),
        "gdn_conv_w": nrm(ks[14], (DEPTH, GDN_CONV, GDN_QKV_W), GDN_CONV ** -0.5),
        "gdn_a_log": jnp.log(jax.random.uniform(ks[15], (DEPTH, 2, GDN_HEADS), f32, 1.0, 16.0)),
        "gdn_dt_bias": dt + jnp.log(-jnp.expm1(-dt)),
        "gdn_norm_w": 1.0 + nrm(ks[17], (DEPTH, GDN_DV), 0.02),
        "w_router": nrm(ks[18], (DEPTH, D, N_EXPERTS), D ** -0.5),
        "w_exp_gate": nrm(ks[19], (DEPTH, N_EXPERTS, D, EXPERT_FF), D ** -0.5),
        "w_exp_up": nrm(ks[20], (DEPTH, N_EXPERTS, D, EXPERT_FF), D ** -0.5),
        "w_exp_down": nrm(ks[21], (DEPTH, N_EXPERTS, EXPERT_FF, D), EXPERT_FF ** -0.5),
        "final_norm_w": 1.0 + nrm(ks[22], (D,), 0.02),
    }


def modulated_norm(h, w, shift, scale):
    return rmsnorm(h, w) * (1 + scale) + shift


def reference(x, c, ctx, c_ctx, w_ada, b_ada, norm_mix_w, norm_ffn_w, w_in, w_out, na_rpb,
              gla_gate_w, gla_gate_b, gla_norm_w, gdn_conv_w, gdn_a_log, gdn_dt_bias, gdn_norm_w,
              w_router, w_exp_gate, w_exp_up, w_exp_down, final_norm_w):
    s = x.shape[1]
    pos = np.arange(s)
    row_pos, col_pos = pos // GRID_W, pos % GRID_W
    hx, hc = x, ctx
    for l in range(DEPTH):
        ctx_out = l < DEPTH - 1
        mod_x = (jax.nn.silu(c) @ w_ada[l] + b_ada[l])[:, None, :]
        mod_c = (jax.nn.silu(c_ctx) @ w_ada[l] + b_ada[l])[None, None, :]
        sh1x, sc1x, g1x, sh2x, sc2x, g2x = jnp.split(mod_x, 6, axis=-1)
        sh1c, sc1c, g1c, sh2c, sc2c, g2c = jnp.split(mod_c, 6, axis=-1)
        nx = modulated_norm(hx, norm_mix_w[l], sh1x, sc1x)
        nc = modulated_norm(hc, norm_mix_w[l], sh1c, sc1c)
        ox, oc = token_mixers(nx, nc, w_in[l], w_out[l], na_rpb[l], gla_gate_w[l], gla_gate_b[l], gla_norm_w[l],
                              gdn_conv_w[l], gdn_a_log[l], gdn_dt_bias[l], gdn_norm_w[l], row_pos, col_pos, ctx_out)
        hx = hx + g1x * ox
        hx = hx + g2x * expert_choice_ffn(modulated_norm(hx, norm_ffn_w[l], sh2x, sc2x), w_router[l],
                                          w_exp_gate[l], w_exp_up[l], w_exp_down[l])
        if ctx_out:
            hc = hc + g1c * oc
            hc = hc + g2c * expert_choice_ffn(modulated_norm(hc, norm_ffn_w[l], sh2c, sc2c), w_router[l],
                                              w_exp_gate[l], w_exp_up[l], w_exp_down[l])
    return rmsnorm(hx, final_norm_w)
```

```python
import functools

import numpy as np
import jax
import jax.numpy as jnp
from jax import lax
from jax.experimental import pallas as pl
from jax.experimental.pallas import tpu as pltpu

F32 = jnp.float32
BF16 = jnp.bfloat16
I32 = jnp.int32
HI = lax.Precision.HIGHEST

GRID_W = 64
NA_HEADS, NA_DH, NA_WIN_R, NA_WIN_C = 16, 64, 8, 16
GLA_HEADS, GLA_DK, GLA_DV, GLA_RANK, GLA_TAU = 4, 64, 128, 16, 16.0
GLA_BLK = 16
GDN_HEADS, GDN_DK, GDN_DV, GDN_CONV, GDN_CHUNK = 4, 128, 128, 5, 64
N_EXPERTS, EC_CAPACITY = 16, 2
ROPE_BASE = 10000.0
NORM_EPS = 1e-6
NEG = -1e30

NA_W = NA_HEADS * NA_DH
COL_NA_Q, COL_NA_K, COL_NA_V = 0, NA_W, 2 * NA_W
COL_GLA_Q = 3 * NA_W
COL_GLA_K = COL_GLA_Q + GLA_HEADS * GLA_DK
COL_GLA_V = COL_GLA_K + GLA_HEADS * GLA_DK
COL_GLA_G = COL_GLA_V + GLA_HEADS * GLA_DV
COL_GDN_Q = COL_GLA_G + GLA_HEADS * GLA_DV
COL_GDN_K = COL_GDN_Q + GDN_HEADS * GDN_DK
COL_GDN_V = COL_GDN_K + GDN_HEADS * GDN_DK
COL_GDN_Z = COL_GDN_V + GDN_HEADS * GDN_DV
COL_SMALL = COL_GDN_Z + GDN_HEADS * GDN_DV
SMALL_A, SMALL_B = 2 * GLA_RANK, 2 * GLA_RANK + 2 * GDN_HEADS
INPROJ_TN = 512
NP_COLS = 7168

VMEM_LIMIT = 56 * 1024 * 1024


def _cparams(sem):
    return pltpu.CompilerParams(dimension_semantics=sem, vmem_limit_bytes=VMEM_LIMIT)


def _sigmoid(x):
    return 1.0 / (1.0 + jnp.exp(-x))


def _silu(x):
    return x * _sigmoid(x)


def _softplus(x):
    return jnp.maximum(x, 0.0) + jnp.log(1.0 + jnp.exp(-jnp.abs(x)))


def _bdot(a, b):
    return jnp.dot(a.astype(BF16), b.astype(BF16), preferred_element_type=F32)


def _dot_nt(a, b):
    return lax.dot_general(a, b, (((1,), (1,)), ((), ())), preferred_element_type=F32)


def _dot_tn(a, b):
    return lax.dot_general(a, b, (((0,), (0,)), ((), ())), preferred_element_type=F32)


def _dot01(m01, x):
    hi = x.astype(BF16)
    r1 = x - hi.astype(F32)
    mid = r1.astype(BF16)
    lo = (r1 - mid.astype(F32)).astype(BF16)
    d = lambda p: jnp.dot(m01, p, preferred_element_type=F32)
    return d(hi) + d(mid) + d(lo)


def _ada_kernel(c_ref, w_ref, b_ref, o_ref):
    o_ref[...] = jnp.dot(_silu(c_ref[...]), w_ref[...], precision=HI,
                         preferred_element_type=F32) + b_ref[...]


def _ada(cvec, w_ada, b_ada):
    depth, d, n6 = w_ada.shape
    tn = 1024 if n6 % 1024 == 0 else 512
    assert n6 % tn == 0
    return pl.pallas_call(
        _ada_kernel,
        grid=(depth, n6 // tn),
        in_specs=[pl.BlockSpec((8, d), lambda l, j: (0, 0)),
                  pl.BlockSpec((None, d, tn), lambda l, j: (l, 0, j)),
                  pl.BlockSpec((None, 1, tn), lambda l, j: (l, 0, j))],
        out_specs=pl.BlockSpec((None, 8, tn), lambda l, j: (l, 0, j)),
        out_shape=jax.ShapeDtypeStruct((depth, 8, n6), F32),
        compiler_params=_cparams(("arbitrary", "arbitrary")),
        name="ada",
    )(cvec, w_ada, b_ada.reshape(depth, 1, n6))


def _mod_norm(x, nw, mod_b, mod_c, row0, s_lat, k_shift):
    ms = jnp.mean(x * x, axis=-1, keepdims=True)
    y = x * lax.rsqrt(ms + NORM_EPS) * nw
    row = row0 + lax.broadcasted_iota(I32, (x.shape[0], 1), 0)
    is_lat = row < s_lat
    shift = jnp.where(is_lat, mod_b[k_shift:k_shift + 1], mod_c[k_shift:k_shift + 1])
    scale = jnp.where(is_lat, mod_b[k_shift + 1:k_shift + 2], mod_c[k_shift + 1:k_shift + 2])
    return y * (1.0 + scale) + shift


def _inproj_kernel(x_ref, nw_ref, mod_ref, w_ref, o_ref, xn_ref, *, s_lat, tm, nb):
    b, i, j = pl.program_id(0), pl.program_id(1), pl.program_id(2)

    @pl.when(j == 0)
    def _():
        rc = tm // 4

        def chunk(ci, carry):
            r = pl.multiple_of(ci * rc, 16)
            xn = _mod_norm(x_ref[pl.ds(r, rc), :], nw_ref[...], mod_ref[b], mod_ref[nb], i * tm + r, s_lat, 0)
            xn_ref[pl.ds(r, rc), :] = xn.astype(BF16)
            return carry

        lax.fori_loop(0, 4, chunk, 0)

    o_ref[...] = jnp.dot(xn_ref[...], w_ref[...].astype(BF16), preferred_element_type=F32)


def _inproj(h, nw, mod, w_p, s_lat):
    nb, t, d = h.shape
    tm = t // 4
    tn = INPROJ_TN
    npc = w_p.shape[1]
    return pl.pallas_call(
        functools.partial(_inproj_kernel, s_lat=s_lat, tm=tm, nb=nb),
        grid=(nb, t // tm, npc // tn),
        in_specs=[pl.BlockSpec((None, tm, d), lambda b, i, j: (b, i, 0)),
                  pl.BlockSpec((1, d), lambda b, i, j: (0, 0)),
                  pl.BlockSpec((8, 6, d), lambda b, i, j: (0, 0, 0)),
                  pl.BlockSpec((d, tn), lambda b, i, j: (0, j))],
        out_specs=pl.BlockSpec((None, tm, tn), lambda b, i, j: (b, i, j)),
        out_shape=jax.ShapeDtypeStruct((nb, t, npc), F32),
        scratch_shapes=[pltpu.VMEM((tm, d), BF16)],
        compiler_params=_cparams(("arbitrary", "arbitrary", "arbitrary")),
        name="inproj",
    )(h, nw, mod, w_p)


def _na_bias_table(rpb):
    dd = np.arange(NA_WIN_R)
    jj = np.arange(NA_WIN_R)
    dr = jj[None, :] + (NA_WIN_R - 1) - dd[:, None]
    cq = np.arange(GRID_W)
    dc = np.clip(cq[None, :] - cq[:, None] + NA_WIN_C - 1, 0, 2 * NA_WIN_C - 2)
    cstart = np.clip(cq - NA_WIN_C // 2, 0, GRID_W - NA_WIN_C)
    colmask = (cq[None, :] >= cstart[:, None]) & (cq[None, :] < cstart[:, None] + NA_WIN_C)
    tbl = rpb[:, dr[:, None, :, None], dc[None, :, None, :]].astype(F32)
    tbl = jnp.where(colmask[None, None, :, None, :], tbl, NEG)
    return tbl.reshape(rpb.shape[0], NA_WIN_R, GRID_W, NA_WIN_R * GRID_W)


def _softmax_pv(s_list, v_list):
    m = s_list[0].max(axis=-1, keepdims=True)
    for s in s_list[1:]:
        m = jnp.maximum(m, s.max(axis=-1, keepdims=True))
    den = 0.0
    o = 0.0
    for s, v in zip(s_list, v_list):
        p = jnp.exp(s - m)
        den = den + p.sum(axis=-1, keepdims=True)
        o = o + jnp.dot(p.astype(BF16), v, preferred_element_type=F32)
    return o / den


def _na_kernel(q_ref, k_ref, v_ref, bias_ref, o_ref, kb_ref, vb_ref, *, s_lat, l_ctx):
    rows = s_lat // GRID_W
    nwin = NA_WIN_R * GRID_W
    scale = NA_DH ** -0.5
    kb_ref[...] = k_ref[...].astype(BF16)
    vb_ref[...] = v_ref[...].astype(BF16)
    lane = lax.broadcasted_iota(I32, (1, 2 * NA_DH), 1)
    head_lanes = (lane < NA_DH, lane >= NA_DH)
    kc = kb_ref[pl.ds(s_lat, l_ctx), :]
    vc = vb_ref[pl.ds(s_lat, l_ctx), :]

    qc = q_ref[pl.ds(s_lat, l_ctx), :] * scale
    outs = []
    for hh in range(2):
        qh = jnp.where(head_lanes[hh], qc, 0.0).astype(BF16)
        outs.append(_softmax_pv([_dot_nt(qh, kc)], [vc]))
    o_ref[pl.ds(s_lat, l_ctx), :] = jnp.where(head_lanes[0], outs[0], outs[1])

    def body(r, carry):
        r0 = jnp.clip(r - NA_WIN_R // 2, 0, rows - NA_WIN_R)
        d = r - r0
        qs = pl.multiple_of(r * GRID_W, GRID_W)
        ks = pl.multiple_of(r0 * GRID_W, GRID_W)
        q = q_ref[pl.ds(qs, GRID_W), :] * scale
        kw = kb_ref[pl.ds(ks, nwin), :]
        vw = vb_ref[pl.ds(ks, nwin), :]
        res = []
        for hh in range(2):
            qh = jnp.where(head_lanes[hh], q, 0.0).astype(BF16)
            s_loc = _dot_nt(qh, kw) + bias_ref[hh, d]
            s_ctx = _dot_nt(qh, kc)
            res.append(_softmax_pv([s_loc, s_ctx], [vw, vc]))
        o_ref[pl.ds(qs, GRID_W), :] = jnp.where(head_lanes[0], res[0], res[1])
        return carry

    lax.fori_loop(0, rows, body, 0)


def _na(p, bias_tbl, s_lat, l_ctx):
    nb, t, _ = p.shape
    blk = lambda off: pl.BlockSpec((None, t, 128), lambda b, g, off=off: (b, 0, off // 128 + g))
    return pl.pallas_call(
        functools.partial(_na_kernel, s_lat=s_lat, l_ctx=l_ctx),
        grid=(nb, NA_HEADS // 2),
        in_specs=[blk(COL_NA_Q), blk(COL_NA_K), blk(COL_NA_V),
                  pl.BlockSpec((2, NA_WIN_R, GRID_W, NA_WIN_R * GRID_W), lambda b, g: (g, 0, 0, 0))],
        out_specs=pl.BlockSpec((None, t, 128), lambda b, g: (b, 0, g)),
        out_shape=jax.ShapeDtypeStruct((nb, t, NA_W), F32),
        scratch_shapes=[pltpu.VMEM((t, 128), BF16), pltpu.VMEM((t, 128), BF16)],
        compiler_params=_cparams(("arbitrary", "arbitrary")),
        name="na",
    )(p, p, p, bias_tbl)


def _rope_tables(s_lat, l_ctx):
    width = GLA_HEADS * GLA_DK
    nf = GLA_DK // 4
    pos = np.arange(s_lat)
    lane = np.arange(width)
    sub = lane % GLA_DK
    freqs = ROPE_BASE ** (-jnp.arange(nf, dtype=F32) / nf)
    p_sel = jnp.where((sub < GLA_DK // 2)[None, :], jnp.asarray(pos // GRID_W, F32)[:, None],
                      jnp.asarray(pos % GRID_W, F32)[:, None])
    ang = p_sel * freqs[sub % nf][None, :]
    sign = np.where((sub % (2 * nf)) < nf, -1.0, 1.0).astype(np.float32)
    cos = jnp.concatenate([jnp.cos(ang), jnp.ones((l_ctx, width), F32)], axis=0)
    sin = jnp.concatenate([jnp.sin(ang) * sign[None, :], jnp.zeros((l_ctx, width), F32)], axis=0)
    return cos, sin


def _block_tri(n, blk):
    i = np.arange(n)
    same = (i[:, None] // blk) == (i[None, :] // blk)
    lower = same & (i[None, :] <= i[:, None])
    upper = same & (i[None, :] >= i[:, None])
    return jnp.asarray(lower, BF16), jnp.asarray(upper, BF16)


def _gla_prep_kernel(q_ref, k_ref, sm_ref, cos_ref, sin_ref, gw_ref, gb_ref, tl_ref, tu_ref,
                     qr_ref, kr_ref, cf_ref, cb_ref):
    width = GLA_HEADS * GLA_DK
    nf = GLA_DK // 4
    lane = lax.broadcasted_iota(I32, (1, width), 1)
    first = (lane % (2 * nf)) < nf
    cos, sin = cos_ref[...], sin_ref[...]

    def rope(x):
        swapped = jnp.where(first, pltpu.roll(x, width - nf, 1), pltpu.roll(x, nf, 1))
        return x * cos + swapped * sin

    qr_ref[...] = rope(q_ref[...]) * (GLA_DK ** -0.5)
    kr_ref[...] = rope(k_ref[...])
    sm = sm_ref[...]
    for dr, (tri_ref, out_ref) in enumerate(((tl_ref, cf_ref), (tu_ref, cb_ref))):
        z = jnp.dot(sm, gw_ref[dr], precision=HI, preferred_element_type=F32) + gb_ref[dr]
        log_a = (jnp.minimum(z, 0.0) - jnp.log(1.0 + jnp.exp(-jnp.abs(z)))) * (1.0 / GLA_TAU)
        out_ref[...] = _dot01(tri_ref[...], log_a)


def _gla_prep(p, cos, sin, gw_p, gb, s_lat):
    nb, t, _ = p.shape
    width = GLA_HEADS * GLA_DK
    tr = 256
    tl, tu = _block_tri(tr, GLA_BLK)
    row_blk = lambda w, off: pl.BlockSpec((None, tr, w), lambda b, i, off=off, w=w: (b, i, off // w))
    tab = pl.BlockSpec((tr, width), lambda b, i: (i, 0))
    full = lambda shp: pl.BlockSpec(shp, lambda b, i, n=len(shp): (0,) * n)
    out = pl.BlockSpec((None, tr, width), lambda b, i: (b, i, 0))
    return pl.pallas_call(
        _gla_prep_kernel,
        grid=(nb, t // tr),
        in_specs=[row_blk(width, COL_GLA_Q), row_blk(width, COL_GLA_K), row_blk(128, COL_SMALL),
                  tab, tab, full((2, 128, width)), full((2, 1, width)), full((tr, tr)), full((tr, tr))],
        out_specs=[out] * 4,
        out_shape=[jax.ShapeDtypeStruct((nb, t, width), F32)] * 4,
        compiler_params=_cparams(("arbitrary", "arbitrary")),
        name="gla_prep",
    )(p, p, p, cos, sin, gw_p, gb, tl, tu)


def _gla_scan_kernel(qr_ref, kr_ref, cf_ref, cb_ref, v_ref, r3_ref, o_ref, st_ref, *, s_lat, l_ctx):
    nlat, nctx = s_lat // GLA_BLK, l_ctx // GLA_BLK
    o_ref[...] = jnp.zeros(o_ref.shape, F32)
    st_ref[...] = jnp.zeros(st_ref.shape, F32)
    sub = lax.broadcasted_iota(I32, (GLA_BLK, 2 * GLA_DK), 0)
    bd = (lax.broadcasted_iota(I32, (2 * GLA_DV, 2 * GLA_DK), 0) // GLA_DV
          == lax.broadcasted_iota(I32, (2 * GLA_DV, 2 * GLA_DK), 1) // GLA_DK)

    def step(i, carry):
        in_ctx = i < nctx
        jf = jnp.where(in_ctx, nlat + i, i - nctx)
        jb = jnp.where(in_ctx, nlat + nctx - 1 - i, nlat - 1 - (i - nctx))
        for dr, (j, c_ref) in enumerate(((jf, cf_ref), (jb, cb_ref))):
            r = pl.multiple_of(j * GLA_BLK, GLA_BLK)
            q = qr_ref[pl.ds(r, GLA_BLK), :]
            k = kr_ref[pl.ds(r, GLA_BLK), :]
            cum = c_ref[pl.ds(r, GLA_BLK), :]
            v = v_ref[pl.ds(r, GLA_BLK), :]
            tot = cum[GLA_BLK - 1:GLA_BLK] if dr == 0 else cum[0:1]
            qd = (q * jnp.exp(cum)).astype(BF16)
            kd = (k * jnp.exp(tot - cum)).astype(BF16)
            st = st_ref[dr]
            o_state = _dot_nt(qd, st.astype(BF16))
            tiles = []
            for s in range(GLA_BLK):
                valid = (sub >= s) if dr == 0 else (sub <= s)
                w = jnp.exp(jnp.where(valid, cum - cum[s:s + 1], NEG))
                tiles.append((w * q * k[s:s + 1]).astype(BF16))
            res = jnp.dot(jnp.concatenate(tiles, axis=0), r3_ref[...], preferred_element_type=F32)
            o_diag = res[0:GLA_BLK] * v[0:1]
            for s in range(1, GLA_BLK):
                o_diag = o_diag + res[s * GLA_BLK:(s + 1) * GLA_BLK] * v[s:s + 1]
            o_ref[pl.ds(r, GLA_BLK), :] += o_state + o_diag
            upd = _dot_tn(v.astype(BF16), kd)
            st_ref[dr] = st * jnp.exp(tot) + jnp.where(bd, upd, 0.0)
        return carry

    lax.fori_loop(0, nlat + nctx, step, 0)


def _gla_scan(qr, kr, cf, cb, p, s_lat, l_ctx):
    nb, t, _ = p.shape
    lanes = 2 * GLA_DK
    r3 = jnp.asarray((np.arange(lanes)[:, None] // GLA_DK) == (np.arange(2 * GLA_DV)[None, :] // GLA_DV), BF16)
    blk = pl.BlockSpec((None, t, lanes), lambda b, g: (b, 0, g))
    return pl.pallas_call(
        functools.partial(_gla_scan_kernel, s_lat=s_lat, l_ctx=l_ctx),
        grid=(nb, GLA_HEADS // 2),
        in_specs=[blk, blk, blk, blk,
                  pl.BlockSpec((None, t, 2 * GLA_DV), lambda b, g: (b, 0, COL_GLA_V // (2 * GLA_DV) + g)),
                  pl.BlockSpec((lanes, 2 * GLA_DV), lambda b, g: (0, 0))],
        out_specs=pl.BlockSpec((None, t, 2 * GLA_DV), lambda b, g: (b, 0, g)),
        out_shape=jax.ShapeDtypeStruct((nb, t, GLA_HEADS * GLA_DV), F32),
        scratch_shapes=[pltpu.VMEM((2, 2 * GLA_DV, lanes), F32)],
        compiler_params=_cparams(("arbitrary", "arbitrary")),
        name="gla_scan",
    )(qr, kr, cf, cb, p, r3)


def _gdn_conv_kernel(x_ref, w_ref, o_ref, *, s_lat, normalize, scale):
    t = x_ref.shape[0]
    x = x_ref[...]
    tpos = lax.broadcasted_iota(I32, (t, 1), 0)
    half = GDN_CONV // 2
    acc = x * w_ref[half:half + 1, :]
    for j in range(GDN_CONV):
        if j == half:
            continue
        dlt = j - half
        src = tpos + dlt
        ok = (src >= 0) & (src < t) & ((src < s_lat) == (tpos < s_lat))
        acc = acc + jnp.where(ok, pltpu.roll(x, (-dlt) % t, 0), 0.0) * w_ref[j:j + 1, :]
    y = _silu(acc)
    if normalize:
        y = y * lax.rsqrt(jnp.sum(y * y, axis=-1, keepdims=True) + 1e-6) * scale
    o_ref[...] = y


def _gdn_conv(p, conv_w, s_lat, part):
    nb, t, _ = p.shape
    col = (COL_GDN_Q, COL_GDN_K, COL_GDN_V)[part]
    return pl.pallas_call(
        functools.partial(_gdn_conv_kernel, s_lat=s_lat, normalize=part < 2,
                          scale=GDN_DK ** -0.5 if part == 0 else 1.0),
        grid=(nb, GDN_HEADS),
        in_specs=[pl.BlockSpec((None, t, 128), lambda b, g: (b, 0, col // 128 + g)),
                  pl.BlockSpec((GDN_CONV, 128), lambda b, g: (0, part * GDN_HEADS + g))],
        out_specs=pl.BlockSpec((None, t, 128), lambda b, g: (b, 0, g)),
        out_shape=jax.ShapeDtypeStruct((nb, t, GDN_HEADS * 128), F32),
        compiler_params=_cparams(("arbitrary", "arbitrary")),
        name=f"gdn_conv{part}",
    )(p, conv_w)


def _gdn_chunk_kernel(q_ref, k_ref, v_ref, sm_ref, alog_ref, dtb_ref,
                      u_ref, w_ref, qd_ref, kd_ref, a_ref, gl_ref, *, tc):
    g = pl.program_id(1)
    cs = GDN_CHUNK
    lane = lax.broadcasted_iota(I32, (1, 128), 1)
    subl = lax.broadcasted_iota(I32, (128, 1), 0)
    ti = lax.broadcasted_iota(I32, (cs, cs), 0)
    ui = lax.broadcasted_iota(I32, (cs, cs), 1)
    eye = jnp.where(ti == ui, 1.0, 0.0)
    for c in range(tc // cs):
        rows = slice(c * cs, (c + 1) * cs)
        q, k, v = q_ref[rows, :], k_ref[rows, :], v_ref[rows, :]
        sm = sm_ref[rows, :]
        sm_t = sm.T
        kb = k.astype(BF16)
        kk = _dot_nt(kb, kb)
        qk = _dot_nt(q.astype(BF16), kb)
        for dr in range(2):
            ca = SMALL_A + GDN_HEADS * dr + g
            cbeta = SMALL_B + GDN_HEADS * dr + g
            a_col = jnp.sum(jnp.where(lane == ca, sm, 0.0), axis=1, keepdims=True)
            b_col = jnp.sum(jnp.where(lane == cbeta, sm, 0.0), axis=1, keepdims=True)
            a_row = jnp.sum(jnp.where(subl == ca, sm_t, 0.0), axis=0, keepdims=True)
            neg_rate = -jnp.exp(alog_ref[dr])
            g_col = neg_rate * _softplus(a_col + dtb_ref[dr])
            g_row = neg_rate * _softplus(a_row + dtb_ref[dr])
            beta = _sigmoid(b_col)
            incl = (ui <= ti) if dr == 0 else (ui >= ti)
            strict = (ui < ti) if dr == 0 else (ui > ti)
            gc_col = jnp.sum(jnp.where(incl, g_row, 0.0), axis=1, keepdims=True)
            incl_t = (ti <= ui) if dr == 0 else (ti >= ui)
            gc_row = jnp.sum(jnp.where(incl_t, g_col, 0.0), axis=0, keepdims=True)
            decay = jnp.exp(jnp.where(incl, gc_col - gc_row, NEG))
            m = jnp.where(strict, beta * kk * decay, 0.0)
            inv = eye - m
            mk = _bdot(m, m)
            for lvl in range(5):
                inv = inv + _bdot(inv, mk)
                if lvl < 4:
                    mk = _bdot(mk, mk)
            e_gc = jnp.exp(gc_col)
            rhs = jnp.concatenate([v * beta, k * (beta * e_gc)], axis=1)
            sol = _bdot(inv, rhs)
            gc_last = gc_col[cs - 1:cs] if dr == 0 else gc_col[0:1]
            u_ref[dr, rows, :] = sol[:, :GDN_DV]
            w_ref[dr, rows, :] = sol[:, GDN_DV:].astype(BF16)
            qd_ref[dr, rows, :] = (q * e_gc).astype(BF16)
            kd_ref[dr, rows, :] = (k * jnp.exp(gc_last - gc_col)).astype(BF16)
            a_ref[dr, rows, :] = (qk * decay).astype(BF16)
            gl_ref[dr, c * 8:(c + 1) * 8, :] = jnp.broadcast_to(jnp.exp(gc_last), (8, 128))


def _gdn_chunk(qn, kn, vn, p, a_log, dt_bias):
    nb, t, _ = p.shape
    tc = 256
    nch = t // GDN_CHUNK
    rb = lambda: pl.BlockSpec((None, tc, 128), lambda b, g, i: (b, i, g))
    par = pl.BlockSpec((2, None, 1, 1), lambda b, g, i: (0, g, 0, 0))
    ob = lambda w: pl.BlockSpec((None, None, 2, tc, w), lambda b, g, i: (b, g, 0, i, 0))
    shp = lambda w, dt: jax.ShapeDtypeStruct((nb, GDN_HEADS, 2, t, w), dt)
    return pl.pallas_call(
        functools.partial(_gdn_chunk_kernel, tc=tc),
        grid=(nb, GDN_HEADS, t // tc),
        in_specs=[rb(), rb(), rb(),
                  pl.BlockSpec((None, tc, 128), lambda b, g, i: (b, i, COL_SMALL // 128)), par, par],
        out_specs=[ob(128), ob(128), ob(128), ob(128), ob(GDN_CHUNK),
                   pl.BlockSpec((None, None, 2, (tc // GDN_CHUNK) * 8, 128), lambda b, g, i: (b, g, 0, i, 0))],
        out_shape=[shp(128, F32), shp(128, BF16), shp(128, BF16), shp(128, BF16), shp(GDN_CHUNK, BF16),
                   jax.ShapeDtypeStruct((nb, GDN_HEADS, 2, nch * 8, 128), F32)],
        compiler_params=_cparams(("arbitrary", "arbitrary", "arbitrary")),
        name="gdn_chunk",
    )(qn, kn, vn, p, a_log.reshape(2, GDN_HEADS, 1, 1).astype(F32), dt_bias.reshape(2, GDN_HEADS, 1, 1).astype(F32))


def _gdn_scan_kernel(u_ref, w_ref, qd_ref, kd_ref, a_ref, gl_ref, o_ref, st_ref, *, s_lat, l_ctx):
    cs = GDN_CHUNK
    ncl, ncc = s_lat // cs, l_ctx // cs
    o_ref[...] = jnp.zeros(o_ref.shape, F32)
    st_ref[...] = jnp.zeros(st_ref.shape, F32)

    def step(i, carry):
        in_ctx = i < ncc
        cf = jnp.where(in_ctx, ncl + i, i - ncc)
        cb = jnp.where(in_ctx, ncl + ncc - 1 - i, ncl - 1 - (i - ncc))
        for dr, c in enumerate((cf, cb)):
            r = pl.multiple_of(c * cs, cs)
            st = st_ref[dr]
            stb = st.astype(BF16)
            ws = jnp.dot(w_ref[dr, pl.ds(r, cs), :], stb, preferred_element_type=F32)
            qs = jnp.dot(qd_ref[dr, pl.ds(r, cs), :], stb, preferred_element_type=F32)
            vnb = (u_ref[dr, pl.ds(r, cs), :] - ws).astype(BF16)
            o_ref[pl.ds(r, cs), :] += qs + jnp.dot(a_ref[dr, pl.ds(r, cs), :], vnb, preferred_element_type=F32)
            upd = _dot_tn(kd_ref[dr, pl.ds(r, cs), :], vnb)
            gl = gl_ref[dr, pl.ds(pl.multiple_of(c * 8, 8), 8), :]
            st_ref[dr] = st * gl[0:1] + upd
        return carry

    lax.fori_loop(0, ncl + ncc, step, 0)


def _gdn_scan(u, w, qd, kd, a, gl, s_lat, l_ctx):
    nb, _, _, t, _ = u.shape
    nch8 = gl.shape[3]
    ib = lambda wd: pl.BlockSpec((None, None, 2, t, wd), lambda b, g: (b, g, 0, 0, 0))
    return pl.pallas_call(
        functools.partial(_gdn_scan_kernel, s_lat=s_lat, l_ctx=l_ctx),
        grid=(nb, GDN_HEADS),
        in_specs=[ib(128), ib(128), ib(128), ib(128), ib(GDN_CHUNK),
                  pl.BlockSpec((None, None, 2, nch8, 128), lambda b, g: (b, g, 0, 0, 0))],
        out_specs=pl.BlockSpec((None, t, 128), lambda b, g: (b, 0, g)),
        out_shape=jax.ShapeDtypeStruct((nb, t, GDN_HEADS * GDN_DV), F32),
        scratch_shapes=[pltpu.VMEM((2, GDN_DK, GDN_DV), F32)],
        compiler_params=_cparams(("arbitrary", "arbitrary")),
        name="gdn_scan",
    )(u, w, qd, kd, a, gl)


def _head_norm_gate(o, gate, nw):
    y = o * lax.rsqrt(jnp.mean(o * o, axis=-1, keepdims=True) + NORM_EPS) * nw
    return y * _silu(gate)


def _outproj_kernel(na_ref, gla_ref, gg_ref, gdn_ref, gz_ref, gnw_ref, dnw_ref, w_ref, h_ref, mod_ref,
                    o_ref, lhs_ref, *, s_lat, tm, nb):
    b, i, j = pl.program_id(0), pl.program_id(1), pl.program_id(2)

    @pl.when(j == 0)
    def _():
        lhs_ref[:, 0:NA_W] = na_ref[...].astype(BF16)
        for hh in range(GLA_HEADS):
            sl = slice(hh * GLA_DV, (hh + 1) * GLA_DV)
            y = _head_norm_gate(gla_ref[:, sl], gg_ref[:, sl], gnw_ref[...])
            lhs_ref[:, NA_W + hh * GLA_DV:NA_W + (hh + 1) * GLA_DV] = y.astype(BF16)
        base = NA_W + GLA_HEADS * GLA_DV
        for hh in range(GDN_HEADS):
            sl = slice(hh * GDN_DV, (hh + 1) * GDN_DV)
            y = _head_norm_gate(gdn_ref[:, sl], gz_ref[:, sl], dnw_ref[...])
            lhs_ref[:, base + hh * GDN_DV:base + (hh + 1) * GDN_DV] = y.astype(BF16)

    mix = jnp.dot(lhs_ref[...], w_ref[...].astype(BF16), preferred_element_type=F32)
    row = i * tm + lax.broadcasted_iota(I32, (tm, 1), 0)
    gate = jnp.where(row < s_lat, mod_ref[b][2:3], mod_ref[nb][2:3])
    o_ref[...] = h_ref[...] + gate * mix


def _outproj(o_na, o_gla, o_gdn, p, gla_nw, gdn_nw, w_out, h, mod, s_lat):
    nb, t, d = h.shape
    tm = t // 8
    tn = min(512, d)
    gw = GLA_HEADS * GLA_DV
    rb = lambda w, c=0: pl.BlockSpec((None, tm, w), lambda b, i, j, c=c, w=w: (b, i, c // w))
    return pl.pallas_call(
        functools.partial(_outproj_kernel, s_lat=s_lat, tm=tm, nb=nb),
        grid=(nb, t // tm, d // tn),
        in_specs=[rb(NA_W), rb(gw), rb(gw, COL_GLA_G), rb(gw), rb(gw, COL_GDN_Z),
                  pl.BlockSpec((1, GLA_DV), lambda b, i, j: (0, 0)),
                  pl.BlockSpec((1, GDN_DV), lambda b, i, j: (0, 0)),
                  pl.BlockSpec((w_out.shape[0], tn), lambda b, i, j: (0, j)),
                  pl.BlockSpec((None, tm, tn), lambda b, i, j: (b, i, j)),
                  pl.BlockSpec((8, 6, tn), lambda b, i, j: (0, 0, j))],
        out_specs=pl.BlockSpec((None, tm, tn), lambda b, i, j: (b, i, j)),
        out_shape=jax.ShapeDtypeStruct((nb, t, d), F32),
        scratch_shapes=[pltpu.VMEM((tm, w_out.shape[0]), BF16)],
        compiler_params=_cparams(("arbitrary", "arbitrary", "arbitrary")),
        name="outproj",
    )(o_na, o_gla, p, o_gdn, p, gla_nw, gdn_nw, w_out, h, mod)


def _router_kernel(h_ref, nw_ref, mod_ref, wr_ref, hn_ref, aff_ref, *, s_lat, tm, nb):
    b, i = pl.program_id(0), pl.program_id(1)
    hn = _mod_norm(h_ref[...], nw_ref[...], mod_ref[b], mod_ref[nb], i * tm, s_lat, 3)
    hn_ref[...] = hn.astype(BF16)
    logits = lax.dot_general(wr_ref[...], hn, (((1,), (1,)), ((), ())), precision=HI,
                             preferred_element_type=F32)
    e = jnp.exp(logits - logits.max(axis=0, keepdims=True))
    aff_ref[...] = e / e.sum(axis=0, keepdims=True)


def _router(h, nw, mod, w_router_t, s_lat):
    nb, t, d = h.shape
    tm = 256
    return pl.pallas_call(
        functools.partial(_router_kernel, s_lat=s_lat, tm=tm, nb=nb),
        grid=(nb, t // tm),
        in_specs=[pl.BlockSpec((None, tm, d), lambda b, i: (b, i, 0)),
                  pl.BlockSpec((1, d), lambda b, i: (0, 0)),
                  pl.BlockSpec((8, 6, d), lambda b, i: (0, 0, 0)),
                  pl.BlockSpec((N_EXPERTS, d), lambda b, i: (0, 0))],
        out_specs=[pl.BlockSpec((None, tm, d), lambda b, i: (b, i, 0)),
                   pl.BlockSpec((None, N_EXPERTS, tm), lambda b, i: (b, 0, i))],
        out_shape=[jax.ShapeDtypeStruct((nb, t, d), BF16),
                   jax.ShapeDtypeStruct((nb, N_EXPERTS, t), F32)],
        compiler_params=_cparams(("arbitrary", "arbitrary")),
        name="router",
    )(h, nw, mod, w_router_t)


def _lane_prefix(x01, tri):
    n = x01.shape[1]
    lane = lax.broadcasted_iota(I32, (1, 128), 1)
    off = jnp.zeros((x01.shape[0], 1), F32)
    before = jnp.zeros((x01.shape[0], 128), F32)
    outs = []
    for blk in range(n // 128):
        before = jnp.where(lane == blk, off, before)
        cb = jnp.dot(x01[:, blk * 128:(blk + 1) * 128].astype(BF16), tri, preferred_element_type=F32) + off
        outs.append(cb)
        off = cb[:, 127:128]
    return jnp.concatenate(outs, axis=1), before


def _select_kernel(aff_ref, tri_ref, slot_ref, s0_ref, *, cap):
    aff = aff_ref[...]
    ne = aff.shape[0]

    def bisect(_, c):
        lo, hi = c
        mid = 0.5 * (lo + hi)
        cnt = jnp.sum(jnp.where(aff >= mid, 1, 0), axis=1, keepdims=True)
        ok = cnt >= cap
        return jnp.where(ok, mid, lo), jnp.where(ok, hi, mid)

    lo, _ = lax.fori_loop(0, SELECT_STEPS, bisect, (jnp.zeros((ne, 1), F32), jnp.full((ne, 1), 2.0, F32)))
    thr = jnp.min(jnp.where(aff >= lo, aff, 2.0), axis=1, keepdims=True)
    gt = aff > thr
    eq = aff == thr
    need = (cap - jnp.sum(jnp.where(gt, 1, 0), axis=1, keepdims=True)).astype(F32)
    eq_f = jnp.where(eq, 1.0, 0.0)
    eq_incl, _ = _lane_prefix(eq_f, tri_ref[...])
    sel = gt | (eq & ((eq_incl - eq_f) < need))
    sel_f = jnp.where(sel, 1.0, 0.0)
    pos_incl, before = _lane_prefix(sel_f, tri_ref[...])
    slot_ref[...] = jnp.where(sel, pos_incl - 1.0, -1.0).astype(I32)
    s0_ref[...] = before.astype(I32)


def _select(aff, n, blk_idx, cap):
    nb = aff.shape[0]
    tri = jnp.asarray(np.arange(128)[:, None] <= np.arange(128)[None, :], BF16)
    return pl.pallas_call(
        functools.partial(_select_kernel, cap=cap),
        grid=(nb,),
        in_specs=[pl.BlockSpec((None, N_EXPERTS, n), lambda b: (b, 0, blk_idx)),
                  pl.BlockSpec((128, 128), lambda b: (0, 0))],
        out_specs=[pl.BlockSpec((None, N_EXPERTS, n), lambda b: (b, 0, 0)),
                   pl.BlockSpec((None, N_EXPERTS, 128), lambda b: (b, 0, 0))],
        out_shape=[jax.ShapeDtypeStruct((nb, N_EXPERTS, n), I32),
                   jax.ShapeDtypeStruct((nb, N_EXPERTS, 128), I32)],
        compiler_params=_cparams(("arbitrary",)),
        name="select",
    )(aff, tri)


SELECT_STEPS = 152
COMPACT_WIN = 128 + 8


def _compact_kernel(s0_ref, hn_ref, slot_ref, aff_ref, xs_ref, g_ref, acc_ref, gacc_ref, *, n, cap):
    b, e = pl.program_id(0), pl.program_id(1)
    win = COMPACT_WIN
    acc_ref[...] = jnp.zeros(acc_ref.shape, F32)
    gacc_ref[...] = jnp.zeros(gacc_ref.shape, F32)
    jj = lax.broadcasted_iota(I32, (win, 128), 0)

    def body(blk, carry):
        s0 = s0_ref[(b * N_EXPERTS + e) * 128 + blk]
        s0a = pl.multiple_of((s0 // 8) * 8, 8)
        t0 = pl.multiple_of(blk * 128, 128)
        sl = slot_ref[pl.ds(blk, 1), :]
        af = aff_ref[pl.ds(blk, 1), :]
        hit = jj == (sl - s0a)
        acc_ref[pl.ds(s0a, win), :] += jnp.dot(jnp.where(hit, 1.0, 0.0).astype(BF16),
                                               hn_ref[pl.ds(t0, 128), :], preferred_element_type=F32)
        gv = jnp.sum(jnp.where(hit, af, 0.0), axis=1, keepdims=True)
        gacc_ref[pl.ds(s0a, win), :] += jnp.broadcast_to(gv, (win, 128))
        return carry

    lax.fori_loop(0, n // 128, body, 0)
    xs_ref[...] = acc_ref[0:cap, :].astype(BF16)
    g_ref[...] = gacc_ref[0:cap, :]


def _compact(hn, slot, aff, s0, n, cap, row_blk):
    nb, _, d = hn.shape
    nblk = n // 128
    slot4 = slot.reshape(nb, N_EXPERTS, nblk, 128)
    aff4 = aff.reshape(nb, N_EXPERTS, nblk, 128)
    grid_spec = pltpu.PrefetchScalarGridSpec(
        num_scalar_prefetch=1,
        grid=(nb, N_EXPERTS),
        in_specs=[pl.BlockSpec((None, n, d), lambda b, e, s: (b, row_blk, 0)),
                  pl.BlockSpec((None, None, nblk, 128), lambda b, e, s: (b, e, 0, 0)),
                  pl.BlockSpec((None, None, nblk, 128), lambda b, e, s: (b, e, 0, 0))],
        out_specs=[pl.BlockSpec((None, cap, d), lambda b, e, s: (e, b, 0)),
                   pl.BlockSpec((None, cap, 128), lambda b, e, s: (e, b, 0))],
        scratch_shapes=[pltpu.VMEM((cap + COMPACT_WIN, d), F32), pltpu.VMEM((cap + COMPACT_WIN, 128), F32)],
    )
    return pl.pallas_call(
        functools.partial(_compact_kernel, n=n, cap=cap),
        grid_spec=grid_spec,
        out_shape=[jax.ShapeDtypeStruct((N_EXPERTS, nb * cap, d), BF16),
                   jax.ShapeDtypeStruct((N_EXPERTS, nb * cap, 128), F32)],
        compiler_params=_cparams(("arbitrary", "arbitrary")),
        name="compact",
    )(s0.reshape(-1), hn, slot4, aff4)


def _expert_kernel(*refs, n_in):
    xs = refs[0:n_in]
    gs = refs[n_in:2 * n_in]
    wg_ref, wu_ref, wd_ref = refs[2 * n_in:2 * n_in + 3]
    ys = refs[2 * n_in + 3:3 * n_in + 3]
    accs = refs[3 * n_in + 3:]
    f = pl.program_id(1)
    last = pl.num_programs(1) - 1
    wg = wg_ref[...].astype(BF16)
    wu = wu_ref[...].astype(BF16)
    wd = wd_ref[...].astype(BF16)
    for x_ref, g_ref, y_ref, acc_ref in zip(xs, gs, ys, accs):
        x = x_ref[...]
        a = jnp.dot(x, wg, preferred_element_type=F32)
        u = jnp.dot(x, wu, preferred_element_type=F32)
        part = jnp.dot((_silu(a) * u).astype(BF16), wd, preferred_element_type=F32)

        @pl.when(f == 0)
        def _():
            acc_ref[...] = part

        @pl.when(f > 0)
        def _():
            acc_ref[...] += part

        @pl.when(f == last)
        def _():
            gate = g_ref[...]
            for cblk in range(acc_ref.shape[1] // 128):
                sl = slice(cblk * 128, (cblk + 1) * 128)
                y_ref[:, sl] = (acc_ref[:, sl] * gate).astype(BF16)


def _experts(xs_list, g_list, w_gate, w_up, w_down):
    ne, d, ff = w_gate.shape
    tf = 256
    n_in = len(xs_list)
    xspec = lambda m, w: pl.BlockSpec((None, m, w), lambda e, f: (e, 0, 0))
    return pl.pallas_call(
        functools.partial(_expert_kernel, n_in=n_in),
        grid=(ne, ff // tf),
        in_specs=([xspec(x.shape[1], d) for x in xs_list] + [xspec(g.shape[1], 128) for g in g_list]
                  + [pl.BlockSpec((None, d, tf), lambda e, f: (e, 0, f)),
                     pl.BlockSpec((None, d, tf), lambda e, f: (e, 0, f)),
                     pl.BlockSpec((None, tf, d), lambda e, f: (e, f, 0))]),
        out_specs=[xspec(x.shape[1], d) for x in xs_list],
        out_shape=[jax.ShapeDtypeStruct(x.shape, BF16) for x in xs_list],
        scratch_shapes=[pltpu.VMEM((x.shape[1], d), F32) for x in xs_list],
        compiler_params=_cparams(("arbitrary", "arbitrary")),
        name="experts",
    )(*xs_list, *g_list, w_gate, w_up, w_down)


def _combine_kernel(kb_ref, *refs, two, wb, cap, mod_row_static):
    if two:
        y0_ref, y1_ref, slot_ref, h_ref, mod_ref, o_ref, acc_ref = refs
    else:
        y0_ref, slot_ref, h_ref, mod_ref, o_ref, acc_ref = refs
    b, i, e = pl.program_id(0), pl.program_id(1), pl.program_id(2)

    @pl.when(e == 0)
    def _():
        acc_ref[...] = jnp.zeros(acc_ref.shape, F32)

    kb = kb_ref[(b * N_EXPERTS + e) * 128 + i]
    sl = slot_ref[...]
    j = jnp.where(sl >= 0, sl + (b * cap - kb * wb), -1)
    wrows = wb * (2 if two else 1)
    hit = lax.broadcasted_iota(I32, (wrows, 128), 0) == j
    yw = jnp.concatenate([y0_ref[...], y1_ref[...]], axis=0) if two else y0_ref[...]
    acc_ref[...] += _dot_tn(jnp.where(hit, 1.0, 0.0).astype(BF16), yw)

    @pl.when(e == pl.num_programs(2) - 1)
    def _():
        mrow = mod_ref[b] if mod_row_static is None else mod_ref[mod_row_static]
        o_ref[...] = h_ref[...] + mrow[5:6] * acc_ref[...]


def _combine(y, slot, s0, h, mod, n, cap, tok_blk0, is_ctx):
    nb, t, d = h.shape
    nblk = n // 128
    rows = y.shape[1]
    two = rows >= 256
    wb = 128 if two else rows
    nwb = rows // wb
    slot5 = slot.reshape(nb, N_EXPERTS, nblk, 1, 128)
    kb = ((jnp.arange(nb, dtype=I32)[:, None, None] * cap + s0) // wb).astype(I32) if two else jnp.zeros_like(s0)
    yspec = lambda nxt: pl.BlockSpec(
        (None, wb, d), lambda b, i, e, kbr, nxt=nxt: (e, jnp.minimum(kbr[(b * N_EXPERTS + e) * 128 + i] + nxt, nwb - 1), 0))
    in_specs = [yspec(0)] + ([yspec(1)] if two else []) + [
        pl.BlockSpec((None, None, None, 1, 128), lambda b, i, e, kbr: (b, e, i, 0, 0)),
        pl.BlockSpec((None, 128, d), lambda b, i, e, kbr: (b, tok_blk0 + i, 0)),
        pl.BlockSpec((8, 6, d), lambda b, i, e, kbr: (0, 0, 0))]
    grid_spec = pltpu.PrefetchScalarGridSpec(
        num_scalar_prefetch=1,
        grid=(nb, nblk, N_EXPERTS),
        in_specs=in_specs,
        out_specs=pl.BlockSpec((None, 128, d), lambda b, i, e, kbr: (b, tok_blk0 + i, 0)),
        scratch_shapes=[pltpu.VMEM((128, d), F32)],
    )
    n_y = 2 if two else 1
    return pl.pallas_call(
        functools.partial(_combine_kernel, two=two, wb=wb, cap=cap, mod_row_static=nb if is_ctx else None),
        grid_spec=grid_spec,
        out_shape=jax.ShapeDtypeStruct((nb, t, d), F32),
        input_output_aliases={1 + n_y + 1: 0},
        compiler_params=_cparams(("arbitrary", "arbitrary", "arbitrary")),
        name="combine",
    )(kb.reshape(-1), *([y] * n_y), slot5, h, mod)


def _moe(h, nw, mod, w_router, w_gate, w_up, w_down, s_lat, l_ctx, with_ctx):
    hn, aff = _router(h, nw, mod, w_router.T, s_lat)
    streams = [(s_lat, 0, 0, False)]
    if with_ctx:
        streams.append((l_ctx, s_lat // l_ctx, s_lat // 128, True))
    sel = []
    for n, blk_idx, _, _ in streams:
        cap = EC_CAPACITY * n // N_EXPERTS
        slot, s0 = _select(aff, n, blk_idx, cap)
        aff_s = lax.slice_in_dim(aff, blk_idx * n, blk_idx * n + n, axis=2)
        xs, gates = _compact(hn, slot, aff_s, s0, n, cap, blk_idx)
        sel.append((slot, s0, xs, gates, cap))
    ys = _experts([s[2] for s in sel], [s[3] for s in sel], w_gate, w_up, w_down)
    for (n, _, tok_blk0, is_ctx), (slot, s0, _, _, cap), y in zip(streams, sel, ys):
        h = _combine(y, slot, s0, h, mod, n, cap, tok_blk0, is_ctx)
    return h


def _final_norm_kernel(h_ref, w_ref, o_ref):
    x = h_ref[...]
    o_ref[...] = x * lax.rsqrt(jnp.mean(x * x, axis=-1, keepdims=True) + NORM_EPS) * w_ref[...]


def _final_norm(h, w, s_lat):
    nb, _, d = h.shape
    tm = 512
    return pl.pallas_call(
        _final_norm_kernel,
        grid=(nb, s_lat // tm),
        in_specs=[pl.BlockSpec((None, tm, d), lambda b, i: (b, i, 0)),
                  pl.BlockSpec((1, d), lambda b, i: (0, 0))],
        out_specs=pl.BlockSpec((None, tm, d), lambda b, i: (b, i, 0)),
        out_shape=jax.ShapeDtypeStruct((nb, s_lat, d), F32),
        compiler_params=_cparams(("arbitrary", "arbitrary")),
        name="final_norm",
    )(h, w)


def _reorder_w_in(w_in):
    d = w_in.shape[0]
    lr0 = COL_GDN_Q
    qkv0 = lr0 + 2 * GLA_RANK
    ab0 = qkv0 + 3 * GDN_HEADS * GDN_DK + GDN_HEADS * GDN_DV
    n_in = ab0 + 4 * GDN_HEADS
    return jnp.concatenate([w_in[:, :lr0], w_in[:, qkv0:ab0], w_in[:, lr0:qkv0], w_in[:, ab0:n_in],
                            jnp.zeros((d, NP_COLS - n_in), w_in.dtype)], axis=1)


def _token_mixers(h, nw, mod, w_in, w_out, rpb, gate_w, gate_b, gla_nw, conv_w, a_log, dt_bias, gdn_nw,
                  rope, s_lat, l_ctx):
    p = _inproj(h, nw, mod, _reorder_w_in(w_in), s_lat)
    o_na = _na(p, _na_bias_table(rpb), s_lat, l_ctx)
    gw_p = jnp.zeros((2, 128, GLA_HEADS * GLA_DK), F32)
    for dr in range(2):
        gw_p = gw_p.at[dr, dr * GLA_RANK:(dr + 1) * GLA_RANK].set(gate_w[dr])
    qr, kr, cf, cb = _gla_prep(p, rope[0], rope[1], gw_p, gate_b[:, None, :], s_lat)
    o_gla = _gla_scan(qr, kr, cf, cb, p, s_lat, l_ctx)
    qn, kn, vn = (_gdn_conv(p, conv_w, s_lat, part) for part in range(3))
    o_gdn = _gdn_scan(*_gdn_chunk(qn, kn, vn, p, a_log, dt_bias), s_lat, l_ctx)
    return _outproj(o_na, o_gla, o_gdn, p, gla_nw[None, :], gdn_nw[None, :], w_out, h, mod, s_lat)


def kernel(x, c, ctx, c_ctx, w_ada, b_ada, norm_mix_w, norm_ffn_w, w_in, w_out, na_rpb, gla_gate_w, gla_gate_b,
           gla_norm_w, gdn_conv_w, gdn_a_log, gdn_dt_bias, gdn_norm_w, w_router, w_exp_gate, w_exp_up,
           w_exp_down, final_norm_w):
    nb, s_lat, d = x.shape
    l_ctx = ctx.shape[1]
    depth = w_ada.shape[0]
    assert nb < 8 and s_lat % 256 == 0 and l_ctx == 256 and s_lat // GRID_W >= NA_WIN_R
    h = jnp.concatenate([x, ctx], axis=1)
    cvec = jnp.zeros((8, d), F32).at[:nb].set(c).at[nb].set(c_ctx)
    mods = _ada(cvec, w_ada, b_ada).reshape(depth, 8, 6, d)
    rope = _rope_tables(s_lat, l_ctx)
    for l in range(depth):
        h = _token_mixers(h, norm_mix_w[l][None, :], mods[l], w_in[l], w_out[l], na_rpb[l], gla_gate_w[l],
                          gla_gate_b[l], gla_norm_w[l], gdn_conv_w[l], gdn_a_log[l], gdn_dt_bias[l],
                          gdn_norm_w[l], rope, s_lat, l_ctx)
        h = _moe(h, norm_ffn_w[l][None, :], mods[l], w_router[l], w_exp_gate[l], w_exp_up[l], w_exp_down[l],
                 s_lat, l_ctx, with_ctx=l < depth - 1)
    return _final_norm(h, final_norm_w[None, :], s_lat)
```

```python
import functools

import numpy as np
import jax
import jax.numpy as jnp
from jax import lax
from jax.experimental import pallas as pl
from jax.experimental.pallas import tpu as pltpu

F32 = jnp.float32
BF16 = jnp.bfloat16
I32 = jnp.int32
HI = lax.Precision.HIGHEST

GRID_W = 64
NA_HEADS, NA_DH, NA_WIN_R, NA_WIN_C = 16, 64, 8, 16
GLA_HEADS, GLA_DK, GLA_DV, GLA_RANK, GLA_TAU = 4, 64, 128, 16, 16.0
GLA_BLK = 16
GDN_HEADS, GDN_DK, GDN_DV, GDN_CONV, GDN_CHUNK = 4, 128, 128, 5, 64
N_EXPERTS, EC_CAPACITY = 16, 2
ROPE_BASE = 10000.0
NORM_EPS = 1e-6
NEG = -1e30

NA_W = NA_HEADS * NA_DH
COL_NA_Q, COL_NA_K, COL_NA_V = 0, NA_W, 2 * NA_W
COL_GLA_Q = 3 * NA_W
COL_GLA_K = COL_GLA_Q + GLA_HEADS * GLA_DK
COL_GLA_V = COL_GLA_K + GLA_HEADS * GLA_DK
COL_GLA_G = COL_GLA_V + GLA_HEADS * GLA_DV
COL_GDN_Q = COL_GLA_G + GLA_HEADS * GLA_DV
COL_GDN_K = COL_GDN_Q + GDN_HEADS * GDN_DK
COL_GDN_V = COL_GDN_K + GDN_HEADS * GDN_DK
COL_GDN_Z = COL_GDN_V + GDN_HEADS * GDN_DV
COL_SMALL = COL_GDN_Z + GDN_HEADS * GDN_DV
SMALL_A, SMALL_B = 2 * GLA_RANK, 2 * GLA_RANK + 2 * GDN_HEADS
INPROJ_TN = 512
NP_COLS = 7168

VMEM_LIMIT = 56 * 1024 * 1024


def _cparams(sem):
    return pltpu.CompilerParams(dimension_semantics=sem, vmem_limit_bytes=VMEM_LIMIT)


def _sigmoid(x):
    return 1.0 / (1.0 + jnp.exp(-x))


def _silu(x):
    return x * _sigmoid(x)


def _softplus(x):
    return jnp.maximum(x, 0.0) + jnp.log(1.0 + jnp.exp(-jnp.abs(x)))


def _bdot(a, b):
    return jnp.dot(a.astype(BF16), b.astype(BF16), preferred_element_type=F32)


def _dot_nt(a, b):
    return lax.dot_general(a, b, (((1,), (1,)), ((), ())), preferred_element_type=F32)


def _dot_tn(a, b):
    return lax.dot_general(a, b, (((0,), (0,)), ((), ())), preferred_element_type=F32)


def _dot01(m01, x):
    hi = x.astype(BF16)
    r1 = x - hi.astype(F32)
    mid = r1.astype(BF16)
    lo = (r1 - mid.astype(F32)).astype(BF16)
    d = lambda p: jnp.dot(m01, p, preferred_element_type=F32)
    return d(hi) + d(mid) + d(lo)


def _ada_kernel(c_ref, w_ref, b_ref, o_ref):
    o_ref[...] = jnp.dot(_silu(c_ref[...]), w_ref[...], precision=HI,
                         preferred_element_type=F32) + b_ref[...]


def _ada(cvec, w_ada, b_ada):
    depth, d, n6 = w_ada.shape
    tn = 1024 if n6 % 1024 == 0 else 512
    assert n6 % tn == 0
    return pl.pallas_call(
        _ada_kernel,
        grid=(depth, n6 // tn),
        in_specs=[pl.BlockSpec((8, d), lambda l, j: (0, 0)),
                  pl.BlockSpec((None, d, tn), lambda l, j: (l, 0, j)),
                  pl.BlockSpec((None, 1, tn), lambda l, j: (l, 0, j))],
        out_specs=pl.BlockSpec((None, 8, tn), lambda l, j: (l, 0, j)),
        out_shape=jax.ShapeDtypeStruct((depth, 8, n6), F32),
        compiler_params=_cparams(("arbitrary", "arbitrary")),
        name="ada",
    )(cvec, w_ada, b_ada.reshape(depth, 1, n6))


def _mod_norm(x, nw, mod_b, mod_c, row0, s_lat, k_shift):
    ms = jnp.mean(x * x, axis=-1, keepdims=True)
    y = x * lax.rsqrt(ms + NORM_EPS) * nw
    row = row0 + lax.broadcasted_iota(I32, (x.shape[0], 1), 0)
    is_lat = row < s_lat
    shift = jnp.where(is_lat, mod_b[k_shift:k_shift + 1], mod_c[k_shift:k_shift + 1])
    scale = jnp.where(is_lat, mod_b[k_shift + 1:k_shift + 2], mod_c[k_shift + 1:k_shift + 2])
    return y * (1.0 + scale) + shift


def _inproj_kernel(x_ref, nw_ref, mod_ref, w_ref, o_ref, xn_ref, *, s_lat, tm, nb):
    b, i, j = pl.program_id(0), pl.program_id(1), pl.program_id(2)

    @pl.when(j == 0)
    def _():
        rc = tm // 4

        def chunk(ci, carry):
            r = pl.multiple_of(ci * rc, 16)
            xn = _mod_norm(x_ref[pl.ds(r, rc), :], nw_ref[...], mod_ref[b], mod_ref[nb], i * tm + r, s_lat, 0)
            xn_ref[pl.ds(r, rc), :] = xn.astype(BF16)
            return carry

        lax.fori_loop(0, 4, chunk, 0)

    o_ref[...] = jnp.dot(xn_ref[...], w_ref[...].astype(BF16), preferred_element_type=F32)


def _inproj(h, nw, mod, w_p, s_lat):
    nb, t, d = h.shape
    tm = t // 4
    tn = INPROJ_TN
    npc = w_p.shape[1]
    return pl.pallas_call(
        functools.partial(_inproj_kernel, s_lat=s_lat, tm=tm, nb=nb),
        grid=(nb, t // tm, npc // tn),
        in_specs=[pl.BlockSpec((None, tm, d), lambda b, i, j: (b, i, 0)),
                  pl.BlockSpec((1, d), lambda b, i, j: (0, 0)),
                  pl.BlockSpec((8, 6, d), lambda b, i, j: (0, 0, 0)),
                  pl.BlockSpec((d, tn), lambda b, i, j: (0, j))],
        out_specs=pl.BlockSpec((None, tm, tn), lambda b, i, j: (b, i, j)),
        out_shape=jax.ShapeDtypeStruct((nb, t, npc), F32),
        scratch_shapes=[pltpu.VMEM((tm, d), BF16)],
        compiler_params=_cparams(("arbitrary", "arbitrary", "arbitrary")),
        name="inproj",
    )(h, nw, mod, w_p)


NA_NDR = 2 * NA_WIN_R - 1
NA_NDC = 2 * NA_WIN_C - 1


def _na_bias_kernel(rpb_ref, onehot_ref, mask_ref, o_ref):
    t = jnp.dot(rpb_ref[...], onehot_ref[...], precision=HI, preferred_element_type=F32)
    o_ref[...] = jnp.where(mask_ref[...] > 0.0, t, NEG)


def _na_bias_table(rpb):
    nh = rpb.shape[0]
    cq = np.arange(GRID_W)
    dc = np.clip(cq[None, :] - cq[:, None] + NA_WIN_C - 1, 0, NA_NDC - 1)
    cstart = np.clip(cq - NA_WIN_C // 2, 0, GRID_W - NA_WIN_C)
    colmask = (cq[None, :] >= cstart[:, None]) & (cq[None, :] < cstart[:, None] + NA_WIN_C)
    onehot = jnp.asarray(np.arange(128)[:, None] == dc.reshape(1, -1), F32)
    mask = jnp.asarray(colmask.reshape(1, -1), F32)
    rpb_p = jnp.zeros((nh, 16, 128), F32).at[:, :NA_NDR, :NA_NDC].set(rpb)
    full = pl.pallas_call(
        _na_bias_kernel,
        grid=(nh,),
        in_specs=[pl.BlockSpec((None, 16, 128), lambda h: (h, 0, 0)),
                  pl.BlockSpec((128, GRID_W * GRID_W), lambda h: (0, 0)),
                  pl.BlockSpec((1, GRID_W * GRID_W), lambda h: (0, 0))],
        out_specs=pl.BlockSpec((None, 16, GRID_W * GRID_W), lambda h: (h, 0, 0)),
        out_shape=jax.ShapeDtypeStruct((nh, 16, GRID_W * GRID_W), F32),
        compiler_params=_cparams(("arbitrary",)),
        name="na_bias",
    )(rpb_p, onehot, mask)
    t15 = full[:, :NA_NDR].reshape(nh, NA_NDR, GRID_W, GRID_W)
    return jnp.concatenate([t15[:, :-1], t15[:, 1:]], axis=-1)


def _softmax_pv_many(s_lists, v_lists):
    n = len(s_lists)
    ms = []
    for sl in s_lists:
        m = sl[0].max(axis=-1, keepdims=True)
        for s in sl[1:]:
            m = jnp.maximum(m, s.max(axis=-1, keepdims=True))
        ms.append(m)
    ps = [[jnp.exp(s - ms[c]) for s in s_lists[c]] for c in range(n)]
    dens = []
    for c in range(n):
        den = ps[c][0].sum(axis=-1, keepdims=True)
        for p in ps[c][1:]:
            den = den + p.sum(axis=-1, keepdims=True)
        dens.append(den)
    outs = []
    for c in range(n):
        o = jnp.dot(ps[c][0].astype(BF16), v_lists[c][0], preferred_element_type=F32)
        for p, v in zip(ps[c][1:], v_lists[c][1:]):
            o = o + jnp.dot(p.astype(BF16), v, preferred_element_type=F32)
        outs.append(o)
    return [o / den for o, den in zip(outs, dens)]


NA_ROWS_PER_STEP = 2


def _na_kernel(q_ref, k_ref, v_ref, bias_ref, o_ref, kb_ref, vb_ref, *, s_lat, l_ctx):
    rows = s_lat // GRID_W
    nwin = NA_WIN_R * GRID_W
    scale = NA_DH ** -0.5
    kb_ref[...] = k_ref[...].astype(BF16)
    vb_ref[...] = v_ref[...].astype(BF16)
    lane = lax.broadcasted_iota(I32, (1, 2 * NA_DH), 1)
    head_lanes = (lane < NA_DH, lane >= NA_DH)
    kc = kb_ref[pl.ds(s_lat, l_ctx), :]
    vc = vb_ref[pl.ds(s_lat, l_ctx), :]

    qc = q_ref[pl.ds(s_lat, l_ctx), :] * scale
    qhs = [jnp.where(head_lanes[hh], qc, 0.0).astype(BF16) for hh in range(2)]
    outs = _softmax_pv_many([[_dot_nt(qh, kc)] for qh in qhs], [[vc], [vc]])
    o_ref[pl.ds(s_lat, l_ctx), :] = jnp.where(head_lanes[0], outs[0], outs[1])

    def body(it, carry):
        s_lists, v_lists, starts = [], [], []
        for rr in range(NA_ROWS_PER_STEP):
            r = it * NA_ROWS_PER_STEP + rr
            r0 = jnp.clip(r - NA_WIN_R // 2, 0, rows - NA_WIN_R)
            d = r - r0
            qs = pl.multiple_of(r * GRID_W, GRID_W)
            ks = pl.multiple_of(r0 * GRID_W, GRID_W)
            starts.append(qs)
            q = q_ref[pl.ds(qs, GRID_W), :] * scale
            kw = kb_ref[pl.ds(ks, nwin), :]
            vw = vb_ref[pl.ds(ks, nwin), :]
            for hh in range(2):
                qh = jnp.where(head_lanes[hh], q, 0.0).astype(BF16)
                bias = jnp.concatenate([bias_ref[hh, 2 * m + NA_WIN_R - 1 - d] for m in range(NA_WIN_R // 2)], axis=1)
                s_lists.append([_dot_nt(qh, kw) + bias, _dot_nt(qh, kc)])
                v_lists.append([vw, vc])
        res = _softmax_pv_many(s_lists, v_lists)
        for rr in range(NA_ROWS_PER_STEP):
            o_ref[pl.ds(starts[rr], GRID_W), :] = jnp.where(head_lanes[0], res[2 * rr], res[2 * rr + 1])
        return carry

    lax.fori_loop(0, rows // NA_ROWS_PER_STEP, body, 0)


def _na(p, bias_tbl, s_lat, l_ctx):
    nb, t, _ = p.shape
    blk = lambda off: pl.BlockSpec((None, t, 128), lambda b, g, off=off: (b, 0, off // 128 + g))
    return pl.pallas_call(
        functools.partial(_na_kernel, s_lat=s_lat, l_ctx=l_ctx),
        grid=(nb, NA_HEADS // 2),
        in_specs=[blk(COL_NA_Q), blk(COL_NA_K), blk(COL_NA_V),
                  pl.BlockSpec((2, NA_NDR - 1, GRID_W, 2 * GRID_W), lambda b, g: (g, 0, 0, 0))],
        out_specs=pl.BlockSpec((None, t, 128), lambda b, g: (b, 0, g)),
        out_shape=jax.ShapeDtypeStruct((nb, t, NA_W), F32),
        scratch_shapes=[pltpu.VMEM((t, 128), BF16), pltpu.VMEM((t, 128), BF16)],
        compiler_params=_cparams(("arbitrary", "arbitrary")),
        name="na",
    )(p, p, p, bias_tbl)


def _rope_tables(s_lat, l_ctx):
    width = GLA_HEADS * GLA_DK
    nf = GLA_DK // 4
    pos = np.arange(s_lat)
    lane = np.arange(width)
    sub = lane % GLA_DK
    freqs = ROPE_BASE ** (-jnp.arange(nf, dtype=F32) / nf)
    p_sel = jnp.where((sub < GLA_DK // 2)[None, :], jnp.asarray(pos // GRID_W, F32)[:, None],
                      jnp.asarray(pos % GRID_W, F32)[:, None])
    ang = p_sel * freqs[sub % nf][None, :]
    sign = np.where((sub % (2 * nf)) < nf, -1.0, 1.0).astype(np.float32)
    cos = jnp.concatenate([jnp.cos(ang), jnp.ones((l_ctx, width), F32)], axis=0)
    sin = jnp.concatenate([jnp.sin(ang) * sign[None, :], jnp.zeros((l_ctx, width), F32)], axis=0)
    return cos, sin


def _block_tri(n, blk):
    i = np.arange(n)
    same = (i[:, None] // blk) == (i[None, :] // blk)
    lower = same & (i[None, :] <= i[:, None])
    upper = same & (i[None, :] >= i[:, None])
    return jnp.asarray(lower, BF16), jnp.asarray(upper, BF16)


def _gla_prep_kernel(q_ref, k_ref, sm_ref, cos_ref, sin_ref, gw_ref, gb_ref, tl_ref, tu_ref,
                     qr_ref, kr_ref, cf_ref, cb_ref):
    width = GLA_HEADS * GLA_DK
    nf = GLA_DK // 4
    lane = lax.broadcasted_iota(I32, (1, width), 1)
    first = (lane % (2 * nf)) < nf
    cos, sin = cos_ref[...], sin_ref[...]

    def rope(x):
        swapped = jnp.where(first, pltpu.roll(x, width - nf, 1), pltpu.roll(x, nf, 1))
        return x * cos + swapped * sin

    qr_ref[...] = rope(q_ref[...]) * (GLA_DK ** -0.5)
    kr_ref[...] = rope(k_ref[...])
    sm = sm_ref[...]
    for dr, (tri_ref, out_ref) in enumerate(((tl_ref, cf_ref), (tu_ref, cb_ref))):
        z = jnp.dot(sm, gw_ref[dr], precision=HI, preferred_element_type=F32) + gb_ref[dr]
        log_a = (jnp.minimum(z, 0.0) - jnp.log(1.0 + jnp.exp(-jnp.abs(z)))) * (1.0 / GLA_TAU)
        out_ref[...] = _dot01(tri_ref[...], log_a)


def _gla_prep(p, cos, sin, gw_p, gb, s_lat):
    nb, t, _ = p.shape
    width = GLA_HEADS * GLA_DK
    tr = 256
    tl, tu = _block_tri(tr, GLA_BLK)
    row_blk = lambda w, off: pl.BlockSpec((None, tr, w), lambda b, i, off=off, w=w: (b, i, off // w))
    tab = pl.BlockSpec((tr, width), lambda b, i: (i, 0))
    full = lambda shp: pl.BlockSpec(shp, lambda b, i, n=len(shp): (0,) * n)
    out = pl.BlockSpec((None, tr, width), lambda b, i: (b, i, 0))
    return pl.pallas_call(
        _gla_prep_kernel,
        grid=(nb, t // tr),
        in_specs=[row_blk(width, COL_GLA_Q), row_blk(width, COL_GLA_K), row_blk(128, COL_SMALL),
                  tab, tab, full((2, 128, width)), full((2, 1, width)), full((tr, tr)), full((tr, tr))],
        out_specs=[out] * 4,
        out_shape=[jax.ShapeDtypeStruct((nb, t, width), F32)] * 4,
        compiler_params=_cparams(("arbitrary", "arbitrary")),
        name="gla_prep",
    )(p, p, p, cos, sin, gw_p, gb, tl, tu)


def _gla_scan_kernel(qr_ref, kr_ref, cf_ref, cb_ref, v_ref, r3_ref, o_ref, st_ref, *, s_lat, l_ctx):
    nlat, nctx = s_lat // GLA_BLK, l_ctx // GLA_BLK
    o_ref[...] = jnp.zeros(o_ref.shape, F32)
    st_ref[...] = jnp.zeros(st_ref.shape, F32)
    sub = lax.broadcasted_iota(I32, (GLA_BLK, 2 * GLA_DK), 0)
    bd = (lax.broadcasted_iota(I32, (2 * GLA_DV, 2 * GLA_DK), 0) // GLA_DV
          == lax.broadcasted_iota(I32, (2 * GLA_DV, 2 * GLA_DK), 1) // GLA_DK)

    def step(i, carry):
        in_ctx = i < nctx
        jf = jnp.where(in_ctx, nlat + i, i - nctx)
        jb = jnp.where(in_ctx, nlat + nctx - 1 - i, nlat - 1 - (i - nctx))
        dirs = (0, 1)
        rs = [pl.multiple_of(j * GLA_BLK, GLA_BLK) for j in (jf, jb)]
        qs = [qr_ref[pl.ds(r, GLA_BLK), :] for r in rs]
        ks = [kr_ref[pl.ds(r, GLA_BLK), :] for r in rs]
        cums = [c_ref[pl.ds(r, GLA_BLK), :] for c_ref, r in zip((cf_ref, cb_ref), rs)]
        vs = [v_ref[pl.ds(r, GLA_BLK), :] for r in rs]
        tots = [cums[0][GLA_BLK - 1:GLA_BLK], cums[1][0:1]]
        sts = [st_ref[dr] for dr in dirs]
        qds = [(qs[dr] * jnp.exp(cums[dr])).astype(BF16) for dr in dirs]
        kds = [(ks[dr] * jnp.exp(tots[dr] - cums[dr])).astype(BF16) for dr in dirs]
        o_states = [_dot_nt(qds[dr], sts[dr].astype(BF16)) for dr in dirs]
        upds = [_dot_tn(vs[dr].astype(BF16), kds[dr]) for dr in dirs]
        xs = []
        for dr in dirs:
            tiles = []
            for s in range(GLA_BLK):
                valid = (sub >= s) if dr == 0 else (sub <= s)
                w = jnp.exp(jnp.where(valid, cums[dr] - cums[dr][s:s + 1], NEG))
                tiles.append((w * qs[dr] * ks[dr][s:s + 1]).astype(BF16))
            xs.append(jnp.concatenate(tiles, axis=0))
        ress = [jnp.dot(x, r3_ref[...], preferred_element_type=F32) for x in xs]
        for dr in dirs:
            st_ref[dr] = sts[dr] * jnp.exp(tots[dr]) + jnp.where(bd, upds[dr], 0.0)
        for dr in dirs:
            o_diag = ress[dr][0:GLA_BLK] * vs[dr][0:1]
            for s in range(1, GLA_BLK):
                o_diag = o_diag + ress[dr][s * GLA_BLK:(s + 1) * GLA_BLK] * vs[dr][s:s + 1]
            o_ref[pl.ds(rs[dr], GLA_BLK), :] += o_states[dr] + o_diag
        return carry

    lax.fori_loop(0, nlat + nctx, step, 0)


def _gla_scan(qr, kr, cf, cb, p, s_lat, l_ctx):
    nb, t, _ = p.shape
    lanes = 2 * GLA_DK
    r3 = jnp.asarray((np.arange(lanes)[:, None] // GLA_DK) == (np.arange(2 * GLA_DV)[None, :] // GLA_DV), BF16)
    blk = pl.BlockSpec((None, t, lanes), lambda b, g: (b, 0, g))
    return pl.pallas_call(
        functools.partial(_gla_scan_kernel, s_lat=s_lat, l_ctx=l_ctx),
        grid=(nb, GLA_HEADS // 2),
        in_specs=[blk, blk, blk, blk,
                  pl.BlockSpec((None, t, 2 * GLA_DV), lambda b, g: (b, 0, COL_GLA_V // (2 * GLA_DV) + g)),
                  pl.BlockSpec((lanes, 2 * GLA_DV), lambda b, g: (0, 0))],
        out_specs=pl.BlockSpec((None, t, 2 * GLA_DV), lambda b, g: (b, 0, g)),
        out_shape=jax.ShapeDtypeStruct((nb, t, GLA_HEADS * GLA_DV), F32),
        scratch_shapes=[pltpu.VMEM((2, 2 * GLA_DV, lanes), F32)],
        compiler_params=_cparams(("arbitrary", "arbitrary")),
        name="gla_scan",
    )(qr, kr, cf, cb, p, r3)


def _gdn_conv_kernel(x_ref, w_ref, o_ref, *, s_lat, normalize, scale):
    t = x_ref.shape[0]
    x = x_ref[...]
    tpos = lax.broadcasted_iota(I32, (t, 1), 0)
    half = GDN_CONV // 2
    acc = x * w_ref[half:half + 1, :]
    for j in range(GDN_CONV):
        if j == half:
            continue
        dlt = j - half
        src = tpos + dlt
        ok = (src >= 0) & (src < t) & ((src < s_lat) == (tpos < s_lat))
        acc = acc + jnp.where(ok, pltpu.roll(x, (-dlt) % t, 0), 0.0) * w_ref[j:j + 1, :]
    y = _silu(acc)
    if normalize:
        y = y * lax.rsqrt(jnp.sum(y * y, axis=-1, keepdims=True) + 1e-6) * scale
    o_ref[...] = y


def _gdn_conv(p, conv_w, s_lat, part):
    nb, t, _ = p.shape
    col = (COL_GDN_Q, COL_GDN_K, COL_GDN_V)[part]
    return pl.pallas_call(
        functools.partial(_gdn_conv_kernel, s_lat=s_lat, normalize=part < 2,
                          scale=GDN_DK ** -0.5 if part == 0 else 1.0),
        grid=(nb, GDN_HEADS),
        in_specs=[pl.BlockSpec((None, t, 128), lambda b, g: (b, 0, col // 128 + g)),
                  pl.BlockSpec((GDN_CONV, 128), lambda b, g: (0, part * GDN_HEADS + g))],
        out_specs=pl.BlockSpec((None, t, 128), lambda b, g: (b, 0, g)),
        out_shape=jax.ShapeDtypeStruct((nb, t, GDN_HEADS * 128), F32),
        compiler_params=_cparams(("arbitrary", "arbitrary")),
        name=f"gdn_conv{part}",
    )(p, conv_w)


def _gdn_chunk_kernel(q_ref, k_ref, v_ref, sm_ref, alog_ref, dtb_ref,
                      u_ref, w_ref, qd_ref, kd_ref, a_ref, gl_ref, *, tc):
    g = pl.program_id(1)
    cs = GDN_CHUNK
    lane = lax.broadcasted_iota(I32, (1, 128), 1)
    subl = lax.broadcasted_iota(I32, (128, 1), 0)
    ti = lax.broadcasted_iota(I32, (tc, tc), 0)
    ui = lax.broadcasted_iota(I32, (tc, tc), 1)
    same = (ti // cs) == (ui // cs)
    eye = jnp.where(ti == ui, 1.0, 0.0)
    q, k, v = q_ref[...], k_ref[...], v_ref[...]
    sm = sm_ref[...]
    sm_t = sm.T
    kb = k.astype(BF16)
    kk = _dot_nt(kb, kb)
    qk = _dot_nt(q.astype(BF16), kb)
    ms, decays, betas, e_gcs, k_decs = [], [], [], [], []
    for dr in range(2):
        ca = SMALL_A + GDN_HEADS * dr + g
        cbeta = SMALL_B + GDN_HEADS * dr + g
        a_col = jnp.sum(jnp.where(lane == ca, sm, 0.0), axis=1, keepdims=True)
        b_col = jnp.sum(jnp.where(lane == cbeta, sm, 0.0), axis=1, keepdims=True)
        a_row = jnp.sum(jnp.where(subl == ca, sm_t, 0.0), axis=0, keepdims=True)
        neg_rate = -jnp.exp(alog_ref[dr])
        g_col = neg_rate * _softplus(a_col + dtb_ref[dr])
        g_row = neg_rate * _softplus(a_row + dtb_ref[dr])
        beta = _sigmoid(b_col)
        incl = same & ((ui <= ti) if dr == 0 else (ui >= ti))
        strict = same & ((ui < ti) if dr == 0 else (ui > ti))
        incl_t = same & ((ti <= ui) if dr == 0 else (ti >= ui))
        gc_col = jnp.sum(jnp.where(incl, g_row, 0.0), axis=1, keepdims=True)
        gc_row = jnp.sum(jnp.where(incl_t, g_col, 0.0), axis=0, keepdims=True)
        gc_tot = jnp.sum(jnp.where(same, g_row, 0.0), axis=1, keepdims=True)
        decay = jnp.exp(jnp.where(incl, gc_col - gc_row, NEG))
        ms.append(jnp.where(strict, beta * kk * decay, 0.0))
        decays.append(decay)
        betas.append(beta)
        e_gcs.append(jnp.exp(gc_col))
        k_decs.append(k * jnp.exp(gc_tot - gc_col))
        for c in range(tc // cs):
            gl_ref[dr, c * 8:(c + 1) * 8, :] = jnp.broadcast_to(jnp.exp(gc_tot[c * cs:c * cs + 1]), (8, 128))
    invs = [eye - m for m in ms]
    mks = [_bdot(m, m) for m in ms]
    for lvl in range(5):
        invs = [inv + _bdot(inv, mk) for inv, mk in zip(invs, mks)]
        if lvl < 4:
            mks = [_bdot(mk, mk) for mk in mks]
    sols = [_bdot(invs[dr], jnp.concatenate([v * betas[dr], k * (betas[dr] * e_gcs[dr])], axis=1)) for dr in range(2)]
    for dr in range(2):
        u_ref[dr] = sols[dr][:, :GDN_DV]
        w_ref[dr] = sols[dr][:, GDN_DV:].astype(BF16)
        qd_ref[dr] = (q * e_gcs[dr]).astype(BF16)
        kd_ref[dr] = k_decs[dr].astype(BF16)
        aqk = qk * decays[dr]
        for c in range(tc // cs):
            blk = aqk[c * cs:(c + 1) * cs]
            folded = blk[:, 0:128]
            for piece in range(1, tc // 128):
                folded = folded + blk[:, piece * 128:(piece + 1) * 128]
            a_ref[dr, c * cs:(c + 1) * cs, :] = folded.astype(BF16)


def _gdn_chunk(qn, kn, vn, p, a_log, dt_bias):
    nb, t, _ = p.shape
    tc = 256
    nch = t // GDN_CHUNK
    rb = lambda: pl.BlockSpec((None, tc, 128), lambda b, g, i: (b, i, g))
    par = pl.BlockSpec((2, None, 1, 1), lambda b, g, i: (0, g, 0, 0))
    ob = lambda w: pl.BlockSpec((None, None, 2, tc, w), lambda b, g, i: (b, g, 0, i, 0))
    shp = lambda w, dt: jax.ShapeDtypeStruct((nb, GDN_HEADS, 2, t, w), dt)
    return pl.pallas_call(
        functools.partial(_gdn_chunk_kernel, tc=tc),
        grid=(nb, GDN_HEADS, t // tc),
        in_specs=[rb(), rb(), rb(),
                  pl.BlockSpec((None, tc, 128), lambda b, g, i: (b, i, COL_SMALL // 128)), par, par],
        out_specs=[ob(128), ob(128), ob(128), ob(128), ob(128),
                   pl.BlockSpec((None, None, 2, (tc // GDN_CHUNK) * 8, 128), lambda b, g, i: (b, g, 0, i, 0))],
        out_shape=[shp(128, F32), shp(128, BF16), shp(128, BF16), shp(128, BF16), shp(128, BF16),
                   jax.ShapeDtypeStruct((nb, GDN_HEADS, 2, nch * 8, 128), F32)],
        compiler_params=_cparams(("arbitrary", "arbitrary", "arbitrary")),
        name="gdn_chunk",
    )(qn, kn, vn, p, a_log.reshape(2, GDN_HEADS, 1, 1).astype(F32), dt_bias.reshape(2, GDN_HEADS, 1, 1).astype(F32))


def _gdn_scan_kernel(u_ref, w_ref, qd_ref, kd_ref, a_ref, gl_ref, o_ref, st_ref, *, s_lat, l_ctx):
    cs = GDN_CHUNK
    ncl, ncc = s_lat // cs, l_ctx // cs
    o_ref[...] = jnp.zeros(o_ref.shape, F32)
    st_ref[...] = jnp.zeros(st_ref.shape, F32)

    def step(i, carry):
        in_ctx = i < ncc
        cf = jnp.where(in_ctx, ncl + i, i - ncc)
        cb = jnp.where(in_ctx, ncl + ncc - 1 - i, ncl - 1 - (i - ncc))
        dirs = (0, 1)
        rs = [pl.multiple_of(c * cs, cs) for c in (cf, cb)]
        sts = [st_ref[dr] for dr in dirs]
        stbs = [st.astype(BF16) for st in sts]
        wss = [jnp.dot(w_ref[dr, pl.ds(rs[dr], cs), :], stbs[dr], preferred_element_type=F32) for dr in dirs]
        qss = [jnp.dot(qd_ref[dr, pl.ds(rs[dr], cs), :], stbs[dr], preferred_element_type=F32) for dr in dirs]
        vnbs = [(u_ref[dr, pl.ds(rs[dr], cs), :] - wss[dr]).astype(BF16) for dr in dirs]
        avs = [jnp.dot(a_ref[dr, pl.ds(rs[dr], cs), :], jnp.concatenate([vnbs[dr], vnbs[dr]], axis=0),
                       preferred_element_type=F32) for dr in dirs]
        upds = [_dot_tn(kd_ref[dr, pl.ds(rs[dr], cs), :], vnbs[dr]) for dr in dirs]
        for dr, c in zip(dirs, (cf, cb)):
            gl = gl_ref[dr, pl.ds(pl.multiple_of(c * 8, 8), 8), :]
            st_ref[dr] = sts[dr] * gl[0:1] + upds[dr]
        for dr in dirs:
            o_ref[pl.ds(rs[dr], cs), :] += qss[dr] + avs[dr]
        return carry

    lax.fori_loop(0, ncl + ncc, step, 0)


def _gdn_scan(u, w, qd, kd, a, gl, s_lat, l_ctx):
    nb, _, _, t, _ = u.shape
    nch8 = gl.shape[3]
    ib = lambda wd: pl.BlockSpec((None, None, 2, t, wd), lambda b, g: (b, g, 0, 0, 0))
    return pl.pallas_call(
        functools.partial(_gdn_scan_kernel, s_lat=s_lat, l_ctx=l_ctx),
        grid=(nb, GDN_HEADS),
        in_specs=[ib(128), ib(128), ib(128), ib(128), ib(128),
                  pl.BlockSpec((None, None, 2, nch8, 128), lambda b, g: (b, g, 0, 0, 0))],
        out_specs=pl.BlockSpec((None, t, 128), lambda b, g: (b, 0, g)),
        out_shape=jax.ShapeDtypeStruct((nb, t, GDN_HEADS * GDN_DV), F32),
        scratch_shapes=[pltpu.VMEM((2, GDN_DK, GDN_DV), F32)],
        compiler_params=_cparams(("arbitrary", "arbitrary")),
        name="gdn_scan",
    )(u, w, qd, kd, a, gl)


def _head_norm_gate(o, gate, nw):
    y = o * lax.rsqrt(jnp.mean(o * o, axis=-1, keepdims=True) + NORM_EPS) * nw
    return y * _silu(gate)


def _outproj_kernel(na_ref, gla_ref, gg_ref, gdn_ref, gz_ref, gnw_ref, dnw_ref, w_ref, h_ref, mod_ref,
                    o_ref, lhs_ref, *, s_lat, tm, nb):
    b, i, j = pl.program_id(0), pl.program_id(1), pl.program_id(2)

    @pl.when(j == 0)
    def _():
        lhs_ref[:, 0:NA_W] = na_ref[...].astype(BF16)
        for hh in range(GLA_HEADS):
            sl = slice(hh * GLA_DV, (hh + 1) * GLA_DV)
            y = _head_norm_gate(gla_ref[:, sl], gg_ref[:, sl], gnw_ref[...])
            lhs_ref[:, NA_W + hh * GLA_DV:NA_W + (hh + 1) * GLA_DV] = y.astype(BF16)
        base = NA_W + GLA_HEADS * GLA_DV
        for hh in range(GDN_HEADS):
            sl = slice(hh * GDN_DV, (hh + 1) * GDN_DV)
            y = _head_norm_gate(gdn_ref[:, sl], gz_ref[:, sl], dnw_ref[...])
            lhs_ref[:, base + hh * GDN_DV:base + (hh + 1) * GDN_DV] = y.astype(BF16)

    mix = jnp.dot(lhs_ref[...], w_ref[...].astype(BF16), preferred_element_type=F32)
    row = i * tm + lax.broadcasted_iota(I32, (tm, 1), 0)
    gate = jnp.where(row < s_lat, mod_ref[b][2:3], mod_ref[nb][2:3])
    o_ref[...] = h_ref[...] + gate * mix


def _outproj(o_na, o_gla, o_gdn, p, gla_nw, gdn_nw, w_out, layer, h, mod, s_lat):
    nb, t, d = h.shape
    tm = t // 8
    tn = min(512, d)
    mix_w = w_out.shape[1]
    gw = GLA_HEADS * GLA_DV
    rb = lambda w, c=0: pl.BlockSpec((None, tm, w), lambda b, i, j, c=c, w=w: (b, i, c // w))
    return pl.pallas_call(
        functools.partial(_outproj_kernel, s_lat=s_lat, tm=tm, nb=nb),
        grid=(nb, t // tm, d // tn),
        in_specs=[rb(NA_W), rb(gw), rb(gw, COL_GLA_G), rb(gw), rb(gw, COL_GDN_Z),
                  pl.BlockSpec((1, GLA_DV), lambda b, i, j: (0, 0)),
                  pl.BlockSpec((1, GDN_DV), lambda b, i, j: (0, 0)),
                  pl.BlockSpec((None, mix_w, tn), lambda b, i, j: (layer, 0, j)),
                  pl.BlockSpec((None, tm, tn), lambda b, i, j: (b, i, j)),
                  pl.BlockSpec((8, 6, tn), lambda b, i, j: (0, 0, j))],
        out_specs=pl.BlockSpec((None, tm, tn), lambda b, i, j: (b, i, j)),
        out_shape=jax.ShapeDtypeStruct((nb, t, d), F32),
        scratch_shapes=[pltpu.VMEM((tm, mix_w), BF16)],
        compiler_params=_cparams(("arbitrary", "arbitrary", "arbitrary")),
        name="outproj",
    )(o_na, o_gla, p, o_gdn, p, gla_nw, gdn_nw, w_out, h, mod)


def _router_kernel(h_ref, nw_ref, mod_ref, wr_ref, hn_ref, aff_ref, *, s_lat, tm, nb):
    b, i = pl.program_id(0), pl.program_id(1)
    hn = _mod_norm(h_ref[...], nw_ref[...], mod_ref[b], mod_ref[nb], i * tm, s_lat, 3)
    hn_ref[...] = hn.astype(BF16)
    logits = lax.dot_general(wr_ref[...], hn, (((1,), (1,)), ((), ())), precision=HI,
                             preferred_element_type=F32)
    e = jnp.exp(logits - logits.max(axis=0, keepdims=True))
    aff_ref[...] = e / e.sum(axis=0, keepdims=True)


def _router(h, nw, mod, w_router_t, s_lat):
    nb, t, d = h.shape
    tm = 256
    return pl.pallas_call(
        functools.partial(_router_kernel, s_lat=s_lat, tm=tm, nb=nb),
        grid=(nb, t // tm),
        in_specs=[pl.BlockSpec((None, tm, d), lambda b, i: (b, i, 0)),
                  pl.BlockSpec((1, d), lambda b, i: (0, 0)),
                  pl.BlockSpec((8, 6, d), lambda b, i: (0, 0, 0)),
                  pl.BlockSpec((N_EXPERTS, d), lambda b, i: (0, 0))],
        out_specs=[pl.BlockSpec((None, tm, d), lambda b, i: (b, i, 0)),
                   pl.BlockSpec((None, N_EXPERTS, tm), lambda b, i: (b, 0, i))],
        out_shape=[jax.ShapeDtypeStruct((nb, t, d), BF16),
                   jax.ShapeDtypeStruct((nb, N_EXPERTS, t), F32)],
        compiler_params=_cparams(("arbitrary", "arbitrary")),
        name="router",
    )(h, nw, mod, w_router_t)


def _lane_prefix(x01, tri):
    n = x01.shape[1]
    lane = lax.broadcasted_iota(I32, (1, 128), 1)
    off = jnp.zeros((x01.shape[0], 1), F32)
    before = jnp.zeros((x01.shape[0], 128), F32)
    outs = []
    for blk in range(n // 128):
        before = jnp.where(lane == blk, off, before)
        cb = jnp.dot(x01[:, blk * 128:(blk + 1) * 128].astype(BF16), tri, preferred_element_type=F32) + off
        outs.append(cb)
        off = cb[:, 127:128]
    return jnp.concatenate(outs, axis=1), before


def _select_kernel(aff_ref, tri_ref, slot_ref, s0_ref, *, cap):
    aff = aff_ref[...]
    ne = aff.shape[0]

    def bisect(_, c):
        lo, hi = c
        mid = 0.5 * (lo + hi)
        cnt = jnp.sum(jnp.where(aff >= mid, 1, 0), axis=1, keepdims=True)
        ok = cnt >= cap
        return jnp.where(ok, mid, lo), jnp.where(ok, hi, mid)

    lo, _ = lax.fori_loop(0, SELECT_STEPS, bisect, (jnp.zeros((ne, 1), F32), jnp.full((ne, 1), 2.0, F32)))
    thr = jnp.min(jnp.where(aff >= lo, aff, 2.0), axis=1, keepdims=True)
    gt = aff > thr
    eq = aff == thr
    need = (cap - jnp.sum(jnp.where(gt, 1, 0), axis=1, keepdims=True)).astype(F32)
    eq_f = jnp.where(eq, 1.0, 0.0)
    eq_incl, _ = _lane_prefix(eq_f, tri_ref[...])
    sel = gt | (eq & ((eq_incl - eq_f) < need))
    sel_f = jnp.where(sel, 1.0, 0.0)
    pos_incl, before = _lane_prefix(sel_f, tri_ref[...])
    slot_ref[...] = jnp.where(sel, pos_incl - 1.0, -1.0).astype(I32)
    s0_ref[...] = before.astype(I32)


def _select(aff, n, blk_idx, cap):
    nb = aff.shape[0]
    tri = jnp.asarray(np.arange(128)[:, None] <= np.arange(128)[None, :], BF16)
    return pl.pallas_call(
        functools.partial(_select_kernel, cap=cap),
        grid=(nb,),
        in_specs=[pl.BlockSpec((None, N_EXPERTS, n), lambda b: (b, 0, blk_idx)),
                  pl.BlockSpec((128, 128), lambda b: (0, 0))],
        out_specs=[pl.BlockSpec((None, N_EXPERTS, n), lambda b: (b, 0, 0)),
                   pl.BlockSpec((None, N_EXPERTS, 128), lambda b: (b, 0, 0))],
        out_shape=[jax.ShapeDtypeStruct((nb, N_EXPERTS, n), I32),
                   jax.ShapeDtypeStruct((nb, N_EXPERTS, 128), I32)],
        compiler_params=_cparams(("arbitrary",)),
        name="select",
    )(aff, tri)


SELECT_STEPS = 152
COMPACT_WIN = 128 + 8
COMPACT_WIN_SMALL = 32
COMBINE_WIN = 256


def _compact_kernel(s0_ref, hn_ref, slot_ref, aff_ref, xs_ref, g_ref, acc_ref, gacc_ref, *, n, cap):
    b, e = pl.program_id(0), pl.program_id(1)
    nblk = n // 128
    acc_ref[...] = jnp.zeros(acc_ref.shape, F32)
    gacc_ref[...] = jnp.zeros(gacc_ref.shape, F32)

    def body(blk, carry):
        base = (b * N_EXPERTS + e) * 128
        s0 = s0_ref[base + blk]
        s1 = jnp.where(blk + 1 < nblk, s0_ref[base + jnp.minimum(blk + 1, nblk - 1)], cap)
        s0a = pl.multiple_of((s0 // 8) * 8, 8)
        t0 = pl.multiple_of(blk * 128, 128)

        def scatter_rows(win):
            sl = slot_ref[pl.ds(blk, 1), :]
            af = aff_ref[pl.ds(blk, 1), :]
            hit = lax.broadcasted_iota(I32, (win, 128), 0) == (sl - s0a)
            acc_ref[pl.ds(s0a, win), :] += jnp.dot(jnp.where(hit, 1.0, 0.0).astype(BF16),
                                                   hn_ref[pl.ds(t0, 128), :], preferred_element_type=F32)
            gv = jnp.sum(jnp.where(hit, af, 0.0), axis=1, keepdims=True)
            gacc_ref[pl.ds(s0a, win), :] += jnp.broadcast_to(gv, (win, 128))

        small = (s1 - s0a) <= COMPACT_WIN_SMALL

        @pl.when(small)
        def _():
            scatter_rows(COMPACT_WIN_SMALL)

        @pl.when(jnp.logical_not(small))
        def _():
            scatter_rows(COMPACT_WIN)

        return carry

    lax.fori_loop(0, nblk, body, 0)
    xs_ref[...] = acc_ref[0:cap, :].astype(BF16)
    g_ref[...] = gacc_ref[0:cap, :]


def _compact(hn, slot, aff, s0, n, cap, row_blk):
    nb, _, d = hn.shape
    nblk = n // 128
    slot4 = slot.reshape(nb, N_EXPERTS, nblk, 128)
    aff4 = aff.reshape(nb, N_EXPERTS, nblk, 128)
    grid_spec = pltpu.PrefetchScalarGridSpec(
        num_scalar_prefetch=1,
        grid=(nb, N_EXPERTS),
        in_specs=[pl.BlockSpec((None, n, d), lambda b, e, s: (b, row_blk, 0)),
                  pl.BlockSpec((None, None, nblk, 128), lambda b, e, s: (b, e, 0, 0)),
                  pl.BlockSpec((None, None, nblk, 128), lambda b, e, s: (b, e, 0, 0))],
        out_specs=[pl.BlockSpec((None, cap, d), lambda b, e, s: (e, b, 0)),
                   pl.BlockSpec((None, cap, 128), lambda b, e, s: (e, b, 0))],
        scratch_shapes=[pltpu.VMEM((cap + COMPACT_WIN, d), F32), pltpu.VMEM((cap + COMPACT_WIN, 128), F32)],
    )
    return pl.pallas_call(
        functools.partial(_compact_kernel, n=n, cap=cap),
        grid_spec=grid_spec,
        out_shape=[jax.ShapeDtypeStruct((N_EXPERTS, nb * cap, d), BF16),
                   jax.ShapeDtypeStruct((N_EXPERTS, nb * cap, 128), F32)],
        compiler_params=_cparams(("arbitrary", "arbitrary")),
        name="compact",
    )(s0.reshape(-1), hn, slot4, aff4)


def _expert_kernel(*refs, n_in):
    xs = refs[0:n_in]
    gs = refs[n_in:2 * n_in]
    wg_ref, wu_ref, wd_ref = refs[2 * n_in:2 * n_in + 3]
    ys = refs[2 * n_in + 3:3 * n_in + 3]
    accs = refs[3 * n_in + 3:]
    f = pl.program_id(1)
    last = pl.num_programs(1) - 1
    wg = wg_ref[...].astype(BF16)
    wu = wu_ref[...].astype(BF16)
    wd = wd_ref[...].astype(BF16)
    for x_ref, g_ref, y_ref, acc_ref in zip(xs, gs, ys, accs):
        x = x_ref[...]
        a = jnp.dot(x, wg, preferred_element_type=F32)
        u = jnp.dot(x, wu, preferred_element_type=F32)
        part = jnp.dot((_silu(a) * u).astype(BF16), wd, preferred_element_type=F32)

        @pl.when(f == 0)
        def _():
            acc_ref[...] = part

        @pl.when(f > 0)
        def _():
            acc_ref[...] += part

        @pl.when(f == last)
        def _():
            gate = g_ref[...]
            for cblk in range(acc_ref.shape[1] // 128):
                sl = slice(cblk * 128, (cblk + 1) * 128)
                y_ref[:, sl] = (acc_ref[:, sl] * gate).astype(BF16)


def _experts(xs_list, g_list, w_gate, w_up, w_down, layer):
    _, ne, d, ff = w_gate.shape
    tf = 256
    n_in = len(xs_list)
    xspec = lambda m, w: pl.BlockSpec((None, m, w), lambda e, f: (e, 0, 0))
    return pl.pallas_call(
        functools.partial(_expert_kernel, n_in=n_in),
        grid=(ne, ff // tf),
        in_specs=([xspec(x.shape[1], d) for x in xs_list] + [xspec(g.shape[1], 128) for g in g_list]
                  + [pl.BlockSpec((None, None, d, tf), lambda e, f: (layer, e, 0, f)),
                     pl.BlockSpec((None, None, d, tf), lambda e, f: (layer, e, 0, f)),
                     pl.BlockSpec((None, None, tf, d), lambda e, f: (layer, e, f, 0))]),
        out_specs=[xspec(x.shape[1], d) for x in xs_list],
        out_shape=[jax.ShapeDtypeStruct(x.shape, BF16) for x in xs_list],
        scratch_shapes=[pltpu.VMEM((x.shape[1], d), F32) for x in xs_list],
        compiler_params=_cparams(("arbitrary", "arbitrary")),
        name="experts",
    )(*xs_list, *g_list, w_gate, w_up, w_down)


def _combine_kernel(s0_ref, y_ref, slot_ref, h_ref, mod_ref, o_ref, *, cap, win, mod_row_static):
    b, i = pl.program_id(0), pl.program_id(2)
    slot_t = slot_ref[...]
    jj = lax.broadcasted_iota(I32, (1, win), 1)
    acc = None
    for e in range(N_EXPERTS):
        s0 = s0_ref[(b * N_EXPERTS + e) * 128 + i]
        s0a = pl.multiple_of(jnp.minimum((s0 // 16) * 16, cap - win), 16)
        hit = (slot_t[:, e:e + 1] - s0a) == jj
        part = jnp.dot(jnp.where(hit, 1.0, 0.0).astype(BF16), y_ref[e, pl.ds(s0a, win), :],
                       preferred_element_type=F32)
        acc = part if acc is None else acc + part
    mrow = mod_ref[b] if mod_row_static is None else mod_ref[mod_row_static]
    o_ref[...] = h_ref[...] + mrow[5:6] * acc


def _combine(y, slot, s0, h, mod, n, cap, tok_blk0, is_ctx):
    nb, t, d = h.shape
    nblk = n // 128
    dcols = d // 2 if d >= 256 else d
    win = min(cap, COMBINE_WIN)
    slot_t = jnp.swapaxes(slot, 1, 2)
    grid_spec = pltpu.PrefetchScalarGridSpec(
        num_scalar_prefetch=1,
        grid=(nb, d // dcols, nblk),
        in_specs=[pl.BlockSpec((N_EXPERTS, cap, dcols), lambda b, c, i, s: (0, b, c)),
                  pl.BlockSpec((None, 128, N_EXPERTS), lambda b, c, i, s: (b, i, 0)),
                  pl.BlockSpec((None, 128, dcols), lambda b, c, i, s: (b, tok_blk0 + i, c)),
                  pl.BlockSpec((8, 6, dcols), lambda b, c, i, s: (0, 0, c))],
        out_specs=pl.BlockSpec((None, 128, dcols), lambda b, c, i, s: (b, tok_blk0 + i, c)),
    )
    return pl.pallas_call(
        functools.partial(_combine_kernel, cap=cap, win=win, mod_row_static=nb if is_ctx else None),
        grid_spec=grid_spec,
        out_shape=jax.ShapeDtypeStruct((nb, t, d), F32),
        input_output_aliases={3: 0},
        compiler_params=_cparams(("arbitrary", "arbitrary", "arbitrary")),
        name="combine",
    )(s0.reshape(-1), y, slot_t, h, mod)


def _moe(h, nw, mod, w_router, w_gate, w_up, w_down, layer, s_lat, l_ctx, with_ctx):
    hn, aff = _router(h, nw, mod, w_router.T, s_lat)
    streams = [(s_lat, 0, 0, False)]
    if with_ctx:
        streams.append((l_ctx, s_lat // l_ctx, s_lat // 128, True))
    sel = []
    for n, blk_idx, _, _ in streams:
        cap = EC_CAPACITY * n // N_EXPERTS
        slot, s0 = _select(aff, n, blk_idx, cap)
        aff_s = lax.slice_in_dim(aff, blk_idx * n, blk_idx * n + n, axis=2)
        xs, gates = _compact(hn, slot, aff_s, s0, n, cap, blk_idx)
        sel.append((slot, s0, xs, gates, cap))
    ys = _experts([s[2] for s in sel], [s[3] for s in sel], w_gate, w_up, w_down, layer)
    for (n, _, tok_blk0, is_ctx), (slot, s0, _, _, cap), y in zip(streams, sel, ys):
        h = _combine(y, slot, s0, h, mod, n, cap, tok_blk0, is_ctx)
    return h


def _final_norm_kernel(h_ref, w_ref, o_ref):
    x = h_ref[...]
    o_ref[...] = x * lax.rsqrt(jnp.mean(x * x, axis=-1, keepdims=True) + NORM_EPS) * w_ref[...]


def _final_norm(h, w, s_lat):
    nb, _, d = h.shape
    tm = 512
    return pl.pallas_call(
        _final_norm_kernel,
        grid=(nb, s_lat // tm),
        in_specs=[pl.BlockSpec((None, tm, d), lambda b, i: (b, i, 0)),
                  pl.BlockSpec((1, d), lambda b, i: (0, 0))],
        out_specs=pl.BlockSpec((None, tm, d), lambda b, i: (b, i, 0)),
        out_shape=jax.ShapeDtypeStruct((nb, s_lat, d), F32),
        compiler_params=_cparams(("arbitrary", "arbitrary")),
        name="final_norm",
    )(h, w)


def _reorder_w_in(w_in):
    d = w_in.shape[0]
    lr0 = COL_GDN_Q
    qkv0 = lr0 + 2 * GLA_RANK
    ab0 = qkv0 + 3 * GDN_HEADS * GDN_DK + GDN_HEADS * GDN_DV
    n_in = ab0 + 4 * GDN_HEADS
    return jnp.concatenate([w_in[:, :lr0], w_in[:, qkv0:ab0], w_in[:, lr0:qkv0], w_in[:, ab0:n_in],
                            jnp.zeros((d, NP_COLS - n_in), w_in.dtype)], axis=1)


def _token_mixers(h, nw, mod, w_in, w_out, layer, rpb, gate_w, gate_b, gla_nw, conv_w, a_log, dt_bias, gdn_nw,
                  rope, s_lat, l_ctx):
    p = _inproj(h, nw, mod, _reorder_w_in(w_in), s_lat)
    o_na = _na(p, _na_bias_table(rpb), s_lat, l_ctx)
    gw_p = jnp.zeros((2, 128, GLA_HEADS * GLA_DK), F32)
    for dr in range(2):
        gw_p = gw_p.at[dr, dr * GLA_RANK:(dr + 1) * GLA_RANK].set(gate_w[dr])
    qr, kr, cf, cb = _gla_prep(p, rope[0], rope[1], gw_p, gate_b[:, None, :], s_lat)
    o_gla = _gla_scan(qr, kr, cf, cb, p, s_lat, l_ctx)
    qn, kn, vn = (_gdn_conv(p, conv_w, s_lat, part) for part in range(3))
    o_gdn = _gdn_scan(*_gdn_chunk(qn, kn, vn, p, a_log, dt_bias), s_lat, l_ctx)
    return _outproj(o_na, o_gla, o_gdn, p, gla_nw[None, :], gdn_nw[None, :], w_out, layer, h, mod, s_lat)


def kernel(x, c, ctx, c_ctx, w_ada, b_ada, norm_mix_w, norm_ffn_w, w_in, w_out, na_rpb, gla_gate_w, gla_gate_b,
           gla_norm_w, gdn_conv_w, gdn_a_log, gdn_dt_bias, gdn_norm_w, w_router, w_exp_gate, w_exp_up,
           w_exp_down, final_norm_w):
    nb, s_lat, d = x.shape
    l_ctx = ctx.shape[1]
    depth = w_ada.shape[0]
    assert nb < 8 and s_lat % 256 == 0 and l_ctx == 256 and s_lat // GRID_W >= NA_WIN_R
    h = jnp.concatenate([x, ctx], axis=1)
    cvec = jnp.zeros((8, d), F32).at[:nb].set(c).at[nb].set(c_ctx)
    mods = _ada(cvec, w_ada, b_ada).reshape(depth, 8, 6, d)
    rope = _rope_tables(s_lat, l_ctx)
    for l in range(depth):
        h = _token_mixers(h, norm_mix_w[l][None, :], mods[l], w_in[l], w_out, l, na_rpb[l], gla_gate_w[l],
                          gla_gate_b[l], gla_norm_w[l], gdn_conv_w[l], gdn_a_log[l], gdn_dt_bias[l],
                          gdn_norm_w[l], rope, s_lat, l_ctx)
        h = _moe(h, norm_ffn_w[l][None, :], mods[l], w_router[l], w_exp_gate, w_exp_up, w_exp_down, l,
                 s_lat, l_ctx, with_ctx=l < depth - 1)
    return _final_norm(h, final_norm_w[None, :], s_lat)
```

```python
import functools

import numpy as np
import jax
import jax.numpy as jnp
from jax import lax
from jax.experimental import pallas as pl
from jax.experimental.pallas import tpu as pltpu

F32 = jnp.float32
BF16 = jnp.bfloat16
I32 = jnp.int32
HI = lax.Precision.HIGHEST

GRID_W = 64
NA_HEADS, NA_DH, NA_WIN_R, NA_WIN_C = 16, 64, 8, 16
GLA_HEADS, GLA_DK, GLA_DV, GLA_RANK, GLA_TAU = 4, 64, 128, 16, 16.0
GLA_BLK = 16
GDN_HEADS, GDN_DK, GDN_DV, GDN_CONV, GDN_CHUNK = 4, 128, 128, 5, 64
N_EXPERTS, EC_CAPACITY = 16, 2
ROPE_BASE = 10000.0
NORM_EPS = 1e-6
NEG = -1e30

NA_W = NA_HEADS * NA_DH
COL_NA_Q, COL_NA_K, COL_NA_V = 0, NA_W, 2 * NA_W
COL_GLA_Q = 3 * NA_W
COL_GLA_K = COL_GLA_Q + GLA_HEADS * GLA_DK
COL_GLA_V = COL_GLA_K + GLA_HEADS * GLA_DK
COL_GLA_G = COL_GLA_V + GLA_HEADS * GLA_DV
COL_GDN_Q = COL_GLA_G + GLA_HEADS * GLA_DV
COL_GDN_K = COL_GDN_Q + GDN_HEADS * GDN_DK
COL_GDN_V = COL_GDN_K + GDN_HEADS * GDN_DK
COL_GDN_Z = COL_GDN_V + GDN_HEADS * GDN_DV
COL_SMALL = COL_GDN_Z + GDN_HEADS * GDN_DV
SMALL_A, SMALL_B = 2 * GLA_RANK, 2 * GLA_RANK + 2 * GDN_HEADS
INPROJ_TN = 512
NP_COLS = 7168

VMEM_LIMIT = 56 * 1024 * 1024


def _cparams(sem):
    return pltpu.CompilerParams(dimension_semantics=sem, vmem_limit_bytes=VMEM_LIMIT)


def _sigmoid(x):
    return 1.0 / (1.0 + jnp.exp(-x))


def _silu(x):
    return x * _sigmoid(x)


def _softplus(x):
    return jnp.maximum(x, 0.0) + jnp.log(1.0 + jnp.exp(-jnp.abs(x)))


def _bdot(a, b):
    return jnp.dot(a.astype(BF16), b.astype(BF16), preferred_element_type=F32)


def _dot_nt(a, b):
    return lax.dot_general(a, b, (((1,), (1,)), ((), ())), preferred_element_type=F32)


def _dot_tn(a, b):
    return lax.dot_general(a, b, (((0,), (0,)), ((), ())), preferred_element_type=F32)


def _dot01(m01, x):
    hi = x.astype(BF16)
    r1 = x - hi.astype(F32)
    mid = r1.astype(BF16)
    lo = (r1 - mid.astype(F32)).astype(BF16)
    d = lambda p: jnp.dot(m01, p, preferred_element_type=F32)
    return d(hi) + d(mid) + d(lo)


def _ada_kernel(c_ref, w_ref, b_ref, o_ref):
    o_ref[...] = _bdot(_silu(c_ref[...]), w_ref[...]) + b_ref[...]


def _ada(cvec, w_ada, b_ada):
    depth, d, n6 = w_ada.shape
    tn = 1024 if n6 % 1024 == 0 else 512
    assert n6 % tn == 0
    return pl.pallas_call(
        _ada_kernel,
        grid=(depth, n6 // tn),
        in_specs=[pl.BlockSpec((8, d), lambda l, j: (0, 0)),
                  pl.BlockSpec((None, d, tn), lambda l, j: (l, 0, j)),
                  pl.BlockSpec((None, 1, tn), lambda l, j: (l, 0, j))],
        out_specs=pl.BlockSpec((None, 8, tn), lambda l, j: (l, 0, j)),
        out_shape=jax.ShapeDtypeStruct((depth, 8, n6), F32),
        compiler_params=_cparams(("arbitrary", "arbitrary")),
        name="ada",
    )(cvec, w_ada, b_ada.reshape(depth, 1, n6))


def _mod_norm(x, nw, mod_b, mod_c, row0, s_lat, k_shift):
    ms = jnp.mean(x * x, axis=-1, keepdims=True)
    y = x * lax.rsqrt(ms + NORM_EPS) * nw
    row = row0 + lax.broadcasted_iota(I32, (x.shape[0], 1), 0)
    is_lat = row < s_lat
    shift = jnp.where(is_lat, mod_b[k_shift:k_shift + 1], mod_c[k_shift:k_shift + 1])
    scale = jnp.where(is_lat, mod_b[k_shift + 1:k_shift + 2], mod_c[k_shift + 1:k_shift + 2])
    return y * (1.0 + scale) + shift


def _inproj_kernel(x_ref, nw_ref, mod_ref, w_ref, o_ref, xn_ref, *, s_lat, tm, nb):
    b, i, j = pl.program_id(0), pl.program_id(1), pl.program_id(2)

    @pl.when(j == 0)
    def _():
        rc = tm // 4

        def chunk(ci, carry):
            r = pl.multiple_of(ci * rc, 16)
            xn = _mod_norm(x_ref[pl.ds(r, rc), :], nw_ref[...], mod_ref[b], mod_ref[nb], i * tm + r, s_lat, 0)
            xn_ref[pl.ds(r, rc), :] = xn.astype(BF16)
            return carry

        lax.fori_loop(0, 4, chunk, 0)

    o_ref[...] = jnp.dot(xn_ref[...], w_ref[...].astype(BF16), preferred_element_type=F32)


def _inproj(h, nw, mod, w_p, s_lat):
    nb, t, d = h.shape
    tm = t // 4
    tn = INPROJ_TN
    npc = w_p.shape[1]
    return pl.pallas_call(
        functools.partial(_inproj_kernel, s_lat=s_lat, tm=tm, nb=nb),
        grid=(nb, t // tm, npc // tn),
        in_specs=[pl.BlockSpec((None, tm, d), lambda b, i, j: (b, i, 0)),
                  pl.BlockSpec((1, d), lambda b, i, j: (0, 0)),
                  pl.BlockSpec((8, 6, d), lambda b, i, j: (0, 0, 0)),
                  pl.BlockSpec((d, tn), lambda b, i, j: (0, j))],
        out_specs=pl.BlockSpec((None, tm, tn), lambda b, i, j: (b, i, j)),
        out_shape=jax.ShapeDtypeStruct((nb, t, npc), F32),
        scratch_shapes=[pltpu.VMEM((tm, d), BF16)],
        compiler_params=_cparams(("arbitrary", "arbitrary", "arbitrary")),
        name="inproj",
    )(h, nw, mod, w_p)


NA_NDR = 2 * NA_WIN_R - 1
NA_NDC = 2 * NA_WIN_C - 1


def _na_bias_kernel(rpb_ref, onehot_ref, mask_ref, o_ref):
    t = jnp.dot(rpb_ref[...], onehot_ref[...], precision=HI, preferred_element_type=F32)
    o_ref[...] = jnp.where(mask_ref[...] > 0.0, t, NEG)


def _na_bias_table(rpb):
    nh = rpb.shape[0]
    cq = np.arange(GRID_W)
    dc = np.clip(cq[None, :] - cq[:, None] + NA_WIN_C - 1, 0, NA_NDC - 1)
    cstart = np.clip(cq - NA_WIN_C // 2, 0, GRID_W - NA_WIN_C)
    colmask = (cq[None, :] >= cstart[:, None]) & (cq[None, :] < cstart[:, None] + NA_WIN_C)
    onehot = jnp.asarray(np.arange(128)[:, None] == dc.reshape(1, -1), F32)
    mask = jnp.asarray(colmask.reshape(1, -1), F32)
    rpb_p = jnp.zeros((nh, 16, 128), F32).at[:, :NA_NDR, :NA_NDC].set(rpb)
    full = pl.pallas_call(
        _na_bias_kernel,
        grid=(nh,),
        in_specs=[pl.BlockSpec((None, 16, 128), lambda h: (h, 0, 0)),
                  pl.BlockSpec((128, GRID_W * GRID_W), lambda h: (0, 0)),
                  pl.BlockSpec((1, GRID_W * GRID_W), lambda h: (0, 0))],
        out_specs=pl.BlockSpec((None, 16, GRID_W * GRID_W), lambda h: (h, 0, 0)),
        out_shape=jax.ShapeDtypeStruct((nh, 16, GRID_W * GRID_W), F32),
        compiler_params=_cparams(("arbitrary",)),
        name="na_bias",
    )(rpb_p, onehot, mask)
    t15 = full[:, :NA_NDR].reshape(nh, NA_NDR, GRID_W, GRID_W)
    return jnp.concatenate([t15[:, :-1], t15[:, 1:]], axis=-1)


def _softmax_pv_many(s_lists, v_lists):
    n = len(s_lists)
    ms = []
    for sl in s_lists:
        m = sl[0].max(axis=-1, keepdims=True)
        for s in sl[1:]:
            m = jnp.maximum(m, s.max(axis=-1, keepdims=True))
        ms.append(m)
    ps = [[jnp.exp(s - ms[c]) for s in s_lists[c]] for c in range(n)]
    dens = []
    for c in range(n):
        den = ps[c][0].sum(axis=-1, keepdims=True)
        for p in ps[c][1:]:
            den = den + p.sum(axis=-1, keepdims=True)
        dens.append(den)
    outs = []
    for c in range(n):
        o = jnp.dot(ps[c][0].astype(BF16), v_lists[c][0], preferred_element_type=F32)
        for p, v in zip(ps[c][1:], v_lists[c][1:]):
            o = o + jnp.dot(p.astype(BF16), v, preferred_element_type=F32)
        outs.append(o)
    return [o / den for o, den in zip(outs, dens)]


NA_ROWS_PER_STEP = 2


def _na_kernel(q_ref, k_ref, v_ref, bias_ref, o_ref, kb_ref, vb_ref, *, s_lat, l_ctx):
    rows = s_lat // GRID_W
    nwin = NA_WIN_R * GRID_W
    scale = NA_DH ** -0.5
    kb_ref[...] = k_ref[...].astype(BF16)
    vb_ref[...] = v_ref[...].astype(BF16)
    lane = lax.broadcasted_iota(I32, (1, 2 * NA_DH), 1)
    head_lanes = (lane < NA_DH, lane >= NA_DH)
    kc = kb_ref[pl.ds(s_lat, l_ctx), :]
    vc = vb_ref[pl.ds(s_lat, l_ctx), :]

    qc = q_ref[pl.ds(s_lat, l_ctx), :] * scale
    qhs = [jnp.where(head_lanes[hh], qc, 0.0).astype(BF16) for hh in range(2)]
    outs = _softmax_pv_many([[_dot_nt(qh, kc)] for qh in qhs], [[vc], [vc]])
    o_ref[pl.ds(s_lat, l_ctx), :] = jnp.where(head_lanes[0], outs[0], outs[1])

    def body(it, carry):
        s_lists, v_lists, starts = [], [], []
        for rr in range(NA_ROWS_PER_STEP):
            r = it * NA_ROWS_PER_STEP + rr
            r0 = jnp.clip(r - NA_WIN_R // 2, 0, rows - NA_WIN_R)
            d = r - r0
            qs = pl.multiple_of(r * GRID_W, GRID_W)
            ks = pl.multiple_of(r0 * GRID_W, GRID_W)
            starts.append(qs)
            q = q_ref[pl.ds(qs, GRID_W), :] * scale
            kw = kb_ref[pl.ds(ks, nwin), :]
            vw = vb_ref[pl.ds(ks, nwin), :]
            for hh in range(2):
                qh = jnp.where(head_lanes[hh], q, 0.0).astype(BF16)
                bias = jnp.concatenate([bias_ref[hh, 2 * m + NA_WIN_R - 1 - d] for m in range(NA_WIN_R // 2)], axis=1)
                s_lists.append([_dot_nt(qh, kw) + bias, _dot_nt(qh, kc)])
                v_lists.append([vw, vc])
        res = _softmax_pv_many(s_lists, v_lists)
        for rr in range(NA_ROWS_PER_STEP):
            o_ref[pl.ds(starts[rr], GRID_W), :] = jnp.where(head_lanes[0], res[2 * rr], res[2 * rr + 1])
        return carry

    lax.fori_loop(0, rows // NA_ROWS_PER_STEP, body, 0)


def _na(p, bias_tbl, s_lat, l_ctx):
    nb, t, _ = p.shape
    blk = lambda off: pl.BlockSpec((None, t, 128), lambda b, g, off=off: (b, 0, off // 128 + g))
    return pl.pallas_call(
        functools.partial(_na_kernel, s_lat=s_lat, l_ctx=l_ctx),
        grid=(nb, NA_HEADS // 2),
        in_specs=[blk(COL_NA_Q), blk(COL_NA_K), blk(COL_NA_V),
                  pl.BlockSpec((2, NA_NDR - 1, GRID_W, 2 * GRID_W), lambda b, g: (g, 0, 0, 0))],
        out_specs=pl.BlockSpec((None, t, 128), lambda b, g: (b, 0, g)),
        out_shape=jax.ShapeDtypeStruct((nb, t, NA_W), F32),
        scratch_shapes=[pltpu.VMEM((t, 128), BF16), pltpu.VMEM((t, 128), BF16)],
        compiler_params=_cparams(("arbitrary", "arbitrary")),
        name="na",
    )(p, p, p, bias_tbl)


def _rope_tables(s_lat, l_ctx):
    width = GLA_HEADS * GLA_DK
    nf = GLA_DK // 4
    pos = np.arange(s_lat)
    lane = np.arange(width)
    sub = lane % GLA_DK
    freqs = ROPE_BASE ** (-jnp.arange(nf, dtype=F32) / nf)
    p_sel = jnp.where((sub < GLA_DK // 2)[None, :], jnp.asarray(pos // GRID_W, F32)[:, None],
                      jnp.asarray(pos % GRID_W, F32)[:, None])
    ang = p_sel * freqs[sub % nf][None, :]
    sign = np.where((sub % (2 * nf)) < nf, -1.0, 1.0).astype(np.float32)
    cos = jnp.concatenate([jnp.cos(ang), jnp.ones((l_ctx, width), F32)], axis=0)
    sin = jnp.concatenate([jnp.sin(ang) * sign[None, :], jnp.zeros((l_ctx, width), F32)], axis=0)
    return cos, sin


def _block_tri(n, blk):
    i = np.arange(n)
    same = (i[:, None] // blk) == (i[None, :] // blk)
    lower = same & (i[None, :] <= i[:, None])
    upper = same & (i[None, :] >= i[:, None])
    return jnp.asarray(lower, BF16), jnp.asarray(upper, BF16)


def _gla_prep_kernel(q_ref, k_ref, sm_ref, cos_ref, sin_ref, gw_ref, gb_ref, tl_ref, tu_ref,
                     qr_ref, kr_ref, cf_ref, cb_ref):
    width = GLA_HEADS * GLA_DK
    nf = GLA_DK // 4
    lane = lax.broadcasted_iota(I32, (1, width), 1)
    first = (lane % (2 * nf)) < nf
    cos, sin = cos_ref[...], sin_ref[...]

    def rope(x):
        swapped = jnp.where(first, pltpu.roll(x, width - nf, 1), pltpu.roll(x, nf, 1))
        return x * cos + swapped * sin

    qr_ref[...] = rope(q_ref[...]) * (GLA_DK ** -0.5)
    kr_ref[...] = rope(k_ref[...])
    sm = sm_ref[...]
    for dr, (tri_ref, out_ref) in enumerate(((tl_ref, cf_ref), (tu_ref, cb_ref))):
        z = jnp.dot(sm, gw_ref[dr], precision=HI, preferred_element_type=F32) + gb_ref[dr]
        log_a = (jnp.minimum(z, 0.0) - jnp.log(1.0 + jnp.exp(-jnp.abs(z)))) * (1.0 / GLA_TAU)
        out_ref[...] = _dot01(tri_ref[...], log_a)


def _gla_prep(p, cos, sin, gw_p, gb, s_lat):
    nb, t, _ = p.shape
    width = GLA_HEADS * GLA_DK
    tr = 256
    tl, tu = _block_tri(tr, GLA_BLK)
    row_blk = lambda w, off: pl.BlockSpec((None, tr, w), lambda b, i, off=off, w=w: (b, i, off // w))
    tab = pl.BlockSpec((tr, width), lambda b, i: (i, 0))
    full = lambda shp: pl.BlockSpec(shp, lambda b, i, n=len(shp): (0,) * n)
    out = pl.BlockSpec((None, tr, width), lambda b, i: (b, i, 0))
    return pl.pallas_call(
        _gla_prep_kernel,
        grid=(nb, t // tr),
        in_specs=[row_blk(width, COL_GLA_Q), row_blk(width, COL_GLA_K), row_blk(128, COL_SMALL),
                  tab, tab, full((2, 128, width)), full((2, 1, width)), full((tr, tr)), full((tr, tr))],
        out_specs=[out] * 4,
        out_shape=[jax.ShapeDtypeStruct((nb, t, width), F32)] * 4,
        compiler_params=_cparams(("arbitrary", "arbitrary")),
        name="gla_prep",
    )(p, p, p, cos, sin, gw_p, gb, tl, tu)


def _gla_scan_kernel(qr_ref, kr_ref, cf_ref, cb_ref, v_ref, r3_ref, o_ref, st_ref, *, s_lat, l_ctx):
    nlat, nctx = s_lat // GLA_BLK, l_ctx // GLA_BLK
    o_ref[...] = jnp.zeros(o_ref.shape, F32)
    st_ref[...] = jnp.zeros(st_ref.shape, F32)
    sub = lax.broadcasted_iota(I32, (GLA_BLK, 2 * GLA_DK), 0)
    bd = (lax.broadcasted_iota(I32, (2 * GLA_DV, 2 * GLA_DK), 0) // GLA_DV
          == lax.broadcasted_iota(I32, (2 * GLA_DV, 2 * GLA_DK), 1) // GLA_DK)

    def step(i, carry):
        in_ctx = i < nctx
        jf = jnp.where(in_ctx, nlat + i, i - nctx)
        jb = jnp.where(in_ctx, nlat + nctx - 1 - i, nlat - 1 - (i - nctx))
        dirs = (0, 1)
        rs = [pl.multiple_of(j * GLA_BLK, GLA_BLK) for j in (jf, jb)]
        qs = [qr_ref[pl.ds(r, GLA_BLK), :] for r in rs]
        ks = [kr_ref[pl.ds(r, GLA_BLK), :] for r in rs]
        cums = [c_ref[pl.ds(r, GLA_BLK), :] for c_ref, r in zip((cf_ref, cb_ref), rs)]
        vs = [v_ref[pl.ds(r, GLA_BLK), :] for r in rs]
        tots = [cums[0][GLA_BLK - 1:GLA_BLK], cums[1][0:1]]
        sts = [st_ref[dr] for dr in dirs]
        qds = [(qs[dr] * jnp.exp(cums[dr])).astype(BF16) for dr in dirs]
        kds = [(ks[dr] * jnp.exp(tots[dr] - cums[dr])).astype(BF16) for dr in dirs]
        o_states = [_dot_nt(qds[dr], sts[dr].astype(BF16)) for dr in dirs]
        upds = [_dot_tn(vs[dr].astype(BF16), kds[dr]) for dr in dirs]
        xs = []
        for dr in dirs:
            tiles = []
            for s in range(GLA_BLK):
                valid = (sub >= s) if dr == 0 else (sub <= s)
                w = jnp.exp(jnp.where(valid, cums[dr] - cums[dr][s:s + 1], NEG))
                tiles.append((w * qs[dr] * ks[dr][s:s + 1]).astype(BF16))
            xs.append(jnp.concatenate(tiles, axis=0))
        ress = [jnp.dot(x, r3_ref[...], preferred_element_type=F32) for x in xs]
        for dr in dirs:
            st_ref[dr] = sts[dr] * jnp.exp(tots[dr]) + jnp.where(bd, upds[dr], 0.0)
        for dr in dirs:
            o_diag = ress[dr][0:GLA_BLK] * vs[dr][0:1]
            for s in range(1, GLA_BLK):
                o_diag = o_diag + ress[dr][s * GLA_BLK:(s + 1) * GLA_BLK] * vs[dr][s:s + 1]
            o_ref[pl.ds(rs[dr], GLA_BLK), :] += o_states[dr] + o_diag
        return carry

    lax.fori_loop(0, nlat + nctx, step, 0)


def _gla_scan(qr, kr, cf, cb, p, s_lat, l_ctx):
    nb, t, _ = p.shape
    lanes = 2 * GLA_DK
    r3 = jnp.asarray((np.arange(lanes)[:, None] // GLA_DK) == (np.arange(2 * GLA_DV)[None, :] // GLA_DV), BF16)
    blk = pl.BlockSpec((None, t, lanes), lambda b, g: (b, 0, g))
    return pl.pallas_call(
        functools.partial(_gla_scan_kernel, s_lat=s_lat, l_ctx=l_ctx),
        grid=(nb, GLA_HEADS // 2),
        in_specs=[blk, blk, blk, blk,
                  pl.BlockSpec((None, t, 2 * GLA_DV), lambda b, g: (b, 0, COL_GLA_V // (2 * GLA_DV) + g)),
                  pl.BlockSpec((lanes, 2 * GLA_DV), lambda b, g: (0, 0))],
        out_specs=pl.BlockSpec((None, t, 2 * GLA_DV), lambda b, g: (b, 0, g)),
        out_shape=jax.ShapeDtypeStruct((nb, t, GLA_HEADS * GLA_DV), F32),
        scratch_shapes=[pltpu.VMEM((2, 2 * GLA_DV, lanes), F32)],
        compiler_params=_cparams(("arbitrary", "arbitrary")),
        name="gla_scan",
    )(qr, kr, cf, cb, p, r3)


def _gdn_conv_kernel(x_ref, w_ref, o_ref, *, s_lat, normalize, scale):
    t = x_ref.shape[0]
    x = x_ref[...]
    tpos = lax.broadcasted_iota(I32, (t, 1), 0)
    half = GDN_CONV // 2
    acc = x * w_ref[half:half + 1, :]
    for j in range(GDN_CONV):
        if j == half:
            continue
        dlt = j - half
        src = tpos + dlt
        ok = (src >= 0) & (src < t) & ((src < s_lat) == (tpos < s_lat))
        acc = acc + jnp.where(ok, pltpu.roll(x, (-dlt) % t, 0), 0.0) * w_ref[j:j + 1, :]
    y = _silu(acc)
    if normalize:
        y = y * lax.rsqrt(jnp.sum(y * y, axis=-1, keepdims=True) + 1e-6) * scale
    o_ref[...] = y


def _gdn_conv(p, conv_w, s_lat, part):
    nb, t, _ = p.shape
    col = (COL_GDN_Q, COL_GDN_K, COL_GDN_V)[part]
    return pl.pallas_call(
        functools.partial(_gdn_conv_kernel, s_lat=s_lat, normalize=part < 2,
                          scale=GDN_DK ** -0.5 if part == 0 else 1.0),
        grid=(nb, GDN_HEADS),
        in_specs=[pl.BlockSpec((None, t, 128), lambda b, g: (b, 0, col // 128 + g)),
                  pl.BlockSpec((GDN_CONV, 128), lambda b, g: (0, part * GDN_HEADS + g))],
        out_specs=pl.BlockSpec((None, t, 128), lambda b, g: (b, 0, g)),
        out_shape=jax.ShapeDtypeStruct((nb, t, GDN_HEADS * 128), F32),
        compiler_params=_cparams(("arbitrary", "arbitrary")),
        name=f"gdn_conv{part}",
    )(p, conv_w)


GDN_HEADS_PER_STEP = 2


def _gdn_chunk_kernel(q_ref, k_ref, v_ref, sm_ref, alog_ref, dtb_ref,
                      u_ref, w_ref, qd_ref, kd_ref, a_ref, gl_ref, *, tc):
    g = pl.program_id(1)
    cs = GDN_CHUNK
    nh = GDN_HEADS_PER_STEP
    lane = lax.broadcasted_iota(I32, (1, 128), 1)
    subl = lax.broadcasted_iota(I32, (128, 1), 0)
    ti = lax.broadcasted_iota(I32, (tc, tc), 0)
    ui = lax.broadcasted_iota(I32, (tc, tc), 1)
    same = (ti // cs) == (ui // cs)
    eye = jnp.where(ti == ui, 1.0, 0.0)
    sm = sm_ref[...]
    sm_t = sm.T
    qs = [q_ref[:, hh * 128:(hh + 1) * 128] for hh in range(nh)]
    ks = [k_ref[:, hh * 128:(hh + 1) * 128] for hh in range(nh)]
    vs = [v_ref[:, hh * 128:(hh + 1) * 128] for hh in range(nh)]
    kbs = [k.astype(BF16) for k in ks]
    kks = [_dot_nt(kb, kb) for kb in kbs]
    qks = [_dot_nt(q.astype(BF16), kb) for q, kb in zip(qs, kbs)]
    chains = [(hh, dr) for hh in range(nh) for dr in range(2)]
    ms, decays, betas, e_gcs = [], [], [], []
    for hh, dr in chains:
        head = g * nh + hh
        ca = SMALL_A + GDN_HEADS * dr + head
        cbeta = SMALL_B + GDN_HEADS * dr + head
        a_col = jnp.sum(jnp.where(lane == ca, sm, 0.0), axis=1, keepdims=True)
        b_col = jnp.sum(jnp.where(lane == cbeta, sm, 0.0), axis=1, keepdims=True)
        a_row = jnp.sum(jnp.where(subl == ca, sm_t, 0.0), axis=0, keepdims=True)
        neg_rate = -jnp.exp(alog_ref[dr, hh])
        g_col = neg_rate * _softplus(a_col + dtb_ref[dr, hh])
        g_row = neg_rate * _softplus(a_row + dtb_ref[dr, hh])
        beta = _sigmoid(b_col)
        incl = same & ((ui <= ti) if dr == 0 else (ui >= ti))
        strict = same & ((ui < ti) if dr == 0 else (ui > ti))
        incl_t = same & ((ti <= ui) if dr == 0 else (ti >= ui))
        gc_col = jnp.sum(jnp.where(incl, g_row, 0.0), axis=1, keepdims=True)
        gc_row = jnp.sum(jnp.where(incl_t, g_col, 0.0), axis=0, keepdims=True)
        gc_tot = jnp.sum(jnp.where(same, g_row, 0.0), axis=1, keepdims=True)
        decay = jnp.exp(jnp.where(incl, gc_col - gc_row, NEG))
        ms.append(jnp.where(strict, beta * kks[hh] * decay, 0.0))
        decays.append(decay)
        betas.append(beta)
        e_gcs.append(jnp.exp(gc_col))
        kd_ref[hh, dr] = (ks[hh] * jnp.exp(gc_tot - gc_col)).astype(BF16)
        for c in range(tc // cs):
            gl_ref[hh, dr, c * 8:(c + 1) * 8, :] = jnp.broadcast_to(jnp.exp(gc_tot[c * cs:c * cs + 1]), (8, 128))
    invs = [eye - m for m in ms]
    mks = [_bdot(m, m) for m in ms]
    for lvl in range(5):
        invs = [inv + _bdot(inv, mk) for inv, mk in zip(invs, mks)]
        if lvl < 4:
            mks = [_bdot(mk, mk) for mk in mks]
    sols = [_bdot(invs[ci], jnp.concatenate([vs[hh] * betas[ci], ks[hh] * (betas[ci] * e_gcs[ci])], axis=1))
            for ci, (hh, dr) in enumerate(chains)]
    for ci, (hh, dr) in enumerate(chains):
        u_ref[hh, dr] = sols[ci][:, :GDN_DV]
        w_ref[hh, dr] = sols[ci][:, GDN_DV:].astype(BF16)
        qd_ref[hh, dr] = (qs[hh] * e_gcs[ci]).astype(BF16)
        aqk = qks[hh] * decays[ci]
        for c in range(tc // cs):
            blk = aqk[c * cs:(c + 1) * cs]
            folded = blk[:, 0:128]
            for piece in range(1, tc // 128):
                folded = folded + blk[:, piece * 128:(piece + 1) * 128]
            a_ref[hh, dr, c * cs:(c + 1) * cs, :] = folded.astype(BF16)


def _gdn_chunk(qn, kn, vn, p, a_log, dt_bias):
    nb, t, _ = p.shape
    tc = 256
    nh = GDN_HEADS_PER_STEP
    nch = t // GDN_CHUNK
    rb = lambda: pl.BlockSpec((None, tc, nh * 128), lambda b, g, i: (b, i, g))
    par = pl.BlockSpec((2, nh, 1, 1), lambda b, g, i: (0, g, 0, 0))
    ob = lambda w: pl.BlockSpec((None, nh, 2, tc, w), lambda b, g, i: (b, g, 0, i, 0))
    shp = lambda w, dt: jax.ShapeDtypeStruct((nb, GDN_HEADS, 2, t, w), dt)
    return pl.pallas_call(
        functools.partial(_gdn_chunk_kernel, tc=tc),
        grid=(nb, GDN_HEADS // nh, t // tc),
        in_specs=[rb(), rb(), rb(),
                  pl.BlockSpec((None, tc, 128), lambda b, g, i: (b, i, COL_SMALL // 128)), par, par],
        out_specs=[ob(128), ob(128), ob(128), ob(128), ob(128),
                   pl.BlockSpec((None, nh, 2, (tc // GDN_CHUNK) * 8, 128), lambda b, g, i: (b, g, 0, i, 0))],
        out_shape=[shp(128, F32), shp(128, BF16), shp(128, BF16), shp(128, BF16), shp(128, BF16),
                   jax.ShapeDtypeStruct((nb, GDN_HEADS, 2, nch * 8, 128), F32)],
        compiler_params=_cparams(("arbitrary", "arbitrary", "arbitrary")),
        name="gdn_chunk",
    )(qn, kn, vn, p, a_log.reshape(2, GDN_HEADS, 1, 1).astype(F32), dt_bias.reshape(2, GDN_HEADS, 1, 1).astype(F32))


def _gdn_scan_kernel(u_ref, w_ref, qd_ref, kd_ref, a_ref, gl_ref, o_ref, st_ref, *, s_lat, l_ctx):
    cs = GDN_CHUNK
    ncl, ncc = s_lat // cs, l_ctx // cs
    o_ref[...] = jnp.zeros(o_ref.shape, F32)
    st_ref[...] = jnp.zeros(st_ref.shape, F32)

    def step(i, carry):
        in_ctx = i < ncc
        cf = jnp.where(in_ctx, ncl + i, i - ncc)
        cb = jnp.where(in_ctx, ncl + ncc - 1 - i, ncl - 1 - (i - ncc))
        dirs = (0, 1)
        rs = [pl.multiple_of(c * cs, cs) for c in (cf, cb)]
        sts = [st_ref[dr] for dr in dirs]
        stbs = [st.astype(BF16) for st in sts]
        wss = [jnp.dot(w_ref[dr, pl.ds(rs[dr], cs), :], stbs[dr], preferred_element_type=F32) for dr in dirs]
        qss = [jnp.dot(qd_ref[dr, pl.ds(rs[dr], cs), :], stbs[dr], preferred_element_type=F32) for dr in dirs]
        vnbs = [(u_ref[dr, pl.ds(rs[dr], cs), :] - wss[dr]).astype(BF16) for dr in dirs]
        avs = [jnp.dot(a_ref[dr, pl.ds(rs[dr], cs), :], jnp.concatenate([vnbs[dr], vnbs[dr]], axis=0),
                       preferred_element_type=F32) for dr in dirs]
        upds = [_dot_tn(kd_ref[dr, pl.ds(rs[dr], cs), :], vnbs[dr]) for dr in dirs]
        for dr, c in zip(dirs, (cf, cb)):
            gl = gl_ref[dr, pl.ds(pl.multiple_of(c * 8, 8), 8), :]
            st_ref[dr] = sts[dr] * gl[0:1] + upds[dr]
        for dr in dirs:
            o_ref[pl.ds(rs[dr], cs), :] += qss[dr] + avs[dr]
        return carry

    lax.fori_loop(0, ncl + ncc, step, 0)


def _gdn_scan(u, w, qd, kd, a, gl, s_lat, l_ctx):
    nb, _, _, t, _ = u.shape
    nch8 = gl.shape[3]
    ib = lambda wd: pl.BlockSpec((None, None, 2, t, wd), lambda b, g: (b, g, 0, 0, 0))
    return pl.pallas_call(
        functools.partial(_gdn_scan_kernel, s_lat=s_lat, l_ctx=l_ctx),
        grid=(nb, GDN_HEADS),
        in_specs=[ib(128), ib(128), ib(128), ib(128), ib(128),
                  pl.BlockSpec((None, None, 2, nch8, 128), lambda b, g: (b, g, 0, 0, 0))],
        out_specs=pl.BlockSpec((None, t, 128), lambda b, g: (b, 0, g)),
        out_shape=jax.ShapeDtypeStruct((nb, t, GDN_HEADS * GDN_DV), F32),
        scratch_shapes=[pltpu.VMEM((2, GDN_DK, GDN_DV), F32)],
        compiler_params=_cparams(("arbitrary", "arbitrary")),
        name="gdn_scan",
    )(u, w, qd, kd, a, gl)


def _head_norm_gate(o, gate, nw):
    y = o * lax.rsqrt(jnp.mean(o * o, axis=-1, keepdims=True) + NORM_EPS) * nw
    return y * _silu(gate)


def _outproj_kernel(na_ref, gla_ref, gg_ref, gdn_ref, gz_ref, gnw_ref, dnw_ref, w_ref, h_ref, mod_ref,
                    o_ref, lhs_ref, *, s_lat, tm, nb):
    b, i, j = pl.program_id(0), pl.program_id(1), pl.program_id(2)

    @pl.when(j == 0)
    def _():
        lhs_ref[:, 0:NA_W] = na_ref[...].astype(BF16)
        for hh in range(GLA_HEADS):
            sl = slice(hh * GLA_DV, (hh + 1) * GLA_DV)
            y = _head_norm_gate(gla_ref[:, sl], gg_ref[:, sl], gnw_ref[...])
            lhs_ref[:, NA_W + hh * GLA_DV:NA_W + (hh + 1) * GLA_DV] = y.astype(BF16)
        base = NA_W + GLA_HEADS * GLA_DV
        for hh in range(GDN_HEADS):
            sl = slice(hh * GDN_DV, (hh + 1) * GDN_DV)
            y = _head_norm_gate(gdn_ref[:, sl], gz_ref[:, sl], dnw_ref[...])
            lhs_ref[:, base + hh * GDN_DV:base + (hh + 1) * GDN_DV] = y.astype(BF16)

    mix = jnp.dot(lhs_ref[...], w_ref[...].astype(BF16), preferred_element_type=F32)
    row = i * tm + lax.broadcasted_iota(I32, (tm, 1), 0)
    gate = jnp.where(row < s_lat, mod_ref[b][2:3], mod_ref[nb][2:3])
    o_ref[...] = h_ref[...] + gate * mix


def _outproj(o_na, o_gla, o_gdn, p, gla_nw, gdn_nw, w_out, layer, h, mod, s_lat):
    nb, t, d = h.shape
    tm = t // 8
    tn = min(512, d)
    mix_w = w_out.shape[1]
    gw = GLA_HEADS * GLA_DV
    rb = lambda w, c=0: pl.BlockSpec((None, tm, w), lambda b, i, j, c=c, w=w: (b, i, c // w))
    return pl.pallas_call(
        functools.partial(_outproj_kernel, s_lat=s_lat, tm=tm, nb=nb),
        grid=(nb, t // tm, d // tn),
        in_specs=[rb(NA_W), rb(gw), rb(gw, COL_GLA_G), rb(gw), rb(gw, COL_GDN_Z),
                  pl.BlockSpec((1, GLA_DV), lambda b, i, j: (0, 0)),
                  pl.BlockSpec((1, GDN_DV), lambda b, i, j: (0, 0)),
                  pl.BlockSpec((None, mix_w, tn), lambda b, i, j: (layer, 0, j)),
                  pl.BlockSpec((None, tm, tn), lambda b, i, j: (b, i, j)),
                  pl.BlockSpec((8, 6, tn), lambda b, i, j: (0, 0, j))],
        out_specs=pl.BlockSpec((None, tm, tn), lambda b, i, j: (b, i, j)),
        out_shape=jax.ShapeDtypeStruct((nb, t, d), F32),
        scratch_shapes=[pltpu.VMEM((tm, mix_w), BF16)],
        compiler_params=_cparams(("arbitrary", "arbitrary", "arbitrary")),
        name="outproj",
    )(o_na, o_gla, p, o_gdn, p, gla_nw, gdn_nw, w_out, h, mod)


def _router_kernel(h_ref, nw_ref, mod_ref, wr_ref, hn_ref, aff_ref, *, s_lat, tm, nb):
    b, i = pl.program_id(0), pl.program_id(1)
    hn = _mod_norm(h_ref[...], nw_ref[...], mod_ref[b], mod_ref[nb], i * tm, s_lat, 3)
    hn_ref[...] = hn.astype(BF16)
    logits = lax.dot_general(wr_ref[...], hn, (((1,), (1,)), ((), ())), precision=HI,
                             preferred_element_type=F32)
    e = jnp.exp(logits - logits.max(axis=0, keepdims=True))
    aff_ref[...] = e / e.sum(axis=0, keepdims=True)


def _router(h, nw, mod, w_router_t, s_lat):
    nb, t, d = h.shape
    tm = 256
    return pl.pallas_call(
        functools.partial(_router_kernel, s_lat=s_lat, tm=tm, nb=nb),
        grid=(nb, t // tm),
        in_specs=[pl.BlockSpec((None, tm, d), lambda b, i: (b, i, 0)),
                  pl.BlockSpec((1, d), lambda b, i: (0, 0)),
                  pl.BlockSpec((8, 6, d), lambda b, i: (0, 0, 0)),
                  pl.BlockSpec((N_EXPERTS, d), lambda b, i: (0, 0))],
        out_specs=[pl.BlockSpec((None, tm, d), lambda b, i: (b, i, 0)),
                   pl.BlockSpec((None, N_EXPERTS, tm), lambda b, i: (b, 0, i))],
        out_shape=[jax.ShapeDtypeStruct((nb, t, d), BF16),
                   jax.ShapeDtypeStruct((nb, N_EXPERTS, t), F32)],
        compiler_params=_cparams(("arbitrary", "arbitrary")),
        name="router",
    )(h, nw, mod, w_router_t)


def _lane_prefix(x01, tri):
    n = x01.shape[1]
    lane = lax.broadcasted_iota(I32, (1, 128), 1)
    off = jnp.zeros((x01.shape[0], 1), F32)
    before = jnp.zeros((x01.shape[0], 128), F32)
    outs = []
    for blk in range(n // 128):
        before = jnp.where(lane == blk, off, before)
        cb = jnp.dot(x01[:, blk * 128:(blk + 1) * 128].astype(BF16), tri, preferred_element_type=F32) + off
        outs.append(cb)
        off = cb[:, 127:128]
    return jnp.concatenate(outs, axis=1), before


def _select_kernel(aff_ref, tri_ref, slot_ref, s0_ref, *, cap):
    aff = aff_ref[...]
    ne = aff.shape[0]

    def bisect(_, c):
        lo, hi = c
        mid = 0.5 * (lo + hi)
        cnt = jnp.sum(jnp.where(aff >= mid, 1, 0), axis=1, keepdims=True)
        ok = cnt >= cap
        return jnp.where(ok, mid, lo), jnp.where(ok, hi, mid)

    lo, _ = lax.fori_loop(0, SELECT_STEPS, bisect, (jnp.zeros((ne, 1), F32), jnp.full((ne, 1), 2.0, F32)))
    thr = jnp.min(jnp.where(aff >= lo, aff, 2.0), axis=1, keepdims=True)
    gt = aff > thr
    eq = aff == thr
    need = (cap - jnp.sum(jnp.where(gt, 1, 0), axis=1, keepdims=True)).astype(F32)
    eq_f = jnp.where(eq, 1.0, 0.0)
    eq_incl, _ = _lane_prefix(eq_f, tri_ref[...])
    sel = gt | (eq & ((eq_incl - eq_f) < need))
    sel_f = jnp.where(sel, 1.0, 0.0)
    pos_incl, before = _lane_prefix(sel_f, tri_ref[...])
    slot_ref[...] = jnp.where(sel, pos_incl - 1.0, -1.0).astype(I32)
    s0_ref[...] = before.astype(I32)


def _select(aff, n, blk_idx, cap):
    nb = aff.shape[0]
    tri = jnp.asarray(np.arange(128)[:, None] <= np.arange(128)[None, :], BF16)
    return pl.pallas_call(
        functools.partial(_select_kernel, cap=cap),
        grid=(nb,),
        in_specs=[pl.BlockSpec((None, N_EXPERTS, n), lambda b: (b, 0, blk_idx)),
                  pl.BlockSpec((128, 128), lambda b: (0, 0))],
        out_specs=[pl.BlockSpec((None, N_EXPERTS, n), lambda b: (b, 0, 0)),
                   pl.BlockSpec((None, N_EXPERTS, 128), lambda b: (b, 0, 0))],
        out_shape=[jax.ShapeDtypeStruct((nb, N_EXPERTS, n), I32),
                   jax.ShapeDtypeStruct((nb, N_EXPERTS, 128), I32)],
        compiler_params=_cparams(("arbitrary",)),
        name="select",
    )(aff, tri)


SELECT_STEPS = 152
COMPACT_WIN = 128 + 8
COMPACT_WIN_SMALL = 32
EXPERT_ROW_CHUNK = 256
COMBINE_WIN_SMALL = 64
COMBINE_WIN = 256


def _compact_kernel(s0_ref, hn_ref, slot_ref, aff_ref, xs_ref, g_ref, acc_ref, gacc_ref, *, n, cap):
    b, e = pl.program_id(0), pl.program_id(1)
    nblk = n // 128
    acc_ref[...] = jnp.zeros(acc_ref.shape, F32)
    gacc_ref[...] = jnp.zeros(gacc_ref.shape, F32)

    def body(blk, carry):
        base = (b * N_EXPERTS + e) * 128
        s0 = s0_ref[base + blk]
        s1 = jnp.where(blk + 1 < nblk, s0_ref[base + jnp.minimum(blk + 1, nblk - 1)], cap)
        s0a = pl.multiple_of((s0 // 8) * 8, 8)
        t0 = pl.multiple_of(blk * 128, 128)

        def scatter_rows(win):
            sl = slot_ref[pl.ds(blk, 1), :]
            af = aff_ref[pl.ds(blk, 1), :]
            hit = lax.broadcasted_iota(I32, (win, 128), 0) == (sl - s0a)
            acc_ref[pl.ds(s0a, win), :] += jnp.dot(jnp.where(hit, 1.0, 0.0).astype(BF16),
                                                   hn_ref[pl.ds(t0, 128), :], preferred_element_type=F32)
            gv = jnp.sum(jnp.where(hit, af, 0.0), axis=1, keepdims=True)
            gacc_ref[pl.ds(s0a, win), :] += jnp.broadcast_to(gv, (win, 128))

        small = (s1 - s0a) <= COMPACT_WIN_SMALL

        @pl.when(small)
        def _():
            scatter_rows(COMPACT_WIN_SMALL)

        @pl.when(jnp.logical_not(small))
        def _():
            scatter_rows(COMPACT_WIN)

        return carry

    lax.fori_loop(0, nblk, body, 0)
    xs_ref[...] = acc_ref[0:cap, :].astype(BF16)
    g_ref[...] = gacc_ref[0:cap, :]


def _compact(hn, slot, aff, s0, n, cap, row_blk):
    nb, _, d = hn.shape
    nblk = n // 128
    slot4 = slot.reshape(nb, N_EXPERTS, nblk, 128)
    aff4 = aff.reshape(nb, N_EXPERTS, nblk, 128)
    grid_spec = pltpu.PrefetchScalarGridSpec(
        num_scalar_prefetch=1,
        grid=(nb, N_EXPERTS),
        in_specs=[pl.BlockSpec((None, n, d), lambda b, e, s: (b, row_blk, 0)),
                  pl.BlockSpec((None, None, nblk, 128), lambda b, e, s: (b, e, 0, 0)),
                  pl.BlockSpec((None, None, nblk, 128), lambda b, e, s: (b, e, 0, 0))],
        out_specs=[pl.BlockSpec((None, cap, d), lambda b, e, s: (e, b, 0)),
                   pl.BlockSpec((None, cap, 128), lambda b, e, s: (e, b, 0))],
        scratch_shapes=[pltpu.VMEM((cap + COMPACT_WIN, d), F32), pltpu.VMEM((cap + COMPACT_WIN, 128), F32)],
    )
    return pl.pallas_call(
        functools.partial(_compact_kernel, n=n, cap=cap),
        grid_spec=grid_spec,
        out_shape=[jax.ShapeDtypeStruct((N_EXPERTS, nb * cap, d), BF16),
                   jax.ShapeDtypeStruct((N_EXPERTS, nb * cap, 128), F32)],
        compiler_params=_cparams(("arbitrary", "arbitrary")),
        name="compact",
    )(s0.reshape(-1), hn, slot4, aff4)


def _expert_kernel(*refs, n_in):
    xs = refs[0:n_in]
    gs = refs[n_in:2 * n_in]
    wg_ref, wu_ref, wd_ref = refs[2 * n_in:2 * n_in + 3]
    ys = refs[2 * n_in + 3:3 * n_in + 3]
    accs = refs[3 * n_in + 3:4 * n_in + 3]
    wgu_ref, wdb_ref = refs[4 * n_in + 3:]
    f = pl.program_id(1)
    last = pl.num_programs(1) - 1
    tf = wg_ref.shape[1]
    wgu_ref[:, 0:tf] = wg_ref[...].astype(BF16)
    wgu_ref[:, tf:2 * tf] = wu_ref[...].astype(BF16)
    wdb_ref[...] = wd_ref[...].astype(BF16)

    @pl.when(f == 0)
    def _():
        for acc_ref in accs:
            acc_ref[...] = jnp.zeros(acc_ref.shape, F32)

    chunks = []
    for x_ref, acc_ref in zip(xs, accs):
        m = x_ref.shape[0]
        rc = min(m, EXPERT_ROW_CHUNK)
        chunks += [(x_ref, acc_ref, r0, rc) for r0 in range(0, m, rc)]

    def up(chunk):
        x_ref, _, r0, rc = chunk
        return jnp.dot(x_ref[r0:r0 + rc, :], wgu_ref[...], preferred_element_type=F32)

    def down(chunk, au):
        _, acc_ref, r0, rc = chunk
        hid = (_silu(au[:, 0:tf]) * au[:, tf:2 * tf]).astype(BF16)
        acc_ref[r0:r0 + rc, :] += jnp.dot(hid, wdb_ref[...], preferred_element_type=F32)

    au_prev = up(chunks[0])
    for ci in range(1, len(chunks)):
        au_next = up(chunks[ci])
        down(chunks[ci - 1], au_prev)
        au_prev = au_next
    down(chunks[-1], au_prev)

    @pl.when(f == last)
    def _():
        for g_ref, y_ref, acc_ref in zip(gs, ys, accs):
            gate = g_ref[...]
            for cblk in range(acc_ref.shape[1] // 128):
                sl = slice(cblk * 128, (cblk + 1) * 128)
                y_ref[:, sl] = (acc_ref[:, sl] * gate).astype(BF16)


def _experts(xs_list, g_list, w_gate, w_up, w_down, layer):
    _, ne, d, ff = w_gate.shape
    tf = 256
    n_in = len(xs_list)
    xspec = lambda m, w: pl.BlockSpec((None, m, w), lambda e, f: (e, 0, 0))
    return pl.pallas_call(
        functools.partial(_expert_kernel, n_in=n_in),
        grid=(ne, ff // tf),
        in_specs=([xspec(x.shape[1], d) for x in xs_list] + [xspec(g.shape[1], 128) for g in g_list]
                  + [pl.BlockSpec((None, None, d, tf), lambda e, f: (layer, e, 0, f)),
                     pl.BlockSpec((None, None, d, tf), lambda e, f: (layer, e, 0, f)),
                     pl.BlockSpec((None, None, tf, d), lambda e, f: (layer, e, f, 0))]),
        out_specs=[xspec(x.shape[1], d) for x in xs_list],
        out_shape=[jax.ShapeDtypeStruct(x.shape, BF16) for x in xs_list],
        scratch_shapes=([pltpu.VMEM((x.shape[1], d), F32) for x in xs_list]
                        + [pltpu.VMEM((d, 2 * tf), BF16), pltpu.VMEM((tf, d), BF16)]),
        compiler_params=_cparams(("arbitrary", "arbitrary")),
        name="experts",
    )(*xs_list, *g_list, w_gate, w_up, w_down)


def _combine_kernel(s0_ref, y_ref, slot_ref, h_ref, mod_ref, o_ref, *, cap, win, mod_row_static):
    b, i = pl.program_id(0), pl.program_id(2)
    nblk = pl.num_programs(2)
    mrow = mod_ref[b] if mod_row_static is None else mod_ref[mod_row_static]
    win_small = min(win, COMBINE_WIN_SMALL)

    def window_start(e, w):
        s0 = s0_ref[(b * N_EXPERTS + e) * 128 + i]
        return jnp.minimum((s0 // 16) * 16, cap - w)

    def combine(w):
        slot_t = slot_ref[...]
        jj = lax.broadcasted_iota(I32, (1, w), 1)
        acc = None
        for e in range(N_EXPERTS):
            s0a = pl.multiple_of(window_start(e, w), 16)
            hit = (slot_t[:, e:e + 1] - s0a) == jj
            part = jnp.dot(jnp.where(hit, 1.0, 0.0).astype(BF16), y_ref[e, pl.ds(s0a, w), :],
                           preferred_element_type=F32)
            acc = part if acc is None else acc + part
        o_ref[...] = h_ref[...] + mrow[5:6] * acc

    if win_small == win:
        combine(win)
    else:
        fits = None
        for e in range(N_EXPERTS):
            nxt = s0_ref[(b * N_EXPERTS + e) * 128 + jnp.minimum(i + 1, nblk - 1)]
            end = jnp.where(i + 1 < nblk, nxt, cap)
            ok = end - window_start(e, win_small) <= win_small
            fits = ok if fits is None else jnp.logical_and(fits, ok)

        @pl.when(fits)
        def _():
            combine(win_small)

        @pl.when(jnp.logical_not(fits))
        def _():
            combine(win)


def _combine(y, slot, s0, h, mod, n, cap, tok_blk0, is_ctx):
    nb, t, d = h.shape
    nblk = n // 128
    dcols = d // 2 if d >= 256 else d
    win = min(cap, COMBINE_WIN)
    slot_t = jnp.swapaxes(slot, 1, 2)
    grid_spec = pltpu.PrefetchScalarGridSpec(
        num_scalar_prefetch=1,
        grid=(nb, d // dcols, nblk),
        in_specs=[pl.BlockSpec((N_EXPERTS, cap, dcols), lambda b, c, i, s: (0, b, c)),
                  pl.BlockSpec((None, 128, N_EXPERTS), lambda b, c, i, s: (b, i, 0)),
                  pl.BlockSpec((None, 128, dcols), lambda b, c, i, s: (b, tok_blk0 + i, c)),
                  pl.BlockSpec((8, 6, dcols), lambda b, c, i, s: (0, 0, c))],
        out_specs=pl.BlockSpec((None, 128, dcols), lambda b, c, i, s: (b, tok_blk0 + i, c)),
    )
    return pl.pallas_call(
        functools.partial(_combine_kernel, cap=cap, win=win, mod_row_static=nb if is_ctx else None),
        grid_spec=grid_spec,
        out_shape=jax.ShapeDtypeStruct((nb, t, d), F32),
        input_output_aliases={3: 0},
        compiler_params=_cparams(("arbitrary", "arbitrary", "arbitrary")),
        name="combine",
    )(s0.reshape(-1), y, slot_t, h, mod)


def _moe(h, nw, mod, w_router, w_gate, w_up, w_down, layer, s_lat, l_ctx, with_ctx):
    hn, aff = _router(h, nw, mod, w_router.T, s_lat)
    streams = [(s_lat, 0, 0, False)]
    if with_ctx:
        streams.append((l_ctx, s_lat // l_ctx, s_lat // 128, True))
    sel = []
    for n, blk_idx, _, _ in streams:
        cap = EC_CAPACITY * n // N_EXPERTS
        slot, s0 = _select(aff, n, blk_idx, cap)
        aff_s = lax.slice_in_dim(aff, blk_idx * n, blk_idx * n + n, axis=2)
        xs, gates = _compact(hn, slot, aff_s, s0, n, cap, blk_idx)
        sel.append((slot, s0, xs, gates, cap))
    ys = _experts([s[2] for s in sel], [s[3] for s in sel], w_gate, w_up, w_down, layer)
    for (n, _, tok_blk0, is_ctx), (slot, s0, _, _, cap), y in zip(streams, sel, ys):
        h = _combine(y, slot, s0, h, mod, n, cap, tok_blk0, is_ctx)
    return h


def _final_norm_kernel(h_ref, w_ref, o_ref):
    x = h_ref[...]
    o_ref[...] = x * lax.rsqrt(jnp.mean(x * x, axis=-1, keepdims=True) + NORM_EPS) * w_ref[...]


def _final_norm(h, w, s_lat):
    nb, _, d = h.shape
    tm = 512
    return pl.pallas_call(
        _final_norm_kernel,
        grid=(nb, s_lat // tm),
        in_specs=[pl.BlockSpec((None, tm, d), lambda b, i: (b, i, 0)),
                  pl.BlockSpec((1, d), lambda b, i: (0, 0))],
        out_specs=pl.BlockSpec((None, tm, d), lambda b, i: (b, i, 0)),
        out_shape=jax.ShapeDtypeStruct((nb, s_lat, d), F32),
        compiler_params=_cparams(("arbitrary", "arbitrary")),
        name="final_norm",
    )(h, w)


def _reorder_w_in(w_in):
    d = w_in.shape[0]
    lr0 = COL_GDN_Q
    qkv0 = lr0 + 2 * GLA_RANK
    ab0 = qkv0 + 3 * GDN_HEADS * GDN_DK + GDN_HEADS * GDN_DV
    n_in = ab0 + 4 * GDN_HEADS
    return jnp.concatenate([w_in[:, :lr0], w_in[:, qkv0:ab0], w_in[:, lr0:qkv0], w_in[:, ab0:n_in],
                            jnp.zeros((d, NP_COLS - n_in), w_in.dtype)], axis=1).astype(BF16)


def _token_mixers(h, nw, mod, w_in, w_out, layer, rpb, gate_w, gate_b, gla_nw, conv_w, a_log, dt_bias, gdn_nw,
                  rope, s_lat, l_ctx):
    p = _inproj(h, nw, mod, _reorder_w_in(w_in), s_lat)
    o_na = _na(p, _na_bias_table(rpb), s_lat, l_ctx)
    gw_p = jnp.zeros((2, 128, GLA_HEADS * GLA_DK), F32)
    for dr in range(2):
        gw_p = gw_p.at[dr, dr * GLA_RANK:(dr + 1) * GLA_RANK].set(gate_w[dr])
    qr, kr, cf, cb = _gla_prep(p, rope[0], rope[1], gw_p, gate_b[:, None, :], s_lat)
    o_gla = _gla_scan(qr, kr, cf, cb, p, s_lat, l_ctx)
    qn, kn, vn = (_gdn_conv(p, conv_w, s_lat, part) for part in range(3))
    o_gdn = _gdn_scan(*_gdn_chunk(qn, kn, vn, p, a_log, dt_bias), s_lat, l_ctx)
    return _outproj(o_na, o_gla, o_gdn, p, gla_nw[None, :], gdn_nw[None, :], w_out, layer, h, mod, s_lat)


def kernel(x, c, ctx, c_ctx, w_ada, b_ada, norm_mix_w, norm_ffn_w, w_in, w_out, na_rpb, gla_gate_w, gla_gate_b,
           gla_norm_w, gdn_conv_w, gdn_a_log, gdn_dt_bias, gdn_norm_w, w_router, w_exp_gate, w_exp_up,
           w_exp_down, final_norm_w):
    nb, s_lat, d = x.shape
    l_ctx = ctx.shape[1]
    depth = w_ada.shape[0]
    assert nb < 8 and s_lat % 256 == 0 and l_ctx == 256 and s_lat // GRID_W >= NA_WIN_R
    h = jnp.concatenate([x, ctx], axis=1)
    cvec = jnp.zeros((8, d), F32).at[:nb].set(c).at[nb].set(c_ctx)
    mods = _ada(cvec, w_ada, b_ada).reshape(depth, 8, 6, d)
    rope = _rope_tables(s_lat, l_ctx)
    w_out = w_out.astype(BF16)
    for l in range(depth):
        h = _token_mixers(h, norm_mix_w[l][None, :], mods[l], w_in[l], w_out, l, na_rpb[l], gla_gate_w[l],
                          gla_gate_b[l], gla_norm_w[l], gdn_conv_w[l], gdn_a_log[l], gdn_dt_bias[l],
                          gdn_norm_w[l], rope, s_lat, l_ctx)
        h = _moe(h, norm_ffn_w[l][None, :], mods[l], w_router[l], w_exp_gate, w_exp_up, w_exp_down, l,
                 s_lat, l_ctx, with_ctx=l < depth - 1)
    return _final_norm(h, final_norm_w[None, :], s_lat)
```

```python
import functools

import numpy as np
import jax
import jax.numpy as jnp
from jax import lax
from jax.experimental import pallas as pl
from jax.experimental.pallas import tpu as pltpu

F32 = jnp.float32
BF16 = jnp.bfloat16
I32 = jnp.int32
HI = lax.Precision.HIGHEST

GRID_W = 64
NA_HEADS, NA_DH, NA_WIN_R, NA_WIN_C = 16, 64, 8, 16
GLA_HEADS, GLA_DK, GLA_DV, GLA_RANK, GLA_TAU = 4, 64, 128, 16, 16.0
GLA_BLK = 16
GDN_HEADS, GDN_DK, GDN_DV, GDN_CONV, GDN_CHUNK = 4, 128, 128, 5, 64
N_EXPERTS, EC_CAPACITY = 16, 2
ROPE_BASE = 10000.0
NORM_EPS = 1e-6
NEG = -1e30

NA_W = NA_HEADS * NA_DH
COL_NA_Q, COL_NA_K, COL_NA_V = 0, NA_W, 2 * NA_W
COL_GLA_Q = 3 * NA_W
COL_GLA_K = COL_GLA_Q + GLA_HEADS * GLA_DK
COL_GLA_V = COL_GLA_K + GLA_HEADS * GLA_DK
COL_GLA_G = COL_GLA_V + GLA_HEADS * GLA_DV
COL_GDN_Q = COL_GLA_G + GLA_HEADS * GLA_DV
COL_GDN_K = COL_GDN_Q + GDN_HEADS * GDN_DK
COL_GDN_V = COL_GDN_K + GDN_HEADS * GDN_DK
COL_GDN_Z = COL_GDN_V + GDN_HEADS * GDN_DV
COL_SMALL = COL_GDN_Z + GDN_HEADS * GDN_DV
SMALL_A, SMALL_B = 2 * GLA_RANK, 2 * GLA_RANK + 2 * GDN_HEADS
INPROJ_TN = 512
NP_COLS = 7168

VMEM_LIMIT = 56 * 1024 * 1024


def _cparams(sem):
    return pltpu.CompilerParams(dimension_semantics=sem, vmem_limit_bytes=VMEM_LIMIT)


def _sigmoid(x):
    return 1.0 / (1.0 + jnp.exp(-x))


def _silu(x):
    return x * _sigmoid(x)


def _softplus(x):
    return jnp.maximum(x, 0.0) + jnp.log(1.0 + jnp.exp(-jnp.abs(x)))


def _bdot(a, b):
    return jnp.dot(a.astype(BF16), b.astype(BF16), preferred_element_type=F32)


def _dot_nt(a, b):
    return lax.dot_general(a, b, (((1,), (1,)), ((), ())), preferred_element_type=F32)


def _dot_tn(a, b):
    return lax.dot_general(a, b, (((0,), (0,)), ((), ())), preferred_element_type=F32)


def _dot01(m01, x):
    hi = x.astype(BF16)
    r1 = x - hi.astype(F32)
    mid = r1.astype(BF16)
    lo = (r1 - mid.astype(F32)).astype(BF16)
    d = lambda p: jnp.dot(m01, p, preferred_element_type=F32)
    return d(hi) + d(mid) + d(lo)


def _ada_kernel(c_ref, w_ref, b_ref, o_ref):
    o_ref[...] = _bdot(_silu(c_ref[...]), w_ref[...]) + b_ref[...]


def _ada(cvec, w_ada, b_ada):
    depth, d, n6 = w_ada.shape
    tn = 1024 if n6 % 1024 == 0 else 512
    assert n6 % tn == 0
    return pl.pallas_call(
        _ada_kernel,
        grid=(depth, n6 // tn),
        in_specs=[pl.BlockSpec((8, d), lambda l, j: (0, 0)),
                  pl.BlockSpec((None, d, tn), lambda l, j: (l, 0, j)),
                  pl.BlockSpec((None, 1, tn), lambda l, j: (l, 0, j))],
        out_specs=pl.BlockSpec((None, 8, tn), lambda l, j: (l, 0, j)),
        out_shape=jax.ShapeDtypeStruct((depth, 8, n6), F32),
        compiler_params=_cparams(("arbitrary", "arbitrary")),
        name="ada",
    )(cvec, w_ada, b_ada.reshape(depth, 1, n6))


def _mod_norm(x, nw, mod_b, mod_c, row0, s_lat, k_shift):
    ms = jnp.mean(x * x, axis=-1, keepdims=True)
    y = x * lax.rsqrt(ms + NORM_EPS) * nw
    row = row0 + lax.broadcasted_iota(I32, (x.shape[0], 1), 0)
    is_lat = row < s_lat
    shift = jnp.where(is_lat, mod_b[k_shift:k_shift + 1], mod_c[k_shift:k_shift + 1])
    scale = jnp.where(is_lat, mod_b[k_shift + 1:k_shift + 2], mod_c[k_shift + 1:k_shift + 2])
    return y * (1.0 + scale) + shift


def _inproj_kernel(x_ref, nw_ref, mod_ref, w_ref, o_ref, xn_ref, *, s_lat, tm, nb):
    b, i, j = pl.program_id(0), pl.program_id(1), pl.program_id(2)

    @pl.when(j == 0)
    def _():
        rc = tm // 4

        def chunk(ci, carry):
            r = pl.multiple_of(ci * rc, 16)
            xn = _mod_norm(x_ref[pl.ds(r, rc), :], nw_ref[...], mod_ref[b], mod_ref[nb], i * tm + r, s_lat, 0)
            xn_ref[pl.ds(r, rc), :] = xn.astype(BF16)
            return carry

        lax.fori_loop(0, 4, chunk, 0)

    o_ref[...] = jnp.dot(xn_ref[...], w_ref[...].astype(BF16), preferred_element_type=F32)


def _inproj(h, nw, mod, w_p, s_lat):
    nb, t, d = h.shape
    tm = t // 4
    tn = INPROJ_TN
    npc = w_p.shape[1]
    return pl.pallas_call(
        functools.partial(_inproj_kernel, s_lat=s_lat, tm=tm, nb=nb),
        grid=(nb, t // tm, npc // tn),
        in_specs=[pl.BlockSpec((None, tm, d), lambda b, i, j: (b, i, 0)),
                  pl.BlockSpec((1, d), lambda b, i, j: (0, 0)),
                  pl.BlockSpec((8, 6, d), lambda b, i, j: (0, 0, 0)),
                  pl.BlockSpec((d, tn), lambda b, i, j: (0, j))],
        out_specs=pl.BlockSpec((None, tm, tn), lambda b, i, j: (b, i, j)),
        out_shape=jax.ShapeDtypeStruct((nb, t, npc), F32),
        scratch_shapes=[pltpu.VMEM((tm, d), BF16)],
        compiler_params=_cparams(("arbitrary", "arbitrary", "arbitrary")),
        name="inproj",
    )(h, nw, mod, w_p)


NA_NDR = 2 * NA_WIN_R - 1
NA_NDC = 2 * NA_WIN_C - 1


def _na_bias_kernel(rpb_ref, onehot_ref, mask_ref, o_ref):
    t = jnp.dot(rpb_ref[...], onehot_ref[...], precision=HI, preferred_element_type=F32)
    o_ref[...] = jnp.where(mask_ref[...] > 0.0, t, NEG)


def _na_bias_table(rpb):
    nh = rpb.shape[0]
    cq = np.arange(GRID_W)
    dc = np.clip(cq[None, :] - cq[:, None] + NA_WIN_C - 1, 0, NA_NDC - 1)
    cstart = np.clip(cq - NA_WIN_C // 2, 0, GRID_W - NA_WIN_C)
    colmask = (cq[None, :] >= cstart[:, None]) & (cq[None, :] < cstart[:, None] + NA_WIN_C)
    onehot = jnp.asarray(np.arange(128)[:, None] == dc.reshape(1, -1), F32)
    mask = jnp.asarray(colmask.reshape(1, -1), F32)
    rpb_p = jnp.zeros((nh, 16, 128), F32).at[:, :NA_NDR, :NA_NDC].set(rpb)
    full = pl.pallas_call(
        _na_bias_kernel,
        grid=(nh,),
        in_specs=[pl.BlockSpec((None, 16, 128), lambda h: (h, 0, 0)),
                  pl.BlockSpec((128, GRID_W * GRID_W), lambda h: (0, 0)),
                  pl.BlockSpec((1, GRID_W * GRID_W), lambda h: (0, 0))],
        out_specs=pl.BlockSpec((None, 16, GRID_W * GRID_W), lambda h: (h, 0, 0)),
        out_shape=jax.ShapeDtypeStruct((nh, 16, GRID_W * GRID_W), F32),
        compiler_params=_cparams(("arbitrary",)),
        name="na_bias",
    )(rpb_p, onehot, mask)
    t15 = full[:, :NA_NDR].reshape(nh, NA_NDR, GRID_W, GRID_W)
    return jnp.concatenate([t15[:, :-1], t15[:, 1:]], axis=-1)


def _softmax_pv_many(s_lists, v_lists):
    n = len(s_lists)
    ms = []
    for sl in s_lists:
        m = sl[0].max(axis=-1, keepdims=True)
        for s in sl[1:]:
            m = jnp.maximum(m, s.max(axis=-1, keepdims=True))
        ms.append(m)
    ps = [[jnp.exp(s - ms[c]) for s in s_lists[c]] for c in range(n)]
    dens = []
    for c in range(n):
        den = ps[c][0].sum(axis=-1, keepdims=True)
        for p in ps[c][1:]:
            den = den + p.sum(axis=-1, keepdims=True)
        dens.append(den)
    outs = []
    for c in range(n):
        o = jnp.dot(ps[c][0].astype(BF16), v_lists[c][0], preferred_element_type=F32)
        for p, v in zip(ps[c][1:], v_lists[c][1:]):
            o = o + jnp.dot(p.astype(BF16), v, preferred_element_type=F32)
        outs.append(o)
    return [o / den for o, den in zip(outs, dens)]


NA_ROWS_PER_STEP = 4


def _na_kernel(q_ref, k_ref, v_ref, bias_ref, o_ref, kb_ref, vb_ref, *, s_lat, l_ctx):
    rows = s_lat // GRID_W
    nwin = NA_WIN_R * GRID_W
    scale = NA_DH ** -0.5
    kb_ref[...] = k_ref[...].astype(BF16)
    vb_ref[...] = v_ref[...].astype(BF16)
    lane = lax.broadcasted_iota(I32, (1, 2 * NA_DH), 1)
    head_lanes = (lane < NA_DH, lane >= NA_DH)
    kc = kb_ref[pl.ds(s_lat, l_ctx), :]
    vc = vb_ref[pl.ds(s_lat, l_ctx), :]

    qc = q_ref[pl.ds(s_lat, l_ctx), :] * scale
    qhs = [jnp.where(head_lanes[hh], qc, 0.0).astype(BF16) for hh in range(2)]
    outs = _softmax_pv_many([[_dot_nt(qh, kc)] for qh in qhs], [[vc], [vc]])
    o_ref[pl.ds(s_lat, l_ctx), :] = jnp.where(head_lanes[0], outs[0], outs[1])

    def body(it, carry):
        s_lists, v_lists, starts = [], [], []
        for rr in range(NA_ROWS_PER_STEP):
            r = it * NA_ROWS_PER_STEP + rr
            r0 = jnp.clip(r - NA_WIN_R // 2, 0, rows - NA_WIN_R)
            d = r - r0
            qs = pl.multiple_of(r * GRID_W, GRID_W)
            ks = pl.multiple_of(r0 * GRID_W, GRID_W)
            starts.append(qs)
            q = q_ref[pl.ds(qs, GRID_W), :] * scale
            kw = kb_ref[pl.ds(ks, nwin), :]
            vw = vb_ref[pl.ds(ks, nwin), :]
            q2 = jnp.concatenate([jnp.where(head_lanes[hh], q, 0.0) for hh in range(2)], axis=0).astype(BF16)
            bias = jnp.concatenate(
                [jnp.concatenate([bias_ref[hh, 2 * m + NA_WIN_R - 1 - d] for hh in range(2)], axis=0)
                 for m in range(NA_WIN_R // 2)], axis=1)
            s_lists.append([_dot_nt(q2, kw) + bias, _dot_nt(q2, kc)])
            v_lists.append([vw, vc])
        res = _softmax_pv_many(s_lists, v_lists)
        for rr in range(NA_ROWS_PER_STEP):
            o_ref[pl.ds(starts[rr], GRID_W), :] = jnp.where(head_lanes[0], res[rr][0:GRID_W], res[rr][GRID_W:2 * GRID_W])
        return carry

    lax.fori_loop(0, rows // NA_ROWS_PER_STEP, body, 0)


def _na(p, bias_tbl, s_lat, l_ctx):
    nb, t, _ = p.shape
    blk = lambda off: pl.BlockSpec((None, t, 128), lambda b, g, off=off: (b, 0, off // 128 + g))
    return pl.pallas_call(
        functools.partial(_na_kernel, s_lat=s_lat, l_ctx=l_ctx),
        grid=(nb, NA_HEADS // 2),
        in_specs=[blk(COL_NA_Q), blk(COL_NA_K), blk(COL_NA_V),
                  pl.BlockSpec((2, NA_NDR - 1, GRID_W, 2 * GRID_W), lambda b, g: (g, 0, 0, 0))],
        out_specs=pl.BlockSpec((None, t, 128), lambda b, g: (b, 0, g)),
        out_shape=jax.ShapeDtypeStruct((nb, t, NA_W), F32),
        scratch_shapes=[pltpu.VMEM((t, 128), BF16), pltpu.VMEM((t, 128), BF16)],
        compiler_params=_cparams(("arbitrary", "arbitrary")),
        name="na",
    )(p, p, p, bias_tbl)


def _rope_tables(s_lat, l_ctx):
    width = GLA_HEADS * GLA_DK
    nf = GLA_DK // 4
    pos = np.arange(s_lat)
    lane = np.arange(width)
    sub = lane % GLA_DK
    freqs = ROPE_BASE ** (-jnp.arange(nf, dtype=F32) / nf)
    p_sel = jnp.where((sub < GLA_DK // 2)[None, :], jnp.asarray(pos // GRID_W, F32)[:, None],
                      jnp.asarray(pos % GRID_W, F32)[:, None])
    ang = p_sel * freqs[sub % nf][None, :]
    sign = np.where((sub % (2 * nf)) < nf, -1.0, 1.0).astype(np.float32)
    cos = jnp.concatenate([jnp.cos(ang), jnp.ones((l_ctx, width), F32)], axis=0)
    sin = jnp.concatenate([jnp.sin(ang) * sign[None, :], jnp.zeros((l_ctx, width), F32)], axis=0)
    return cos, sin


def _block_tri(n, blk):
    i = np.arange(n)
    same = (i[:, None] // blk) == (i[None, :] // blk)
    lower = same & (i[None, :] <= i[:, None])
    upper = same & (i[None, :] >= i[:, None])
    return jnp.asarray(lower, BF16), jnp.asarray(upper, BF16)


def _gla_prep_kernel(q_ref, k_ref, sm_ref, cos_ref, sin_ref, gw_ref, gb_ref, tl_ref, tu_ref,
                     qr_ref, kr_ref, cf_ref, cb_ref):
    width = GLA_HEADS * GLA_DK
    nf = GLA_DK // 4
    lane = lax.broadcasted_iota(I32, (1, width), 1)
    first = (lane % (2 * nf)) < nf
    cos, sin = cos_ref[...], sin_ref[...]

    def rope(x):
        swapped = jnp.where(first, pltpu.roll(x, width - nf, 1), pltpu.roll(x, nf, 1))
        return x * cos + swapped * sin

    qr_ref[...] = rope(q_ref[...]) * (GLA_DK ** -0.5)
    kr_ref[...] = rope(k_ref[...])
    sm = sm_ref[...]
    for dr, (tri_ref, out_ref) in enumerate(((tl_ref, cf_ref), (tu_ref, cb_ref))):
        z = jnp.dot(sm, gw_ref[dr], precision=HI, preferred_element_type=F32) + gb_ref[dr]
        log_a = (jnp.minimum(z, 0.0) - jnp.log(1.0 + jnp.exp(-jnp.abs(z)))) * (1.0 / GLA_TAU)
        out_ref[...] = _dot01(tri_ref[...], log_a)


def _gla_prep(p, cos, sin, gw_p, gb, s_lat):
    nb, t, _ = p.shape
    width = GLA_HEADS * GLA_DK
    tr = 256
    tl, tu = _block_tri(tr, GLA_BLK)
    row_blk = lambda w, off: pl.BlockSpec((None, tr, w), lambda b, i, off=off, w=w: (b, i, off // w))
    tab = pl.BlockSpec((tr, width), lambda b, i: (i, 0))
    full = lambda shp: pl.BlockSpec(shp, lambda b, i, n=len(shp): (0,) * n)
    out = pl.BlockSpec((None, tr, width), lambda b, i: (b, i, 0))
    return pl.pallas_call(
        _gla_prep_kernel,
        grid=(nb, t // tr),
        in_specs=[row_blk(width, COL_GLA_Q), row_blk(width, COL_GLA_K), row_blk(128, COL_SMALL),
                  tab, tab, full((2, 128, width)), full((2, 1, width)), full((tr, tr)), full((tr, tr))],
        out_specs=[out] * 4,
        out_shape=[jax.ShapeDtypeStruct((nb, t, width), F32)] * 4,
        compiler_params=_cparams(("arbitrary", "arbitrary")),
        name="gla_prep",
    )(p, p, p, cos, sin, gw_p, gb, tl, tu)


def _gla_scan_kernel(qr_ref, kr_ref, cf_ref, cb_ref, v_ref, r3_ref, o_ref, st_ref, *, s_lat, l_ctx):
    nlat, nctx = s_lat // GLA_BLK, l_ctx // GLA_BLK
    o_ref[...] = jnp.zeros(o_ref.shape, F32)
    st_ref[...] = jnp.zeros(st_ref.shape, F32)
    sub = lax.broadcasted_iota(I32, (GLA_BLK, 2 * GLA_DK), 0)
    bd = (lax.broadcasted_iota(I32, (2 * GLA_DV, 2 * GLA_DK), 0) // GLA_DV
          == lax.broadcasted_iota(I32, (2 * GLA_DV, 2 * GLA_DK), 1) // GLA_DK)

    def step(i, carry):
        in_ctx = i < nctx
        jf = jnp.where(in_ctx, nlat + i, i - nctx)
        jb = jnp.where(in_ctx, nlat + nctx - 1 - i, nlat - 1 - (i - nctx))
        dirs = (0, 1)
        rs = [pl.multiple_of(j * GLA_BLK, GLA_BLK) for j in (jf, jb)]
        qs = [qr_ref[pl.ds(r, GLA_BLK), :] for r in rs]
        ks = [kr_ref[pl.ds(r, GLA_BLK), :] for r in rs]
        cums = [c_ref[pl.ds(r, GLA_BLK), :] for c_ref, r in zip((cf_ref, cb_ref), rs)]
        vs = [v_ref[pl.ds(r, GLA_BLK), :] for r in rs]
        tots = [cums[0][GLA_BLK - 1:GLA_BLK], cums[1][0:1]]
        sts = [st_ref[dr] for dr in dirs]
        qds = [(qs[dr] * jnp.exp(cums[dr])).astype(BF16) for dr in dirs]
        kds = [(ks[dr] * jnp.exp(tots[dr] - cums[dr])).astype(BF16) for dr in dirs]
        o_states = [_dot_nt(qds[dr], sts[dr].astype(BF16)) for dr in dirs]
        upds = [_dot_tn(vs[dr].astype(BF16), kds[dr]) for dr in dirs]
        xs = []
        for dr in dirs:
            tiles = []
            for s in range(GLA_BLK):
                valid = (sub >= s) if dr == 0 else (sub <= s)
                w = jnp.exp(jnp.where(valid, cums[dr] - cums[dr][s:s + 1], NEG))
                tiles.append((w * qs[dr] * ks[dr][s:s + 1]).astype(BF16))
            xs.append(jnp.concatenate(tiles, axis=0))
        ress = [jnp.dot(x, r3_ref[...], preferred_element_type=F32) for x in xs]
        for dr in dirs:
            st_ref[dr] = sts[dr] * jnp.exp(tots[dr]) + jnp.where(bd, upds[dr], 0.0)
        for dr in dirs:
            o_diag = ress[dr][0:GLA_BLK] * vs[dr][0:1]
            for s in range(1, GLA_BLK):
                o_diag = o_diag + ress[dr][s * GLA_BLK:(s + 1) * GLA_BLK] * vs[dr][s:s + 1]
            o_ref[pl.ds(rs[dr], GLA_BLK), :] += o_states[dr] + o_diag
        return carry

    lax.fori_loop(0, nlat + nctx, step, 0, unroll=4)


def _gla_scan(qr, kr, cf, cb, p, s_lat, l_ctx):
    nb, t, _ = p.shape
    lanes = 2 * GLA_DK
    r3 = jnp.asarray((np.arange(lanes)[:, None] // GLA_DK) == (np.arange(2 * GLA_DV)[None, :] // GLA_DV), BF16)
    blk = pl.BlockSpec((None, t, lanes), lambda b, g: (b, 0, g))
    return pl.pallas_call(
        functools.partial(_gla_scan_kernel, s_lat=s_lat, l_ctx=l_ctx),
        grid=(nb, GLA_HEADS // 2),
        in_specs=[blk, blk, blk, blk,
                  pl.BlockSpec((None, t, 2 * GLA_DV), lambda b, g: (b, 0, COL_GLA_V // (2 * GLA_DV) + g)),
                  pl.BlockSpec((lanes, 2 * GLA_DV), lambda b, g: (0, 0))],
        out_specs=pl.BlockSpec((None, t, 2 * GLA_DV), lambda b, g: (b, 0, g)),
        out_shape=jax.ShapeDtypeStruct((nb, t, GLA_HEADS * GLA_DV), F32),
        scratch_shapes=[pltpu.VMEM((2, 2 * GLA_DV, lanes), F32)],
        compiler_params=_cparams(("arbitrary", "arbitrary")),
        name="gla_scan",
    )(qr, kr, cf, cb, p, r3)


def _gdn_conv_kernel(x_ref, w_ref, o_ref, *, s_lat, normalize, scale):
    t = x_ref.shape[0]
    x = x_ref[...]
    tpos = lax.broadcasted_iota(I32, (t, 1), 0)
    half = GDN_CONV // 2
    acc = x * w_ref[half:half + 1, :]
    for j in range(GDN_CONV):
        if j == half:
            continue
        dlt = j - half
        src = tpos + dlt
        ok = (src >= 0) & (src < t) & ((src < s_lat) == (tpos < s_lat))
        acc = acc + jnp.where(ok, pltpu.roll(x, (-dlt) % t, 0), 0.0) * w_ref[j:j + 1, :]
    y = _silu(acc)
    if normalize:
        y = y * lax.rsqrt(jnp.sum(y * y, axis=-1, keepdims=True) + 1e-6) * scale
    o_ref[...] = y


def _gdn_conv(p, conv_w, s_lat, part):
    nb, t, _ = p.shape
    col = (COL_GDN_Q, COL_GDN_K, COL_GDN_V)[part]
    return pl.pallas_call(
        functools.partial(_gdn_conv_kernel, s_lat=s_lat, normalize=part < 2,
                          scale=GDN_DK ** -0.5 if part == 0 else 1.0),
        grid=(nb, GDN_HEADS),
        in_specs=[pl.BlockSpec((None, t, 128), lambda b, g: (b, 0, col // 128 + g)),
                  pl.BlockSpec((GDN_CONV, 128), lambda b, g: (0, part * GDN_HEADS + g))],
        out_specs=pl.BlockSpec((None, t, 128), lambda b, g: (b, 0, g)),
        out_shape=jax.ShapeDtypeStruct((nb, t, GDN_HEADS * 128), F32),
        compiler_params=_cparams(("arbitrary", "arbitrary")),
        name=f"gdn_conv{part}",
    )(p, conv_w)


GDN_HEADS_PER_STEP = 2


def _gdn_chunk_kernel(q_ref, k_ref, v_ref, sm_ref, alog_ref, dtb_ref,
                      u_ref, w_ref, qd_ref, kd_ref, a_ref, gl_ref, *, tc):
    g = pl.program_id(1)
    cs = GDN_CHUNK
    nh = GDN_HEADS_PER_STEP
    lane = lax.broadcasted_iota(I32, (1, 128), 1)
    subl = lax.broadcasted_iota(I32, (128, 1), 0)
    ti = lax.broadcasted_iota(I32, (tc, tc), 0)
    ui = lax.broadcasted_iota(I32, (tc, tc), 1)
    same = (ti // cs) == (ui // cs)
    eye = jnp.where(ti == ui, 1.0, 0.0)
    sm = sm_ref[...]
    sm_t = sm.T
    qs = [q_ref[:, hh * 128:(hh + 1) * 128] for hh in range(nh)]
    ks = [k_ref[:, hh * 128:(hh + 1) * 128] for hh in range(nh)]
    vs = [v_ref[:, hh * 128:(hh + 1) * 128] for hh in range(nh)]
    kbs = [k.astype(BF16) for k in ks]
    kks = [_dot_nt(kb, kb) for kb in kbs]
    qks = [_dot_nt(q.astype(BF16), kb) for q, kb in zip(qs, kbs)]
    chains = [(hh, dr) for hh in range(nh) for dr in range(2)]
    ms, decays, betas, e_gcs = [], [], [], []
    for hh, dr in chains:
        head = g * nh + hh
        ca = SMALL_A + GDN_HEADS * dr + head
        cbeta = SMALL_B + GDN_HEADS * dr + head
        a_col = jnp.sum(jnp.where(lane == ca, sm, 0.0), axis=1, keepdims=True)
        b_col = jnp.sum(jnp.where(lane == cbeta, sm, 0.0), axis=1, keepdims=True)
        a_row = jnp.sum(jnp.where(subl == ca, sm_t, 0.0), axis=0, keepdims=True)
        neg_rate = -jnp.exp(alog_ref[dr, hh])
        g_col = neg_rate * _softplus(a_col + dtb_ref[dr, hh])
        g_row = neg_rate * _softplus(a_row + dtb_ref[dr, hh])
        beta = _sigmoid(b_col)
        incl = same & ((ui <= ti) if dr == 0 else (ui >= ti))
        strict = same & ((ui < ti) if dr == 0 else (ui > ti))
        incl_t = same & ((ti <= ui) if dr == 0 else (ti >= ui))
        gc_col = jnp.sum(jnp.where(incl, g_row, 0.0), axis=1, keepdims=True)
        gc_row = jnp.sum(jnp.where(incl_t, g_col, 0.0), axis=0, keepdims=True)
        gc_tot = jnp.sum(jnp.where(same, g_row, 0.0), axis=1, keepdims=True)
        decay = jnp.exp(jnp.where(incl, gc_col - gc_row, NEG))
        ms.append(jnp.where(strict, beta * kks[hh] * decay, 0.0))
        decays.append(decay)
        betas.append(beta)
        e_gcs.append(jnp.exp(gc_col))
        kd_ref[hh, dr] = (ks[hh] * jnp.exp(gc_tot - gc_col)).astype(BF16)
        for c in range(tc // cs):
            gl_ref[hh, dr, c * 8:(c + 1) * 8, :] = jnp.broadcast_to(jnp.exp(gc_tot[c * cs:c * cs + 1]), (8, 128))
    invs = [eye - m for m in ms]
    mks = [_bdot(m, m) for m in ms]
    for lvl in range(5):
        invs = [inv + _bdot(inv, mk) for inv, mk in zip(invs, mks)]
        if lvl < 4:
            mks = [_bdot(mk, mk) for mk in mks]
    sols = [_bdot(invs[ci], jnp.concatenate([vs[hh] * betas[ci], ks[hh] * (betas[ci] * e_gcs[ci])], axis=1))
            for ci, (hh, dr) in enumerate(chains)]
    for ci, (hh, dr) in enumerate(chains):
        u_ref[hh, dr] = sols[ci][:, :GDN_DV]
        w_ref[hh, dr] = sols[ci][:, GDN_DV:].astype(BF16)
        qd_ref[hh, dr] = (qs[hh] * e_gcs[ci]).astype(BF16)
        aqk = qks[hh] * decays[ci]
        for c in range(tc // cs):
            blk = aqk[c * cs:(c + 1) * cs]
            folded = blk[:, 0:128]
            for piece in range(1, tc // 128):
                folded = folded + blk[:, piece * 128:(piece + 1) * 128]
            a_ref[hh, dr, c * cs:(c + 1) * cs, :] = folded.astype(BF16)


def _gdn_chunk(qn, kn, vn, p, a_log, dt_bias):
    nb, t, _ = p.shape
    tc = 256
    nh = GDN_HEADS_PER_STEP
    nch = t // GDN_CHUNK
    rb = lambda: pl.BlockSpec((None, tc, nh * 128), lambda b, g, i: (b, i, g))
    par = pl.BlockSpec((2, nh, 1, 1), lambda b, g, i: (0, g, 0, 0))
    ob = lambda w: pl.BlockSpec((None, nh, 2, tc, w), lambda b, g, i: (b, g, 0, i, 0))
    shp = lambda w, dt: jax.ShapeDtypeStruct((nb, GDN_HEADS, 2, t, w), dt)
    return pl.pallas_call(
        functools.partial(_gdn_chunk_kernel, tc=tc),
        grid=(nb, GDN_HEADS // nh, t // tc),
        in_specs=[rb(), rb(), rb(),
                  pl.BlockSpec((None, tc, 128), lambda b, g, i: (b, i, COL_SMALL // 128)), par, par],
        out_specs=[ob(128), ob(128), ob(128), ob(128), ob(128),
                   pl.BlockSpec((None, nh, 2, (tc // GDN_CHUNK) * 8, 128), lambda b, g, i: (b, g, 0, i, 0))],
        out_shape=[shp(128, F32), shp(128, BF16), shp(128, BF16), shp(128, BF16), shp(128, BF16),
                   jax.ShapeDtypeStruct((nb, GDN_HEADS, 2, nch * 8, 128), F32)],
        compiler_params=_cparams(("arbitrary", "arbitrary", "arbitrary")),
        name="gdn_chunk",
    )(qn, kn, vn, p, a_log.reshape(2, GDN_HEADS, 1, 1).astype(F32), dt_bias.reshape(2, GDN_HEADS, 1, 1).astype(F32))


GDN_SCAN_ROWS = 256


def _gdn_scan_kernel(*refs):
    fwd, bwd = refs[0:6], refs[6:12]
    o_refs = refs[12:14]
    st_ref = refs[14]
    cs = GDN_CHUNK
    ncg = GDN_SCAN_ROWS // cs
    chains = [(dr, hh) for dr in range(2) for hh in range(GDN_HEADS)]

    @pl.when(pl.program_id(1) == 0)
    def _():
        st_ref[...] = jnp.zeros(st_ref.shape, F32)

    for ci in range(ncg):
        rows = [slice(ci * cs, (ci + 1) * cs), slice((ncg - 1 - ci) * cs, (ncg - ci) * cs)]
        gls = [slice(ci * 8, ci * 8 + 1), slice((ncg - 1 - ci) * 8, (ncg - 1 - ci) * 8 + 1)]

        def piece(k, dr, hh):
            return (fwd, bwd)[dr][k][hh, rows[dr], :]

        sts = [st_ref[dr, hh] for dr, hh in chains]
        stbs = [st.astype(BF16) for st in sts]
        wss = [jnp.dot(piece(1, dr, hh), stb, preferred_element_type=F32) for (dr, hh), stb in zip(chains, stbs)]
        qss = [jnp.dot(piece(2, dr, hh), stb, preferred_element_type=F32) for (dr, hh), stb in zip(chains, stbs)]
        vnbs = [(piece(0, dr, hh) - ws).astype(BF16) for (dr, hh), ws in zip(chains, wss)]
        avs = [jnp.dot(piece(4, dr, hh), jnp.concatenate([vnb, vnb], axis=0), preferred_element_type=F32)
               for (dr, hh), vnb in zip(chains, vnbs)]
        upds = [_dot_tn(piece(3, dr, hh), vnb) for (dr, hh), vnb in zip(chains, vnbs)]
        for (dr, hh), st, upd in zip(chains, sts, upds):
            st_ref[dr, hh] = st * (fwd, bwd)[dr][5][hh, gls[dr], :] + upd
        for (dr, hh), qs, av in zip(chains, qss, avs):
            o_refs[dr][rows[dr], hh * GDN_DV:(hh + 1) * GDN_DV] = qs + av


def _gdn_scan(u, w, qd, kd, a, gl, s_lat, l_ctx):
    nb, nh, _, t, _ = u.shape
    tg = GDN_SCAN_ROWS
    assert l_ctx == tg and s_lat % tg == 0
    ngl = s_lat // tg
    grp = (lambda i: jnp.where(i == 0, ngl, i - 1),
           lambda i: jnp.where(i == 0, ngl, ngl - i))
    rows8 = (tg // GDN_CHUNK) * 8
    in_specs = []
    for dr in range(2):
        for wd in (128,) * 5:
            in_specs.append(pl.BlockSpec((None, nh, None, tg, wd), lambda b, i, dr=dr: (b, 0, dr, grp[dr](i), 0)))
        in_specs.append(pl.BlockSpec((None, nh, None, rows8, 128), lambda b, i, dr=dr: (b, 0, dr, grp[dr](i), 0)))
    return pl.pallas_call(
        _gdn_scan_kernel,
        grid=(nb, ngl + 1),
        in_specs=in_specs,
        out_specs=[pl.BlockSpec((None, tg, nh * GDN_DV), lambda b, i, dr=dr: (b, grp[dr](i), 0)) for dr in range(2)],
        out_shape=[jax.ShapeDtypeStruct((nb, t, nh * GDN_DV), F32)] * 2,
        scratch_shapes=[pltpu.VMEM((2, nh, GDN_DK, GDN_DV), F32)],
        compiler_params=_cparams(("arbitrary", "arbitrary")),
        name="gdn_scan",
    )(u, w, qd, kd, a, gl, u, w, qd, kd, a, gl)


def _head_norm_gate(o, gate, nw):
    y = o * lax.rsqrt(jnp.mean(o * o, axis=-1, keepdims=True) + NORM_EPS) * nw
    return y * _silu(gate)


def _outproj_kernel(na_ref, gla_ref, gg_ref, gdnf_ref, gdnb_ref, gz_ref, gnw_ref, dnw_ref, w_ref, h_ref, mod_ref,
                    o_ref, lhs_ref, *, s_lat, tm, nb):
    b, i, j = pl.program_id(0), pl.program_id(1), pl.program_id(2)

    @pl.when(j == 0)
    def _():
        lhs_ref[:, 0:NA_W] = na_ref[...].astype(BF16)
        for hh in range(GLA_HEADS):
            sl = slice(hh * GLA_DV, (hh + 1) * GLA_DV)
            y = _head_norm_gate(gla_ref[:, sl], gg_ref[:, sl], gnw_ref[...])
            lhs_ref[:, NA_W + hh * GLA_DV:NA_W + (hh + 1) * GLA_DV] = y.astype(BF16)
        base = NA_W + GLA_HEADS * GLA_DV
        for hh in range(GDN_HEADS):
            sl = slice(hh * GDN_DV, (hh + 1) * GDN_DV)
            y = _head_norm_gate(gdnf_ref[:, sl] + gdnb_ref[:, sl], gz_ref[:, sl], dnw_ref[...])
            lhs_ref[:, base + hh * GDN_DV:base + (hh + 1) * GDN_DV] = y.astype(BF16)

    mix = jnp.dot(lhs_ref[...], w_ref[...].astype(BF16), preferred_element_type=F32)
    row = i * tm + lax.broadcasted_iota(I32, (tm, 1), 0)
    gate = jnp.where(row < s_lat, mod_ref[b][2:3], mod_ref[nb][2:3])
    o_ref[...] = h_ref[...] + gate * mix


def _outproj(o_na, o_gla, o_gdn, p, gla_nw, gdn_nw, w_out, layer, h, mod, s_lat):
    nb, t, d = h.shape
    tm = t // 8
    tn = min(512, d)
    mix_w = w_out.shape[1]
    gw = GLA_HEADS * GLA_DV
    rb = lambda w, c=0: pl.BlockSpec((None, tm, w), lambda b, i, j, c=c, w=w: (b, i, c // w))
    return pl.pallas_call(
        functools.partial(_outproj_kernel, s_lat=s_lat, tm=tm, nb=nb),
        grid=(nb, t // tm, d // tn),
        in_specs=[rb(NA_W), rb(gw), rb(gw, COL_GLA_G), rb(gw), rb(gw), rb(gw, COL_GDN_Z),
                  pl.BlockSpec((1, GLA_DV), lambda b, i, j: (0, 0)),
                  pl.BlockSpec((1, GDN_DV), lambda b, i, j: (0, 0)),
                  pl.BlockSpec((None, mix_w, tn), lambda b, i, j: (layer, 0, j)),
                  pl.BlockSpec((None, tm, tn), lambda b, i, j: (b, i, j)),
                  pl.BlockSpec((8, 6, tn), lambda b, i, j: (0, 0, j))],
        out_specs=pl.BlockSpec((None, tm, tn), lambda b, i, j: (b, i, j)),
        out_shape=jax.ShapeDtypeStruct((nb, t, d), F32),
        scratch_shapes=[pltpu.VMEM((tm, mix_w), BF16)],
        compiler_params=_cparams(("arbitrary", "arbitrary", "arbitrary")),
        name="outproj",
    )(o_na, o_gla, p, o_gdn[0], o_gdn[1], p, gla_nw, gdn_nw, w_out, h, mod)


def _router_kernel(h_ref, nw_ref, mod_ref, wr_ref, hn_ref, aff_ref, *, s_lat, tm, nb):
    b, i = pl.program_id(0), pl.program_id(1)
    hn = _mod_norm(h_ref[...], nw_ref[...], mod_ref[b], mod_ref[nb], i * tm, s_lat, 3)
    hn_ref[...] = hn.astype(BF16)
    logits = jnp.dot(hn, wr_ref[...], precision=HI, preferred_element_type=F32).T[0:N_EXPERTS]
    e = jnp.exp(logits - logits.max(axis=0, keepdims=True))
    aff_ref[...] = e / e.sum(axis=0, keepdims=True)


def _router(h, nw, mod, w_router, s_lat):
    nb, t, d = h.shape
    tm = 256
    w_pad = jnp.zeros((d, 128), F32).at[:, :N_EXPERTS].set(w_router)
    return pl.pallas_call(
        functools.partial(_router_kernel, s_lat=s_lat, tm=tm, nb=nb),
        grid=(nb, t // tm),
        in_specs=[pl.BlockSpec((None, tm, d), lambda b, i: (b, i, 0)),
                  pl.BlockSpec((1, d), lambda b, i: (0, 0)),
                  pl.BlockSpec((8, 6, d), lambda b, i: (0, 0, 0)),
                  pl.BlockSpec((d, 128), lambda b, i: (0, 0))],
        out_specs=[pl.BlockSpec((None, tm, d), lambda b, i: (b, i, 0)),
                   pl.BlockSpec((None, N_EXPERTS, tm), lambda b, i: (b, 0, i))],
        out_shape=[jax.ShapeDtypeStruct((nb, t, d), BF16),
                   jax.ShapeDtypeStruct((nb, N_EXPERTS, t), F32)],
        compiler_params=_cparams(("arbitrary", "arbitrary")),
        name="router",
    )(h, nw, mod, w_pad)


def _lane_prefix(x01, tri):
    n = x01.shape[1]
    lane = lax.broadcasted_iota(I32, (1, 128), 1)
    off = jnp.zeros((x01.shape[0], 1), F32)
    before = jnp.zeros((x01.shape[0], 128), F32)
    outs = []
    for blk in range(n // 128):
        before = jnp.where(lane == blk, off, before)
        cb = jnp.dot(x01[:, blk * 128:(blk + 1) * 128].astype(BF16), tri, preferred_element_type=F32) + off
        outs.append(cb)
        off = cb[:, 127:128]
    return jnp.concatenate(outs, axis=1), before


def _select_kernel(aff_ref, tri_ref, slot_ref, s0_ref, *, cap):
    aff = aff_ref[...]
    ne = aff.shape[0]

    def bisect(_, c):
        lo, hi = c
        m2 = 0.5 * (lo + hi)
        m1 = 0.5 * (lo + m2)
        m3 = 0.5 * (m2 + hi)
        ok1, ok2, ok3 = (jnp.sum(jnp.where(aff >= m, 1, 0), axis=1, keepdims=True) >= cap for m in (m1, m2, m3))
        new_lo = jnp.where(ok3, m3, jnp.where(ok2, m2, jnp.where(ok1, m1, lo)))
        new_hi = jnp.where(ok3, hi, jnp.where(ok2, m3, jnp.where(ok1, m2, m1)))
        return new_lo, new_hi

    lo, _ = lax.fori_loop(0, SELECT_STEPS // 2, bisect, (jnp.zeros((ne, 1), F32), jnp.full((ne, 1), 2.0, F32)))
    thr = jnp.min(jnp.where(aff >= lo, aff, 2.0), axis=1, keepdims=True)
    gt = aff > thr
    eq = aff == thr
    need = (cap - jnp.sum(jnp.where(gt, 1, 0), axis=1, keepdims=True)).astype(F32)
    eq_f = jnp.where(eq, 1.0, 0.0)
    eq_incl, _ = _lane_prefix(eq_f, tri_ref[...])
    sel = gt | (eq & ((eq_incl - eq_f) < need))
    sel_f = jnp.where(sel, 1.0, 0.0)
    pos_incl, before = _lane_prefix(sel_f, tri_ref[...])
    slot_ref[...] = jnp.where(sel, pos_incl - 1.0, -1.0).astype(I32)
    s0_ref[...] = before.astype(I32)


def _select(aff, n, blk_idx, cap):
    nb = aff.shape[0]
    tri = jnp.asarray(np.arange(128)[:, None] <= np.arange(128)[None, :], BF16)
    return pl.pallas_call(
        functools.partial(_select_kernel, cap=cap),
        grid=(nb,),
        in_specs=[pl.BlockSpec((None, N_EXPERTS, n), lambda b: (b, 0, blk_idx)),
                  pl.BlockSpec((128, 128), lambda b: (0, 0))],
        out_specs=[pl.BlockSpec((None, N_EXPERTS, n), lambda b: (b, 0, 0)),
                   pl.BlockSpec((None, N_EXPERTS, 128), lambda b: (b, 0, 0))],
        out_shape=[jax.ShapeDtypeStruct((nb, N_EXPERTS, n), I32),
                   jax.ShapeDtypeStruct((nb, N_EXPERTS, 128), I32)],
        compiler_params=_cparams(("arbitrary",)),
        name="select",
    )(aff, tri)


SELECT_STEPS = 152
COMPACT_WIN = 128 + 8
COMPACT_WIN_SMALL = 32
EXPERT_ROW_CHUNK = 256
COMBINE_WIN_SMALL = 64
COMBINE_WIN = 256


def _compact_kernel(s0_ref, hn_ref, slot_ref, aff_ref, xs_ref, g_ref, acc_ref, gacc_ref, *, n, cap):
    b, e = pl.program_id(0), pl.program_id(1)
    nblk = n // 128
    acc_ref[...] = jnp.zeros(acc_ref.shape, F32)
    gacc_ref[...] = jnp.zeros(gacc_ref.shape, F32)

    def body(blk, carry):
        base = (b * N_EXPERTS + e) * 128
        s0 = s0_ref[base + blk]
        s1 = jnp.where(blk + 1 < nblk, s0_ref[base + jnp.minimum(blk + 1, nblk - 1)], cap)
        s0a = pl.multiple_of((s0 // 8) * 8, 8)
        t0 = pl.multiple_of(blk * 128, 128)

        def scatter_rows(win):
            sl = slot_ref[pl.ds(blk, 1), :]
            af = aff_ref[pl.ds(blk, 1), :]
            hit = lax.broadcasted_iota(I32, (win, 128), 0) == (sl - s0a)
            acc_ref[pl.ds(s0a, win), :] += jnp.dot(jnp.where(hit, 1.0, 0.0).astype(BF16),
                                                   hn_ref[pl.ds(t0, 128), :], preferred_element_type=F32)
            gv = jnp.sum(jnp.where(hit, af, 0.0), axis=1, keepdims=True)
            gacc_ref[pl.ds(s0a, win), :] += jnp.broadcast_to(gv, (win, 128))

        small = (s1 - s0a) <= COMPACT_WIN_SMALL

        @pl.when(small)
        def _():
            scatter_rows(COMPACT_WIN_SMALL)

        @pl.when(jnp.logical_not(small))
        def _():
            scatter_rows(COMPACT_WIN)

        return carry

    lax.fori_loop(0, nblk, body, 0)
    xs_ref[...] = acc_ref[0:cap, :].astype(BF16)
    g_ref[...] = gacc_ref[0:cap, :]


def _compact(hn, slot, aff, s0, n, cap, row_blk):
    nb, _, d = hn.shape
    nblk = n // 128
    slot4 = slot.reshape(nb, N_EXPERTS, nblk, 128)
    aff4 = aff.reshape(nb, N_EXPERTS, nblk, 128)
    grid_spec = pltpu.PrefetchScalarGridSpec(
        num_scalar_prefetch=1,
        grid=(nb, N_EXPERTS),
        in_specs=[pl.BlockSpec((None, n, d), lambda b, e, s: (b, row_blk, 0)),
                  pl.BlockSpec((None, None, nblk, 128), lambda b, e, s: (b, e, 0, 0)),
                  pl.BlockSpec((None, None, nblk, 128), lambda b, e, s: (b, e, 0, 0))],
        out_specs=[pl.BlockSpec((None, cap, d), lambda b, e, s: (e, b, 0)),
                   pl.BlockSpec((None, cap, 128), lambda b, e, s: (e, b, 0))],
        scratch_shapes=[pltpu.VMEM((cap + COMPACT_WIN, d), F32), pltpu.VMEM((cap + COMPACT_WIN, 128), F32)],
    )
    return pl.pallas_call(
        functools.partial(_compact_kernel, n=n, cap=cap),
        grid_spec=grid_spec,
        out_shape=[jax.ShapeDtypeStruct((N_EXPERTS, nb * cap, d), BF16),
                   jax.ShapeDtypeStruct((N_EXPERTS, nb * cap, 128), F32)],
        compiler_params=_cparams(("arbitrary", "arbitrary")),
        name="compact",
    )(s0.reshape(-1), hn, slot4, aff4)


def _expert_kernel(*refs, n_in):
    xs = refs[0:n_in]
    gs = refs[n_in:2 * n_in]
    wg_ref, wu_ref, wd_ref = refs[2 * n_in:2 * n_in + 3]
    ys = refs[2 * n_in + 3:3 * n_in + 3]
    accs = refs[3 * n_in + 3:4 * n_in + 3]
    wgu_ref, wdb_ref = refs[4 * n_in + 3:]
    f = pl.program_id(1)
    last = pl.num_programs(1) - 1
    tf = wg_ref.shape[1]
    wgu_ref[:, 0:tf] = wg_ref[...].astype(BF16)
    wgu_ref[:, tf:2 * tf] = wu_ref[...].astype(BF16)
    wdb_ref[...] = wd_ref[...].astype(BF16)

    @pl.when(f == 0)
    def _():
        for acc_ref in accs:
            acc_ref[...] = jnp.zeros(acc_ref.shape, F32)

    chunks = []
    for x_ref, acc_ref in zip(xs, accs):
        m = x_ref.shape[0]
        rc = min(m, EXPERT_ROW_CHUNK)
        chunks += [(x_ref, acc_ref, r0, rc) for r0 in range(0, m, rc)]

    def up(chunk):
        x_ref, _, r0, rc = chunk
        return jnp.dot(x_ref[r0:r0 + rc, :], wgu_ref[...], preferred_element_type=F32)

    def down(chunk, au):
        _, acc_ref, r0, rc = chunk
        hid = (_silu(au[:, 0:tf]) * au[:, tf:2 * tf]).astype(BF16)
        acc_ref[r0:r0 + rc, :] += jnp.dot(hid, wdb_ref[...], preferred_element_type=F32)

    au_prev = up(chunks[0])
    for ci in range(1, len(chunks)):
        au_next = up(chunks[ci])
        down(chunks[ci - 1], au_prev)
        au_prev = au_next
    down(chunks[-1], au_prev)

    @pl.when(f == last)
    def _():
        for g_ref, y_ref, acc_ref in zip(gs, ys, accs):
            gate = g_ref[...]
            for cblk in range(acc_ref.shape[1] // 128):
                sl = slice(cblk * 128, (cblk + 1) * 128)
                y_ref[:, sl] = (acc_ref[:, sl] * gate).astype(BF16)


def _experts(xs_list, g_list, w_gate, w_up, w_down, layer):
    _, ne, d, ff = w_gate.shape
    tf = 256
    n_in = len(xs_list)
    xspec = lambda m, w: pl.BlockSpec((None, m, w), lambda e, f: (e, 0, 0))
    return pl.pallas_call(
        functools.partial(_expert_kernel, n_in=n_in),
        grid=(ne, ff // tf),
        in_specs=([xspec(x.shape[1], d) for x in xs_list] + [xspec(g.shape[1], 128) for g in g_list]
                  + [pl.BlockSpec((None, None, d, tf), lambda e, f: (layer, e, 0, f)),
                     pl.BlockSpec((None, None, d, tf), lambda e, f: (layer, e, 0, f)),
                     pl.BlockSpec((None, None, tf, d), lambda e, f: (layer, e, f, 0))]),
        out_specs=[xspec(x.shape[1], d) for x in xs_list],
        out_shape=[jax.ShapeDtypeStruct(x.shape, BF16) for x in xs_list],
        scratch_shapes=([pltpu.VMEM((x.shape[1], d), F32) for x in xs_list]
                        + [pltpu.VMEM((d, 2 * tf), BF16), pltpu.VMEM((tf, d), BF16)]),
        compiler_params=_cparams(("arbitrary", "arbitrary")),
        name="experts",
    )(*xs_list, *g_list, w_gate, w_up, w_down)


def _combine_kernel(s0_ref, y_ref, slot_ref, h_ref, mod_ref, o_ref, *, cap, win, mod_row_static):
    b, i = pl.program_id(0), pl.program_id(2)
    nblk = pl.num_programs(2)
    mrow = mod_ref[b] if mod_row_static is None else mod_ref[mod_row_static]
    win_small = min(win, COMBINE_WIN_SMALL)

    def window_start(e, w):
        s0 = s0_ref[(b * N_EXPERTS + e) * 128 + i]
        return jnp.minimum((s0 // 16) * 16, cap - w)

    def combine(w):
        slot_t = slot_ref[...]
        jj = lax.broadcasted_iota(I32, (1, w), 1)
        acc = None
        for e in range(N_EXPERTS):
            s0a = pl.multiple_of(window_start(e, w), 16)
            hit = (slot_t[:, e:e + 1] - s0a) == jj
            part = jnp.dot(jnp.where(hit, 1.0, 0.0).astype(BF16), y_ref[e, pl.ds(s0a, w), :],
                           preferred_element_type=F32)
            acc = part if acc is None else acc + part
        o_ref[...] = h_ref[...] + mrow[5:6] * acc

    if win_small == win:
        combine(win)
    else:
        fits = None
        for e in range(N_EXPERTS):
            nxt = s0_ref[(b * N_EXPERTS + e) * 128 + jnp.minimum(i + 1, nblk - 1)]
            end = jnp.where(i + 1 < nblk, nxt, cap)
            ok = end - window_start(e, win_small) <= win_small
            fits = ok if fits is None else jnp.logical_and(fits, ok)

        @pl.when(fits)
        def _():
            combine(win_small)

        @pl.when(jnp.logical_not(fits))
        def _():
            combine(win)


def _combine(y, slot, s0, h, mod, n, cap, tok_blk0, is_ctx):
    nb, t, d = h.shape
    nblk = n // 128
    dcols = d // 2 if d >= 256 else d
    win = min(cap, COMBINE_WIN)
    slot_t = jnp.swapaxes(slot, 1, 2)
    grid_spec = pltpu.PrefetchScalarGridSpec(
        num_scalar_prefetch=1,
        grid=(nb, d // dcols, nblk),
        in_specs=[pl.BlockSpec((N_EXPERTS, cap, dcols), lambda b, c, i, s: (0, b, c)),
                  pl.BlockSpec((None, 128, N_EXPERTS), lambda b, c, i, s: (b, i, 0)),
                  pl.BlockSpec((None, 128, dcols), lambda b, c, i, s: (b, tok_blk0 + i, c)),
                  pl.BlockSpec((8, 6, dcols), lambda b, c, i, s: (0, 0, c))],
        out_specs=pl.BlockSpec((None, 128, dcols), lambda b, c, i, s: (b, tok_blk0 + i, c)),
    )
    return pl.pallas_call(
        functools.partial(_combine_kernel, cap=cap, win=win, mod_row_static=nb if is_ctx else None),
        grid_spec=grid_spec,
        out_shape=jax.ShapeDtypeStruct((nb, t, d), F32),
        input_output_aliases={3: 0},
        compiler_params=_cparams(("arbitrary", "arbitrary", "arbitrary")),
        name="combine",
    )(s0.reshape(-1), y, slot_t, h, mod)


def _moe(h, nw, mod, w_router, w_gate, w_up, w_down, layer, s_lat, l_ctx, with_ctx):
    hn, aff = _router(h, nw, mod, w_router, s_lat)
    streams = [(s_lat, 0, 0, False)]
    if with_ctx:
        streams.append((l_ctx, s_lat // l_ctx, s_lat // 128, True))
    sel = []
    for n, blk_idx, _, _ in streams:
        cap = EC_CAPACITY * n // N_EXPERTS
        slot, s0 = _select(aff, n, blk_idx, cap)
        aff_s = lax.slice_in_dim(aff, blk_idx * n, blk_idx * n + n, axis=2)
        xs, gates = _compact(hn, slot, aff_s, s0, n, cap, blk_idx)
        sel.append((slot, s0, xs, gates, cap))
    ys = _experts([s[2] for s in sel], [s[3] for s in sel], w_gate, w_up, w_down, layer)
    for (n, _, tok_blk0, is_ctx), (slot, s0, _, _, cap), y in zip(streams, sel, ys):
        h = _combine(y, slot, s0, h, mod, n, cap, tok_blk0, is_ctx)
    return h


def _final_norm_kernel(h_ref, w_ref, o_ref):
    x = h_ref[...]
    o_ref[...] = x * lax.rsqrt(jnp.mean(x * x, axis=-1, keepdims=True) + NORM_EPS) * w_ref[...]


def _final_norm(h, w, s_lat):
    nb, _, d = h.shape
    tm = 512
    return pl.pallas_call(
        _final_norm_kernel,
        grid=(nb, s_lat // tm),
        in_specs=[pl.BlockSpec((None, tm, d), lambda b, i: (b, i, 0)),
                  pl.BlockSpec((1, d), lambda b, i: (0, 0))],
        out_specs=pl.BlockSpec((None, tm, d), lambda b, i: (b, i, 0)),
        out_shape=jax.ShapeDtypeStruct((nb, s_lat, d), F32),
        compiler_params=_cparams(("arbitrary", "arbitrary")),
        name="final_norm",
    )(h, w)


def _reorder_w_in(w_in):
    d = w_in.shape[0]
    lr0 = COL_GDN_Q
    qkv0 = lr0 + 2 * GLA_RANK
    ab0 = qkv0 + 3 * GDN_HEADS * GDN_DK + GDN_HEADS * GDN_DV
    n_in = ab0 + 4 * GDN_HEADS
    return jnp.concatenate([w_in[:, :lr0], w_in[:, qkv0:ab0], w_in[:, lr0:qkv0], w_in[:, ab0:n_in],
                            jnp.zeros((d, NP_COLS - n_in), w_in.dtype)], axis=1).astype(BF16)


def _token_mixers(h, nw, mod, w_in, w_out, layer, rpb, gate_w, gate_b, gla_nw, conv_w, a_log, dt_bias, gdn_nw,
                  rope, s_lat, l_ctx):
    p = _inproj(h, nw, mod, _reorder_w_in(w_in), s_lat)
    o_na = _na(p, _na_bias_table(rpb), s_lat, l_ctx)
    gw_p = jnp.zeros((2, 128, GLA_HEADS * GLA_DK), F32)
    for dr in range(2):
        gw_p = gw_p.at[dr, dr * GLA_RANK:(dr + 1) * GLA_RANK].set(gate_w[dr])
    qr, kr, cf, cb = _gla_prep(p, rope[0], rope[1], gw_p, gate_b[:, None, :], s_lat)
    o_gla = _gla_scan(qr, kr, cf, cb, p, s_lat, l_ctx)
    qn, kn, vn = (_gdn_conv(p, conv_w, s_lat, part) for part in range(3))
    o_gdn = _gdn_scan(*_gdn_chunk(qn, kn, vn, p, a_log, dt_bias), s_lat, l_ctx)
    return _outproj(o_na, o_gla, o_gdn, p, gla_nw[None, :], gdn_nw[None, :], w_out, layer, h, mod, s_lat)


def kernel(x, c, ctx, c_ctx, w_ada, b_ada, norm_mix_w, norm_ffn_w, w_in, w_out, na_rpb, gla_gate_w, gla_gate_b,
           gla_norm_w, gdn_conv_w, gdn_a_log, gdn_dt_bias, gdn_norm_w, w_router, w_exp_gate, w_exp_up,
           w_exp_down, final_norm_w):
    nb, s_lat, d = x.shape
    l_ctx = ctx.shape[1]
    depth = w_ada.shape[0]
    assert nb < 8 and s_lat % 256 == 0 and l_ctx == 256 and s_lat // GRID_W >= NA_WIN_R
    h = jnp.concatenate([x, ctx], axis=1)
    cvec = jnp.zeros((8, d), F32).at[:nb].set(c).at[nb].set(c_ctx)
    mods = _ada(cvec, w_ada, b_ada).reshape(depth, 8, 6, d)
    rope = _rope_tables(s_lat, l_ctx)
    w_out = w_out.astype(BF16)
    for l in range(depth):
        h = _token_mixers(h, norm_mix_w[l][None, :], mods[l], w_in[l], w_out, l, na_rpb[l], gla_gate_w[l],
                          gla_gate_b[l], gla_norm_w[l], gdn_conv_w[l], gdn_a_log[l], gdn_dt_bias[l],
                          gdn_norm_w[l], rope, s_lat, l_ctx)
        h = _moe(h, norm_ffn_w[l][None, :], mods[l], w_router[l], w_exp_gate, w_exp_up, w_exp_down, l,
                 s_lat, l_ctx, with_ctx=l < depth - 1)
    return _final_norm(h, final_norm_w[None, :], s_lat)
```

```python
import functools

import numpy as np
import jax
import jax.numpy as jnp
from jax import lax
from jax.experimental import pallas as pl
from jax.experimental.pallas import tpu as pltpu

F32 = jnp.float32
BF16 = jnp.bfloat16
I32 = jnp.int32
HI = lax.Precision.HIGHEST

GRID_W = 64
NA_HEADS, NA_DH, NA_WIN_R, NA_WIN_C = 16, 64, 8, 16
GLA_HEADS, GLA_DK, GLA_DV, GLA_RANK, GLA_TAU = 4, 64, 128, 16, 16.0
GLA_BLK = 16
GDN_HEADS, GDN_DK, GDN_DV, GDN_CONV, GDN_CHUNK = 4, 128, 128, 5, 64
N_EXPERTS, EC_CAPACITY = 16, 2
ROPE_BASE = 10000.0
NORM_EPS = 1e-6
NEG = -1e30

NA_W = NA_HEADS * NA_DH
COL_NA_Q, COL_NA_K, COL_NA_V = 0, NA_W, 2 * NA_W
COL_GLA_Q = 3 * NA_W
COL_GLA_K = COL_GLA_Q + GLA_HEADS * GLA_DK
COL_GLA_V = COL_GLA_K + GLA_HEADS * GLA_DK
COL_GLA_G = COL_GLA_V + GLA_HEADS * GLA_DV
COL_GDN_Q = COL_GLA_G + GLA_HEADS * GLA_DV
COL_GDN_K = COL_GDN_Q + GDN_HEADS * GDN_DK
COL_GDN_V = COL_GDN_K + GDN_HEADS * GDN_DK
COL_GDN_Z = COL_GDN_V + GDN_HEADS * GDN_DV
COL_SMALL = COL_GDN_Z + GDN_HEADS * GDN_DV
SMALL_A, SMALL_B = 2 * GLA_RANK, 2 * GLA_RANK + 2 * GDN_HEADS
INPROJ_TN = 1024
NP_COLS = 7168

VMEM_LIMIT = 56 * 1024 * 1024


def _cparams(sem):
    return pltpu.CompilerParams(dimension_semantics=sem, vmem_limit_bytes=VMEM_LIMIT)


def _sigmoid(x):
    return 1.0 / (1.0 + jnp.exp(-x))


def _silu(x):
    return x * _sigmoid(x)


def _softplus(x):
    return jnp.maximum(x, 0.0) + jnp.log(1.0 + jnp.exp(-jnp.abs(x)))


def _bdot(a, b):
    return jnp.dot(a.astype(BF16), b.astype(BF16), preferred_element_type=F32)


def _dot_nt(a, b):
    return lax.dot_general(a, b, (((1,), (1,)), ((), ())), preferred_element_type=F32)


def _dot_tn(a, b):
    return lax.dot_general(a, b, (((0,), (0,)), ((), ())), preferred_element_type=F32)


def _dot01_rhs(x, m01):
    hi = x.astype(BF16)
    r1 = x - hi.astype(F32)
    mid = r1.astype(BF16)
    lo = (r1 - mid.astype(F32)).astype(BF16)
    d = lambda p: jnp.dot(p, m01, preferred_element_type=F32)
    return d(hi) + d(mid) + d(lo)


def _dot01(m01, x):
    hi = x.astype(BF16)
    r1 = x - hi.astype(F32)
    mid = r1.astype(BF16)
    lo = (r1 - mid.astype(F32)).astype(BF16)
    d = lambda p: jnp.dot(m01, p, preferred_element_type=F32)
    return d(hi) + d(mid) + d(lo)


def _ada_kernel(c_ref, w_ref, b_ref, o_ref):
    o_ref[...] = _bdot(_silu(c_ref[...]), w_ref[...]) + b_ref[...]


def _ada(cvec, w_ada, b_ada):
    depth, d, n6 = w_ada.shape
    tn = 1024 if n6 % 1024 == 0 else 512
    assert n6 % tn == 0
    return pl.pallas_call(
        _ada_kernel,
        grid=(depth, n6 // tn),
        in_specs=[pl.BlockSpec((8, d), lambda l, j: (0, 0)),
                  pl.BlockSpec((None, d, tn), lambda l, j: (l, 0, j)),
                  pl.BlockSpec((None, 1, tn), lambda l, j: (l, 0, j))],
        out_specs=pl.BlockSpec((None, 8, tn), lambda l, j: (l, 0, j)),
        out_shape=jax.ShapeDtypeStruct((depth, 8, n6), F32),
        compiler_params=_cparams(("arbitrary", "arbitrary")),
        name="ada",
    )(cvec, w_ada, b_ada.reshape(depth, 1, n6))


def _mod_norm(x, nw, mod_b, mod_c, row0, s_lat, k_shift):
    ms = jnp.mean(x * x, axis=-1, keepdims=True)
    y = x * lax.rsqrt(ms + NORM_EPS) * nw
    row = row0 + lax.broadcasted_iota(I32, (x.shape[0], 1), 0)
    is_lat = row < s_lat
    shift = jnp.where(is_lat, mod_b[k_shift:k_shift + 1], mod_c[k_shift:k_shift + 1])
    scale = jnp.where(is_lat, mod_b[k_shift + 1:k_shift + 2], mod_c[k_shift + 1:k_shift + 2])
    return y * (1.0 + scale) + shift


def _inproj_kernel(x_ref, nw_ref, mod_ref, w_ref, o_ref, xn_ref, *, s_lat, tm, nb):
    b, i, j = pl.program_id(0), pl.program_id(1), pl.program_id(2)

    @pl.when(j == 0)
    def _():
        rc = tm // 4

        def chunk(ci, carry):
            r = pl.multiple_of(ci * rc, 16)
            xn = _mod_norm(x_ref[pl.ds(r, rc), :], nw_ref[...], mod_ref[b], mod_ref[nb], i * tm + r, s_lat, 0)
            xn_ref[pl.ds(r, rc), :] = xn.astype(BF16)
            return carry

        lax.fori_loop(0, 4, chunk, 0)

    o_ref[...] = jnp.dot(xn_ref[...], w_ref[...].astype(BF16), preferred_element_type=F32)


def _inproj(h, nw, mod, w_p, s_lat):
    nb, t, d = h.shape
    tm = t // 4
    tn = INPROJ_TN
    npc = w_p.shape[1]
    return pl.pallas_call(
        functools.partial(_inproj_kernel, s_lat=s_lat, tm=tm, nb=nb),
        grid=(nb, t // tm, npc // tn),
        in_specs=[pl.BlockSpec((None, tm, d), lambda b, i, j: (b, i, 0)),
                  pl.BlockSpec((1, d), lambda b, i, j: (0, 0)),
                  pl.BlockSpec((8, 6, d), lambda b, i, j: (0, 0, 0)),
                  pl.BlockSpec((d, tn), lambda b, i, j: (0, j))],
        out_specs=pl.BlockSpec((None, tm, tn), lambda b, i, j: (b, i, j)),
        out_shape=jax.ShapeDtypeStruct((nb, t, npc), F32),
        scratch_shapes=[pltpu.VMEM((tm, d), BF16)],
        compiler_params=_cparams(("arbitrary", "arbitrary", "arbitrary")),
        name="inproj",
    )(h, nw, mod, w_p)


NA_NDR = 2 * NA_WIN_R - 1
NA_NDC = 2 * NA_WIN_C - 1


def _na_bias_kernel(rpb_ref, onehot_ref, mask_ref, o_ref):
    t = jnp.dot(rpb_ref[...], onehot_ref[...], precision=HI, preferred_element_type=F32)
    o_ref[...] = jnp.where(mask_ref[...] > 0.0, t, NEG)


def _na_bias_table(rpb):
    nh = rpb.shape[0]
    cq = np.arange(GRID_W)
    dc = np.clip(cq[None, :] - cq[:, None] + NA_WIN_C - 1, 0, NA_NDC - 1)
    cstart = np.clip(cq - NA_WIN_C // 2, 0, GRID_W - NA_WIN_C)
    colmask = (cq[None, :] >= cstart[:, None]) & (cq[None, :] < cstart[:, None] + NA_WIN_C)
    onehot = jnp.asarray(np.arange(128)[:, None] == dc.reshape(1, -1), F32)
    mask = jnp.asarray(colmask.reshape(1, -1), F32)
    rpb_p = jnp.zeros((nh, 16, 128), F32).at[:, :NA_NDR, :NA_NDC].set(rpb)
    full = pl.pallas_call(
        _na_bias_kernel,
        grid=(nh,),
        in_specs=[pl.BlockSpec((None, 16, 128), lambda h: (h, 0, 0)),
                  pl.BlockSpec((128, GRID_W * GRID_W), lambda h: (0, 0)),
                  pl.BlockSpec((1, GRID_W * GRID_W), lambda h: (0, 0))],
        out_specs=pl.BlockSpec((None, 16, GRID_W * GRID_W), lambda h: (h, 0, 0)),
        out_shape=jax.ShapeDtypeStruct((nh, 16, GRID_W * GRID_W), F32),
        compiler_params=_cparams(("arbitrary",)),
        name="na_bias",
    )(rpb_p, onehot, mask)
    t15 = full[:, :NA_NDR].reshape(nh, NA_NDR, GRID_W, GRID_W)
    return jnp.concatenate([t15[:, :-1], t15[:, 1:]], axis=-1)


def _softmax_pv_many(s_lists, v_lists):
    n = len(s_lists)
    ms = []
    for sl in s_lists:
        m = sl[0].max(axis=-1, keepdims=True)
        for s in sl[1:]:
            m = jnp.maximum(m, s.max(axis=-1, keepdims=True))
        ms.append(m)
    ps = [[jnp.exp(s - ms[c]) for s in s_lists[c]] for c in range(n)]
    dens = []
    for c in range(n):
        den = ps[c][0].sum(axis=-1, keepdims=True)
        for p in ps[c][1:]:
            den = den + p.sum(axis=-1, keepdims=True)
        dens.append(den)
    outs = []
    for c in range(n):
        o = jnp.dot(ps[c][0].astype(BF16), v_lists[c][0], preferred_element_type=F32)
        for p, v in zip(ps[c][1:], v_lists[c][1:]):
            o = o + jnp.dot(p.astype(BF16), v, preferred_element_type=F32)
        outs.append(o)
    return [o / den for o, den in zip(outs, dens)]


NA_ROWS_PER_STEP = 8


def _na_kernel(q_ref, k_ref, v_ref, bias_ref, o_ref, kb_ref, vb_ref, *, s_lat, l_ctx):
    rows = s_lat // GRID_W
    nwin = NA_WIN_R * GRID_W
    scale = NA_DH ** -0.5
    kb_ref[...] = k_ref[...].astype(BF16)
    vb_ref[...] = v_ref[...].astype(BF16)
    lane = lax.broadcasted_iota(I32, (1, 2 * NA_DH), 1)
    head_lanes = (lane < NA_DH, lane >= NA_DH)
    kc = kb_ref[pl.ds(s_lat, l_ctx), :]
    vc = vb_ref[pl.ds(s_lat, l_ctx), :]

    qc = q_ref[pl.ds(s_lat, l_ctx), :] * scale
    qhs = [jnp.where(head_lanes[hh], qc, 0.0).astype(BF16) for hh in range(2)]
    outs = _softmax_pv_many([[_dot_nt(qh, kc)] for qh in qhs], [[vc], [vc]])
    o_ref[pl.ds(s_lat, l_ctx), :] = jnp.where(head_lanes[0], outs[0], outs[1])

    def body(it, carry):
        s_lists, v_lists, starts = [], [], []
        for rr in range(NA_ROWS_PER_STEP):
            r = it * NA_ROWS_PER_STEP + rr
            r0 = jnp.clip(r - NA_WIN_R // 2, 0, rows - NA_WIN_R)
            d = r - r0
            qs = pl.multiple_of(r * GRID_W, GRID_W)
            ks = pl.multiple_of(r0 * GRID_W, GRID_W)
            starts.append(qs)
            q = q_ref[pl.ds(qs, GRID_W), :] * scale
            kw = kb_ref[pl.ds(ks, nwin), :]
            vw = vb_ref[pl.ds(ks, nwin), :]
            q2 = jnp.concatenate([jnp.where(head_lanes[hh], q, 0.0) for hh in range(2)], axis=0).astype(BF16)
            bias = jnp.concatenate(
                [jnp.concatenate([bias_ref[hh, 2 * m + NA_WIN_R - 1 - d] for hh in range(2)], axis=0)
                 for m in range(NA_WIN_R // 2)], axis=1)
            s_lists.append([_dot_nt(q2, kw) + bias, _dot_nt(q2, kc)])
            v_lists.append([vw, vc])
        res = _softmax_pv_many(s_lists, v_lists)
        for rr in range(NA_ROWS_PER_STEP):
            o_ref[pl.ds(starts[rr], GRID_W), :] = jnp.where(head_lanes[0], res[rr][0:GRID_W], res[rr][GRID_W:2 * GRID_W])
        return carry

    lax.fori_loop(0, rows // NA_ROWS_PER_STEP, body, 0)


def _na(p, bias_tbl, s_lat, l_ctx):
    nb, t, _ = p.shape
    blk = lambda off: pl.BlockSpec((None, t, 128), lambda b, g, off=off: (b, 0, off // 128 + g))
    return pl.pallas_call(
        functools.partial(_na_kernel, s_lat=s_lat, l_ctx=l_ctx),
        grid=(nb, NA_HEADS // 2),
        in_specs=[blk(COL_NA_Q), blk(COL_NA_K), blk(COL_NA_V),
                  pl.BlockSpec((2, NA_NDR - 1, GRID_W, 2 * GRID_W), lambda b, g: (g, 0, 0, 0))],
        out_specs=pl.BlockSpec((None, t, 128), lambda b, g: (b, 0, g)),
        out_shape=jax.ShapeDtypeStruct((nb, t, NA_W), F32),
        scratch_shapes=[pltpu.VMEM((t, 128), BF16), pltpu.VMEM((t, 128), BF16)],
        compiler_params=_cparams(("arbitrary", "arbitrary")),
        name="na",
    )(p, p, p, bias_tbl)


def _rope_tables(s_lat, l_ctx):
    width = GLA_HEADS * GLA_DK
    nf = GLA_DK // 4
    pos = np.arange(s_lat)
    lane = np.arange(width)
    sub = lane % GLA_DK
    freqs = ROPE_BASE ** (-jnp.arange(nf, dtype=F32) / nf)
    p_sel = jnp.where((sub < GLA_DK // 2)[None, :], jnp.asarray(pos // GRID_W, F32)[:, None],
                      jnp.asarray(pos % GRID_W, F32)[:, None])
    ang = p_sel * freqs[sub % nf][None, :]
    sign = np.where((sub % (2 * nf)) < nf, -1.0, 1.0).astype(np.float32)
    cos = jnp.concatenate([jnp.cos(ang), jnp.ones((l_ctx, width), F32)], axis=0)
    sin = jnp.concatenate([jnp.sin(ang) * sign[None, :], jnp.zeros((l_ctx, width), F32)], axis=0)
    return cos, sin


def _block_tri(n, blk):
    i = np.arange(n)
    same = (i[:, None] // blk) == (i[None, :] // blk)
    lower = same & (i[None, :] <= i[:, None])
    upper = same & (i[None, :] >= i[:, None])
    return jnp.asarray(lower, BF16), jnp.asarray(upper, BF16)


def _gla_prep_kernel(q_ref, k_ref, sm_ref, cos_ref, sin_ref, gw_ref, gb_ref, tl_ref, tu_ref,
                     qr_ref, kr_ref, cf_ref, cb_ref):
    width = GLA_HEADS * GLA_DK
    nf = GLA_DK // 4
    lane = lax.broadcasted_iota(I32, (1, width), 1)
    first = (lane % (2 * nf)) < nf
    cos, sin = cos_ref[...], sin_ref[...]

    def rope(x):
        swapped = jnp.where(first, pltpu.roll(x, width - nf, 1), pltpu.roll(x, nf, 1))
        return x * cos + swapped * sin

    qr_ref[...] = rope(q_ref[...]) * (GLA_DK ** -0.5)
    kr_ref[...] = rope(k_ref[...])
    sm = sm_ref[...]
    for dr, (tri_ref, out_ref) in enumerate(((tl_ref, cf_ref), (tu_ref, cb_ref))):
        z = jnp.dot(sm, gw_ref[dr], precision=HI, preferred_element_type=F32) + gb_ref[dr]
        log_a = (jnp.minimum(z, 0.0) - jnp.log(1.0 + jnp.exp(-jnp.abs(z)))) * (1.0 / GLA_TAU)
        out_ref[...] = _dot01(tri_ref[...], log_a)


def _gla_prep(p, cos, sin, gw_p, gb, s_lat):
    nb, t, _ = p.shape
    width = GLA_HEADS * GLA_DK
    tr = 256
    tl, tu = _block_tri(tr, GLA_BLK)
    row_blk = lambda w, off: pl.BlockSpec((None, tr, w), lambda b, i, off=off, w=w: (b, i, off // w))
    tab = pl.BlockSpec((tr, width), lambda b, i: (i, 0))
    full = lambda shp: pl.BlockSpec(shp, lambda b, i, n=len(shp): (0,) * n)
    out = pl.BlockSpec((None, tr, width), lambda b, i: (b, i, 0))
    return pl.pallas_call(
        _gla_prep_kernel,
        grid=(nb, t // tr),
        in_specs=[row_blk(width, COL_GLA_Q), row_blk(width, COL_GLA_K), row_blk(128, COL_SMALL),
                  tab, tab, full((2, 128, width)), full((2, 1, width)), full((tr, tr)), full((tr, tr))],
        out_specs=[out] * 4,
        out_shape=[jax.ShapeDtypeStruct((nb, t, width), F32)] * 4,
        compiler_params=_cparams(("arbitrary", "arbitrary")),
        name="gla_prep",
    )(p, p, p, cos, sin, gw_p, gb, tl, tu)


def _gla_scan_kernel(qr_ref, kr_ref, cf_ref, cb_ref, v_ref, r3_ref, o_ref, st_ref, *, s_lat, l_ctx):
    nlat, nctx = s_lat // GLA_BLK, l_ctx // GLA_BLK
    o_ref[...] = jnp.zeros(o_ref.shape, F32)
    st_ref[...] = jnp.zeros(st_ref.shape, F32)
    sub = lax.broadcasted_iota(I32, (GLA_BLK, 2 * GLA_DK), 0)
    bd = (lax.broadcasted_iota(I32, (2 * GLA_DV, 2 * GLA_DK), 0) // GLA_DV
          == lax.broadcasted_iota(I32, (2 * GLA_DV, 2 * GLA_DK), 1) // GLA_DK)

    def step(i, carry):
        in_ctx = i < nctx
        jf = jnp.where(in_ctx, nlat + i, i - nctx)
        jb = jnp.where(in_ctx, nlat + nctx - 1 - i, nlat - 1 - (i - nctx))
        dirs = (0, 1)
        rs = [pl.multiple_of(j * GLA_BLK, GLA_BLK) for j in (jf, jb)]
        qs = [qr_ref[pl.ds(r, GLA_BLK), :] for r in rs]
        ks = [kr_ref[pl.ds(r, GLA_BLK), :] for r in rs]
        cums = [c_ref[pl.ds(r, GLA_BLK), :] for c_ref, r in zip((cf_ref, cb_ref), rs)]
        vs = [v_ref[pl.ds(r, GLA_BLK), :] for r in rs]
        tots = [cums[0][GLA_BLK - 1:GLA_BLK], cums[1][0:1]]
        sts = [st_ref[dr] for dr in dirs]
        qds = [(qs[dr] * jnp.exp(cums[dr])).astype(BF16) for dr in dirs]
        kds = [(ks[dr] * jnp.exp(tots[dr] - cums[dr])).astype(BF16) for dr in dirs]
        o_states = [_dot_nt(qds[dr], sts[dr].astype(BF16)) for dr in dirs]
        upds = [_dot_tn(vs[dr].astype(BF16), kds[dr]) for dr in dirs]
        xs = []
        for dr in dirs:
            tiles = []
            for s in range(GLA_BLK):
                valid = (sub >= s) if dr == 0 else (sub <= s)
                w = jnp.exp(jnp.where(valid, cums[dr] - cums[dr][s:s + 1], NEG))
                tiles.append((w * qs[dr] * ks[dr][s:s + 1]).astype(BF16))
            xs.append(jnp.concatenate(tiles, axis=0))
        ress = [jnp.dot(x, r3_ref[...], preferred_element_type=F32) for x in xs]
        for dr in dirs:
            st_ref[dr] = sts[dr] * jnp.exp(tots[dr]) + jnp.where(bd, upds[dr], 0.0)
        for dr in dirs:
            o_diag = ress[dr][0:GLA_BLK] * vs[dr][0:1]
            for s in range(1, GLA_BLK):
                o_diag = o_diag + ress[dr][s * GLA_BLK:(s + 1) * GLA_BLK] * vs[dr][s:s + 1]
            o_ref[pl.ds(rs[dr], GLA_BLK), :] += o_states[dr] + o_diag
        return carry

    lax.fori_loop(0, nlat + nctx, step, 0, unroll=8)


def _gla_scan(qr, kr, cf, cb, p, s_lat, l_ctx):
    nb, t, _ = p.shape
    lanes = 2 * GLA_DK
    r3 = jnp.asarray((np.arange(lanes)[:, None] // GLA_DK) == (np.arange(2 * GLA_DV)[None, :] // GLA_DV), BF16)
    blk = pl.BlockSpec((None, t, lanes), lambda b, g: (b, 0, g))
    return pl.pallas_call(
        functools.partial(_gla_scan_kernel, s_lat=s_lat, l_ctx=l_ctx),
        grid=(nb, GLA_HEADS // 2),
        in_specs=[blk, blk, blk, blk,
                  pl.BlockSpec((None, t, 2 * GLA_DV), lambda b, g: (b, 0, COL_GLA_V // (2 * GLA_DV) + g)),
                  pl.BlockSpec((lanes, 2 * GLA_DV), lambda b, g: (0, 0))],
        out_specs=pl.BlockSpec((None, t, 2 * GLA_DV), lambda b, g: (b, 0, g)),
        out_shape=jax.ShapeDtypeStruct((nb, t, GLA_HEADS * GLA_DV), F32),
        scratch_shapes=[pltpu.VMEM((2, 2 * GLA_DV, lanes), F32)],
        compiler_params=_cparams(("arbitrary", "arbitrary")),
        name="gla_scan",
    )(qr, kr, cf, cb, p, r3)


def _gdn_conv_kernel(x_ref, w_ref, o_ref, *, s_lat, normalize, scale):
    t = x_ref.shape[0]
    x = x_ref[...]
    tpos = lax.broadcasted_iota(I32, (t, 1), 0)
    half = GDN_CONV // 2
    acc = x * w_ref[half:half + 1, :]
    for j in range(GDN_CONV):
        if j == half:
            continue
        dlt = j - half
        src = tpos + dlt
        ok = (src >= 0) & (src < t) & ((src < s_lat) == (tpos < s_lat))
        acc = acc + jnp.where(ok, pltpu.roll(x, (-dlt) % t, 0), 0.0) * w_ref[j:j + 1, :]
    y = _silu(acc)
    if normalize:
        y = y * lax.rsqrt(jnp.sum(y * y, axis=-1, keepdims=True) + 1e-6) * scale
    o_ref[...] = y


def _gdn_conv(p, conv_w, s_lat, part):
    nb, t, _ = p.shape
    col = (COL_GDN_Q, COL_GDN_K, COL_GDN_V)[part]
    return pl.pallas_call(
        functools.partial(_gdn_conv_kernel, s_lat=s_lat, normalize=part < 2,
                          scale=GDN_DK ** -0.5 if part == 0 else 1.0),
        grid=(nb, GDN_HEADS),
        in_specs=[pl.BlockSpec((None, t, 128), lambda b, g: (b, 0, col // 128 + g)),
                  pl.BlockSpec((GDN_CONV, 128), lambda b, g: (0, part * GDN_HEADS + g))],
        out_specs=pl.BlockSpec((None, t, 128), lambda b, g: (b, 0, g)),
        out_shape=jax.ShapeDtypeStruct((nb, t, GDN_HEADS * 128), F32),
        compiler_params=_cparams(("arbitrary", "arbitrary")),
        name=f"gdn_conv{part}",
    )(p, conv_w)


GDN_HEADS_PER_STEP = 2


def _gdn_chunk_kernel(q_ref, k_ref, v_ref, sm_ref, alog_ref, dtb_ref,
                      u_ref, w_ref, qd_ref, kd_ref, a_ref, gl_ref, *, tc):
    g = pl.program_id(1)
    cs = GDN_CHUNK
    nh = GDN_HEADS_PER_STEP
    lane = lax.broadcasted_iota(I32, (1, 128), 1)
    subl = lax.broadcasted_iota(I32, (128, 1), 0)
    ti = lax.broadcasted_iota(I32, (tc, tc), 0)
    ui = lax.broadcasted_iota(I32, (tc, tc), 1)
    same = (ti // cs) == (ui // cs)
    eye = jnp.where(ti == ui, 1.0, 0.0)
    sm = sm_ref[...]
    sm_t = sm.T
    qs = [q_ref[:, hh * 128:(hh + 1) * 128] for hh in range(nh)]
    ks = [k_ref[:, hh * 128:(hh + 1) * 128] for hh in range(nh)]
    vs = [v_ref[:, hh * 128:(hh + 1) * 128] for hh in range(nh)]
    kbs = [k.astype(BF16) for k in ks]
    kks = [_dot_nt(kb, kb) for kb in kbs]
    qks = [_dot_nt(q.astype(BF16), kb) for q, kb in zip(qs, kbs)]
    chains = [(hh, dr) for hh in range(nh) for dr in range(2)]
    ms, decays, betas, e_gcs = [], [], [], []
    for hh, dr in chains:
        head = g * nh + hh
        ca = SMALL_A + GDN_HEADS * dr + head
        cbeta = SMALL_B + GDN_HEADS * dr + head
        a_col = jnp.sum(jnp.where(lane == ca, sm, 0.0), axis=1, keepdims=True)
        b_col = jnp.sum(jnp.where(lane == cbeta, sm, 0.0), axis=1, keepdims=True)
        a_row = jnp.sum(jnp.where(subl == ca, sm_t, 0.0), axis=0, keepdims=True)
        neg_rate = -jnp.exp(alog_ref[dr, hh])
        g_col = neg_rate * _softplus(a_col + dtb_ref[dr, hh])
        g_row = neg_rate * _softplus(a_row + dtb_ref[dr, hh])
        beta = _sigmoid(b_col)
        incl = same & ((ui <= ti) if dr == 0 else (ui >= ti))
        strict = same & ((ui < ti) if dr == 0 else (ui > ti))
        incl_t = same & ((ti <= ui) if dr == 0 else (ti >= ui))
        gc_col = jnp.sum(jnp.where(incl, g_row, 0.0), axis=1, keepdims=True)
        gc_row = jnp.sum(jnp.where(incl_t, g_col, 0.0), axis=0, keepdims=True)
        gc_tot = jnp.sum(jnp.where(same, g_row, 0.0), axis=1, keepdims=True)
        decay = jnp.exp(jnp.where(incl, gc_col - gc_row, NEG))
        ms.append(jnp.where(strict, beta * kks[hh] * decay, 0.0))
        decays.append(decay)
        betas.append(beta)
        e_gcs.append(jnp.exp(gc_col))
        kd_ref[hh, dr] = (ks[hh] * jnp.exp(gc_tot - gc_col)).astype(BF16)
        for c in range(tc // cs):
            gl_ref[hh, dr, c * 8:(c + 1) * 8, :] = jnp.broadcast_to(jnp.exp(gc_tot[c * cs:c * cs + 1]), (8, 128))
    invs = [eye - m for m in ms]
    mks = [_bdot(m, m) for m in ms]
    for lvl in range(5):
        invs = [inv + _bdot(inv, mk) for inv, mk in zip(invs, mks)]
        if lvl < 4:
            mks = [_bdot(mk, mk) for mk in mks]
    sols = [_bdot(invs[ci], jnp.concatenate([vs[hh] * betas[ci], ks[hh] * (betas[ci] * e_gcs[ci])], axis=1))
            for ci, (hh, dr) in enumerate(chains)]
    for ci, (hh, dr) in enumerate(chains):
        u_ref[hh, dr] = sols[ci][:, :GDN_DV]
        w_ref[hh, dr] = sols[ci][:, GDN_DV:].astype(BF16)
        qd_ref[hh, dr] = (qs[hh] * e_gcs[ci]).astype(BF16)
        aqk = qks[hh] * decays[ci]
        for c in range(tc // cs):
            blk = aqk[c * cs:(c + 1) * cs]
            folded = blk[:, 0:128]
            for piece in range(1, tc // 128):
                folded = folded + blk[:, piece * 128:(piece + 1) * 128]
            a_ref[hh, dr, c * cs:(c + 1) * cs, :] = folded.astype(BF16)


def _gdn_chunk(qn, kn, vn, p, a_log, dt_bias):
    nb, t, _ = p.shape
    tc = 256
    nh = GDN_HEADS_PER_STEP
    nch = t // GDN_CHUNK
    rb = lambda: pl.BlockSpec((None, tc, nh * 128), lambda b, g, i: (b, i, g))
    par = pl.BlockSpec((2, nh, 1, 1), lambda b, g, i: (0, g, 0, 0))
    ob = lambda w: pl.BlockSpec((None, nh, 2, tc, w), lambda b, g, i: (b, g, 0, i, 0))
    shp = lambda w, dt: jax.ShapeDtypeStruct((nb, GDN_HEADS, 2, t, w), dt)
    return pl.pallas_call(
        functools.partial(_gdn_chunk_kernel, tc=tc),
        grid=(nb, GDN_HEADS // nh, t // tc),
        in_specs=[rb(), rb(), rb(),
                  pl.BlockSpec((None, tc, 128), lambda b, g, i: (b, i, COL_SMALL // 128)), par, par],
        out_specs=[ob(128), ob(128), ob(128), ob(128), ob(128),
                   pl.BlockSpec((None, nh, 2, (tc // GDN_CHUNK) * 8, 128), lambda b, g, i: (b, g, 0, i, 0))],
        out_shape=[shp(128, F32), shp(128, BF16), shp(128, BF16), shp(128, BF16), shp(128, BF16),
                   jax.ShapeDtypeStruct((nb, GDN_HEADS, 2, nch * 8, 128), F32)],
        compiler_params=_cparams(("arbitrary", "arbitrary", "arbitrary")),
        name="gdn_chunk",
    )(qn, kn, vn, p, a_log.reshape(2, GDN_HEADS, 1, 1).astype(F32), dt_bias.reshape(2, GDN_HEADS, 1, 1).astype(F32))


GDN_SCAN_ROWS = 256


def _gdn_scan_kernel(*refs):
    fwd, bwd = refs[0:6], refs[6:12]
    o_refs = refs[12:14]
    st_ref = refs[14]
    cs = GDN_CHUNK
    ncg = GDN_SCAN_ROWS // cs
    chains = [(dr, hh) for dr in range(2) for hh in range(GDN_HEADS)]

    @pl.when(pl.program_id(1) == 0)
    def _():
        st_ref[...] = jnp.zeros(st_ref.shape, F32)

    for ci in range(ncg):
        rows = [slice(ci * cs, (ci + 1) * cs), slice((ncg - 1 - ci) * cs, (ncg - ci) * cs)]
        gls = [slice(ci * 8, ci * 8 + 1), slice((ncg - 1 - ci) * 8, (ncg - 1 - ci) * 8 + 1)]

        def piece(k, dr, hh):
            return (fwd, bwd)[dr][k][hh, rows[dr], :]

        sts = [st_ref[dr, hh] for dr, hh in chains]
        stbs = [st.astype(BF16) for st in sts]
        wss = [jnp.dot(piece(1, dr, hh), stb, preferred_element_type=F32) for (dr, hh), stb in zip(chains, stbs)]
        qss = [jnp.dot(piece(2, dr, hh), stb, preferred_element_type=F32) for (dr, hh), stb in zip(chains, stbs)]
        vnbs = [(piece(0, dr, hh) - ws).astype(BF16) for (dr, hh), ws in zip(chains, wss)]
        avs = [jnp.dot(piece(4, dr, hh), jnp.concatenate([vnb, vnb], axis=0), preferred_element_type=F32)
               for (dr, hh), vnb in zip(chains, vnbs)]
        upds = [_dot_tn(piece(3, dr, hh), vnb) for (dr, hh), vnb in zip(chains, vnbs)]
        for (dr, hh), st, upd in zip(chains, sts, upds):
            st_ref[dr, hh] = st * (fwd, bwd)[dr][5][hh, gls[dr], :] + upd
        for (dr, hh), qs, av in zip(chains, qss, avs):
            o_refs[dr][rows[dr], hh * GDN_DV:(hh + 1) * GDN_DV] = qs + av


def _gdn_scan(u, w, qd, kd, a, gl, s_lat, l_ctx):
    nb, nh, _, t, _ = u.shape
    tg = GDN_SCAN_ROWS
    assert l_ctx == tg and s_lat % tg == 0
    ngl = s_lat // tg
    grp = (lambda i: jnp.where(i == 0, ngl, i - 1),
           lambda i: jnp.where(i == 0, ngl, ngl - i))
    rows8 = (tg // GDN_CHUNK) * 8
    in_specs = []
    for dr in range(2):
        for wd in (128,) * 5:
            in_specs.append(pl.BlockSpec((None, nh, None, tg, wd), lambda b, i, dr=dr: (b, 0, dr, grp[dr](i), 0)))
        in_specs.append(pl.BlockSpec((None, nh, None, rows8, 128), lambda b, i, dr=dr: (b, 0, dr, grp[dr](i), 0)))
    return pl.pallas_call(
        _gdn_scan_kernel,
        grid=(nb, ngl + 1),
        in_specs=in_specs,
        out_specs=[pl.BlockSpec((None, tg, nh * GDN_DV), lambda b, i, dr=dr: (b, grp[dr](i), 0)) for dr in range(2)],
        out_shape=[jax.ShapeDtypeStruct((nb, t, nh * GDN_DV), F32)] * 2,
        scratch_shapes=[pltpu.VMEM((2, nh, GDN_DK, GDN_DV), F32)],
        compiler_params=_cparams(("arbitrary", "arbitrary")),
        name="gdn_scan",
    )(u, w, qd, kd, a, gl, u, w, qd, kd, a, gl)


def _head_norm_gate(o, gate, nw):
    y = o * lax.rsqrt(jnp.mean(o * o, axis=-1, keepdims=True) + NORM_EPS) * nw
    return y * _silu(gate)


def _outproj_kernel(na_ref, gla_ref, gg_ref, gdnf_ref, gdnb_ref, gz_ref, gnw_ref, dnw_ref, w_ref, h_ref, mod_ref,
                    o_ref, lhs_ref, *, s_lat, tm, nb):
    b, i, j = pl.program_id(0), pl.program_id(1), pl.program_id(2)

    @pl.when(j == 0)
    def _():
        lhs_ref[:, 0:NA_W] = na_ref[...].astype(BF16)
        for hh in range(GLA_HEADS):
            sl = slice(hh * GLA_DV, (hh + 1) * GLA_DV)
            y = _head_norm_gate(gla_ref[:, sl], gg_ref[:, sl], gnw_ref[...])
            lhs_ref[:, NA_W + hh * GLA_DV:NA_W + (hh + 1) * GLA_DV] = y.astype(BF16)
        base = NA_W + GLA_HEADS * GLA_DV
        for hh in range(GDN_HEADS):
            sl = slice(hh * GDN_DV, (hh + 1) * GDN_DV)
            y = _head_norm_gate(gdnf_ref[:, sl] + gdnb_ref[:, sl], gz_ref[:, sl], dnw_ref[...])
            lhs_ref[:, base + hh * GDN_DV:base + (hh + 1) * GDN_DV] = y.astype(BF16)

    mix = jnp.dot(lhs_ref[...], w_ref[...].astype(BF16), preferred_element_type=F32)
    row = i * tm + lax.broadcasted_iota(I32, (tm, 1), 0)
    gate = jnp.where(row < s_lat, mod_ref[b][2:3], mod_ref[nb][2:3])
    o_ref[...] = h_ref[...] + gate * mix


def _outproj(o_na, o_gla, o_gdn, p, gla_nw, gdn_nw, w_out, layer, h, mod, s_lat):
    nb, t, d = h.shape
    tm = t // 8
    tn = min(1024, d)
    mix_w = w_out.shape[1]
    gw = GLA_HEADS * GLA_DV
    rb = lambda w, c=0: pl.BlockSpec((None, tm, w), lambda b, i, j, c=c, w=w: (b, i, c // w))
    return pl.pallas_call(
        functools.partial(_outproj_kernel, s_lat=s_lat, tm=tm, nb=nb),
        grid=(nb, t // tm, d // tn),
        in_specs=[rb(NA_W), rb(gw), rb(gw, COL_GLA_G), rb(gw), rb(gw), rb(gw, COL_GDN_Z),
                  pl.BlockSpec((1, GLA_DV), lambda b, i, j: (0, 0)),
                  pl.BlockSpec((1, GDN_DV), lambda b, i, j: (0, 0)),
                  pl.BlockSpec((None, mix_w, tn), lambda b, i, j: (layer, 0, j)),
                  pl.BlockSpec((None, tm, tn), lambda b, i, j: (b, i, j)),
                  pl.BlockSpec((8, 6, tn), lambda b, i, j: (0, 0, j))],
        out_specs=pl.BlockSpec((None, tm, tn), lambda b, i, j: (b, i, j)),
        out_shape=jax.ShapeDtypeStruct((nb, t, d), F32),
        scratch_shapes=[pltpu.VMEM((tm, mix_w), BF16)],
        compiler_params=_cparams(("arbitrary", "arbitrary", "arbitrary")),
        name="outproj",
    )(o_na, o_gla, p, o_gdn[0], o_gdn[1], p, gla_nw, gdn_nw, w_out, h, mod)


def _router_kernel(h_ref, nw_ref, mod_ref, wr_ref, hn_ref, aff_ref, *, s_lat, tm, nb):
    b, i = pl.program_id(0), pl.program_id(1)
    hn = _mod_norm(h_ref[...], nw_ref[...], mod_ref[b], mod_ref[nb], i * tm, s_lat, 3)
    hn_ref[...] = hn.astype(BF16)
    logits = jnp.dot(hn, wr_ref[...], precision=HI, preferred_element_type=F32).T[0:N_EXPERTS]
    e = jnp.exp(logits - logits.max(axis=0, keepdims=True))
    aff_ref[...] = e / e.sum(axis=0, keepdims=True)


def _router(h, nw, mod, w_router, s_lat):
    nb, t, d = h.shape
    tm = 256
    w_pad = jnp.zeros((d, 128), F32).at[:, :N_EXPERTS].set(w_router)
    return pl.pallas_call(
        functools.partial(_router_kernel, s_lat=s_lat, tm=tm, nb=nb),
        grid=(nb, t // tm),
        in_specs=[pl.BlockSpec((None, tm, d), lambda b, i: (b, i, 0)),
                  pl.BlockSpec((1, d), lambda b, i: (0, 0)),
                  pl.BlockSpec((8, 6, d), lambda b, i: (0, 0, 0)),
                  pl.BlockSpec((d, 128), lambda b, i: (0, 0))],
        out_specs=[pl.BlockSpec((None, tm, d), lambda b, i: (b, i, 0)),
                   pl.BlockSpec((None, N_EXPERTS, tm), lambda b, i: (b, 0, i))],
        out_shape=[jax.ShapeDtypeStruct((nb, t, d), BF16),
                   jax.ShapeDtypeStruct((nb, N_EXPERTS, t), F32)],
        compiler_params=_cparams(("arbitrary", "arbitrary")),
        name="router",
    )(h, nw, mod, w_pad)


def _lane_prefix(x01, tri):
    n = x01.shape[1]
    lane = lax.broadcasted_iota(I32, (1, 128), 1)
    off = jnp.zeros((x01.shape[0], 1), F32)
    before = jnp.zeros((x01.shape[0], 128), F32)
    outs = []
    for blk in range(n // 128):
        before = jnp.where(lane == blk, off, before)
        cb = jnp.dot(x01[:, blk * 128:(blk + 1) * 128].astype(BF16), tri, preferred_element_type=F32) + off
        outs.append(cb)
        off = cb[:, 127:128]
    return jnp.concatenate(outs, axis=1), before


def _select_kernel(aff_ref, tri_ref, slot_ref, s0_ref, *, cap):
    aff = aff_ref[...]
    ne = aff.shape[0]

    def bisect(_, c):
        lo, hi = c
        m2 = 0.5 * (lo + hi)
        m1 = 0.5 * (lo + m2)
        m3 = 0.5 * (m2 + hi)
        ok1, ok2, ok3 = (jnp.sum(jnp.where(aff >= m, 1, 0), axis=1, keepdims=True) >= cap for m in (m1, m2, m3))
        new_lo = jnp.where(ok3, m3, jnp.where(ok2, m2, jnp.where(ok1, m1, lo)))
        new_hi = jnp.where(ok3, hi, jnp.where(ok2, m3, jnp.where(ok1, m2, m1)))
        return new_lo, new_hi

    lo, _ = lax.fori_loop(0, SELECT_STEPS // 2, bisect, (jnp.zeros((ne, 1), F32), jnp.full((ne, 1), 2.0, F32)))
    thr = jnp.min(jnp.where(aff >= lo, aff, 2.0), axis=1, keepdims=True)
    gt = aff > thr
    eq = aff == thr
    need = (cap - jnp.sum(jnp.where(gt, 1, 0), axis=1, keepdims=True)).astype(F32)
    eq_f = jnp.where(eq, 1.0, 0.0)
    eq_incl, _ = _lane_prefix(eq_f, tri_ref[...])
    sel = gt | (eq & ((eq_incl - eq_f) < need))
    sel_f = jnp.where(sel, 1.0, 0.0)
    pos_incl, before = _lane_prefix(sel_f, tri_ref[...])
    slot_ref[...] = jnp.where(sel, pos_incl - 1.0, -1.0).astype(I32)
    s0_ref[...] = before.astype(I32)


def _select(aff, n, blk_idx, cap):
    nb = aff.shape[0]
    tri = jnp.asarray(np.arange(128)[:, None] <= np.arange(128)[None, :], BF16)
    return pl.pallas_call(
        functools.partial(_select_kernel, cap=cap),
        grid=(nb,),
        in_specs=[pl.BlockSpec((None, N_EXPERTS, n), lambda b: (b, 0, blk_idx)),
                  pl.BlockSpec((128, 128), lambda b: (0, 0))],
        out_specs=[pl.BlockSpec((None, N_EXPERTS, n), lambda b: (b, 0, 0)),
                   pl.BlockSpec((None, N_EXPERTS, 128), lambda b: (b, 0, 0))],
        out_shape=[jax.ShapeDtypeStruct((nb, N_EXPERTS, n), I32),
                   jax.ShapeDtypeStruct((nb, N_EXPERTS, 128), I32)],
        compiler_params=_cparams(("arbitrary",)),
        name="select",
    )(aff, tri)


SELECT_STEPS = 152
COMPACT_WIN = 128 + 16
COMPACT_WIN_SMALL = 48
EXPERT_ROW_CHUNK = 256
COMBINE_WIN_SMALL = 48
COMBINE_WIN = 256


def _window_starts(s0_ref, b, i, nblk, cap, w):
    starts, fits = [], None
    for e in range(N_EXPERTS):
        base = (b * N_EXPERTS + e) * 128
        start = jnp.minimum((s0_ref[base + i] // 16) * 16, cap - w)
        end = jnp.where(i + 1 < nblk, s0_ref[base + jnp.minimum(i + 1, nblk - 1)], cap)
        ok = end - start <= w
        fits = ok if fits is None else jnp.logical_and(fits, ok)
        starts.append(pl.multiple_of(start, 16))
    return starts, fits


def _compact_kernel(s0_ref, hn_ref, slot_ref, aff_ref, xs_ref, g_ref, *, cap):
    b, c, i = pl.program_id(0), pl.program_id(1), pl.program_id(2)
    nblk = pl.num_programs(2)

    @pl.when(i == 0)
    def _():
        xs_ref[...] = jnp.zeros(xs_ref.shape, BF16)

    @pl.when(jnp.logical_and(i == 0, c == 0))
    def _():
        g_ref[...] = jnp.zeros(g_ref.shape, F32)

    def scatter(w):
        starts, _ = _window_starts(s0_ref, b, i, nblk, cap, w)
        slots = slot_ref[...]
        row = lax.broadcasted_iota(I32, (w, 128), 0)
        hits = [row == (slots[e:e + 1] - starts[e]) for e in range(N_EXPERTS)]
        lhs = jnp.concatenate([jnp.where(hit, 1.0, 0.0).astype(BF16) for hit in hits], axis=0)
        res = jnp.dot(lhs, hn_ref[...], preferred_element_type=F32)
        for e in range(N_EXPERTS):
            cur = xs_ref[e, pl.ds(starts[e], w), :].astype(F32)
            xs_ref[e, pl.ds(starts[e], w), :] = (cur + res[e * w:(e + 1) * w]).astype(BF16)

        @pl.when(c == 0)
        def _():
            affs = aff_ref[...]
            for e in range(N_EXPERTS):
                gv = jnp.sum(jnp.where(hits[e], affs[e:e + 1], 0.0), axis=1, keepdims=True)
                g_ref[e, pl.ds(starts[e], w), :] += jnp.broadcast_to(gv, (w, 128))

    w_small, w_big = min(cap, COMPACT_WIN_SMALL), min(cap, COMPACT_WIN)
    if w_small == w_big:
        scatter(w_big)
    else:
        _, fits = _window_starts(s0_ref, b, i, nblk, cap, w_small)

        @pl.when(fits)
        def _():
            scatter(w_small)

        @pl.when(jnp.logical_not(fits))
        def _():
            scatter(w_big)


def _compact(hn, slot, aff, s0, n, cap, tok_blk0):
    nb, _, d = hn.shape
    nblk = n // 128
    dcols = d // 2 if d >= 256 else d
    grid_spec = pltpu.PrefetchScalarGridSpec(
        num_scalar_prefetch=1,
        grid=(nb, d // dcols, nblk),
        in_specs=[pl.BlockSpec((None, 128, dcols), lambda b, c, i, s: (b, tok_blk0 + i, c)),
                  pl.BlockSpec((None, N_EXPERTS, 128), lambda b, c, i, s: (b, 0, i)),
                  pl.BlockSpec((None, N_EXPERTS, 128), lambda b, c, i, s: (b, 0, tok_blk0 + i))],
        out_specs=[pl.BlockSpec((N_EXPERTS, cap, dcols), lambda b, c, i, s: (0, b, c)),
                   pl.BlockSpec((N_EXPERTS, cap, 128), lambda b, c, i, s: (0, b, 0))],
    )
    return pl.pallas_call(
        functools.partial(_compact_kernel, cap=cap),
        grid_spec=grid_spec,
        out_shape=[jax.ShapeDtypeStruct((N_EXPERTS, nb * cap, d), BF16),
                   jax.ShapeDtypeStruct((N_EXPERTS, nb * cap, 128), F32)],
        compiler_params=_cparams(("arbitrary", "arbitrary", "arbitrary")),
        name="compact",
    )(s0.reshape(-1), hn, slot, aff)


def _expert_kernel(*refs, n_in):
    xs = refs[0:n_in]
    gs = refs[n_in:2 * n_in]
    wg_ref, wu_ref, wd_ref = refs[2 * n_in:2 * n_in + 3]
    ys = refs[2 * n_in + 3:3 * n_in + 3]
    accs = refs[3 * n_in + 3:4 * n_in + 3]
    wgu_ref, wdb_ref = refs[4 * n_in + 3:]
    f = pl.program_id(1)
    last = pl.num_programs(1) - 1
    tf = wg_ref.shape[1]
    wgu_ref[:, 0:tf] = wg_ref[...].astype(BF16)
    wgu_ref[:, tf:2 * tf] = wu_ref[...].astype(BF16)
    wdb_ref[...] = wd_ref[...].astype(BF16)

    @pl.when(f == 0)
    def _():
        for acc_ref in accs:
            acc_ref[...] = jnp.zeros(acc_ref.shape, F32)

    chunks = []
    for x_ref, acc_ref in zip(xs, accs):
        m = x_ref.shape[0]
        rc = min(m, EXPERT_ROW_CHUNK)
        chunks += [(x_ref, acc_ref, r0, rc) for r0 in range(0, m, rc)]

    def up(chunk):
        x_ref, _, r0, rc = chunk
        return jnp.dot(x_ref[r0:r0 + rc, :], wgu_ref[...], preferred_element_type=F32)

    def down(chunk, au):
        _, acc_ref, r0, rc = chunk
        hid = (_silu(au[:, 0:tf]) * au[:, tf:2 * tf]).astype(BF16)
        acc_ref[r0:r0 + rc, :] += jnp.dot(hid, wdb_ref[...], preferred_element_type=F32)

    au_prev = up(chunks[0])
    for ci in range(1, len(chunks)):
        au_next = up(chunks[ci])
        down(chunks[ci - 1], au_prev)
        au_prev = au_next
    down(chunks[-1], au_prev)

    @pl.when(f == last)
    def _():
        for g_ref, y_ref, acc_ref in zip(gs, ys, accs):
            gate = g_ref[...]
            for cblk in range(acc_ref.shape[1] // 128):
                sl = slice(cblk * 128, (cblk + 1) * 128)
                y_ref[:, sl] = (acc_ref[:, sl] * gate).astype(BF16)


def _experts(xs_list, g_list, w_gate, w_up, w_down, layer):
    _, ne, d, ff = w_gate.shape
    tf = 256
    n_in = len(xs_list)
    xspec = lambda m, w: pl.BlockSpec((None, m, w), lambda e, f: (e, 0, 0))
    return pl.pallas_call(
        functools.partial(_expert_kernel, n_in=n_in),
        grid=(ne, ff // tf),
        in_specs=([xspec(x.shape[1], d) for x in xs_list] + [xspec(g.shape[1], 128) for g in g_list]
                  + [pl.BlockSpec((None, None, d, tf), lambda e, f: (layer, e, 0, f)),
                     pl.BlockSpec((None, None, d, tf), lambda e, f: (layer, e, 0, f)),
                     pl.BlockSpec((None, None, tf, d), lambda e, f: (layer, e, f, 0))]),
        out_specs=[xspec(x.shape[1], d) for x in xs_list],
        out_shape=[jax.ShapeDtypeStruct(x.shape, BF16) for x in xs_list],
        scratch_shapes=([pltpu.VMEM((x.shape[1], d), F32) for x in xs_list]
                        + [pltpu.VMEM((d, 2 * tf), BF16), pltpu.VMEM((tf, d), BF16)]),
        compiler_params=_cparams(("arbitrary", "arbitrary")),
        name="experts",
    )(*xs_list, *g_list, w_gate, w_up, w_down)


def _combine_kernel(s0_ref, y_ref, slot_ref, expand_ref, lane_ref, h_ref, mod_ref, o_ref, *,
                    cap, win, win_small, mod_row_static):
    b, i = pl.program_id(0), pl.program_id(2)
    nblk = pl.num_programs(2)
    mrow = mod_ref[b] if mod_row_static is None else mod_ref[mod_row_static]

    def combine_per_expert(w):
        starts, _ = _window_starts(s0_ref, b, i, nblk, cap, w)
        slot_t = slot_ref[...]
        jj = lax.broadcasted_iota(I32, (1, w), 1)
        acc = None
        for e in range(N_EXPERTS):
            hit = (slot_t[:, e:e + 1] - starts[e]) == jj
            part = jnp.dot(jnp.where(hit, 1.0, 0.0).astype(BF16), y_ref[e, pl.ds(starts[e], w), :],
                           preferred_element_type=F32)
            acc = part if acc is None else acc + part
        o_ref[...] = h_ref[...] + mrow[5:6] * acc

    def combine_stacked(w):
        starts, _ = _window_starts(s0_ref, b, i, nblk, cap, w)
        lane_e, lane_j = lane_ref[0:1, :], lane_ref[1:2, :]
        slot_exp = _dot01_rhs(slot_ref[...].astype(F32), expand_ref[...])
        start_exp = jnp.zeros(lane_e.shape, I32)
        for e in range(N_EXPERTS):
            start_exp = jnp.where(lane_e == e, starts[e], start_exp)
        hit = (slot_exp.astype(I32) - start_exp) == lane_j
        ywin = jnp.concatenate([y_ref[e, pl.ds(starts[e], w), :] for e in range(N_EXPERTS)], axis=0)
        acc = jnp.dot(jnp.where(hit, 1.0, 0.0).astype(BF16), ywin, preferred_element_type=F32)
        o_ref[...] = h_ref[...] + mrow[5:6] * acc

    if win_small >= win:
        combine_per_expert(win)
    else:
        _, fits = _window_starts(s0_ref, b, i, nblk, cap, win_small)

        @pl.when(fits)
        def _():
            combine_stacked(win_small)

        @pl.when(jnp.logical_not(fits))
        def _():
            combine_per_expert(win)


def _combine(y, slot, s0, h, mod, n, cap, tok_blk0, is_ctx):
    nb, t, d = h.shape
    nblk = n // 128
    dcols = d // 2 if d >= 256 else d
    win = min(cap, COMBINE_WIN)
    win_small = min(cap, COMBINE_WIN_SMALL)
    slot_t = jnp.swapaxes(slot, 1, 2)
    kcols = N_EXPERTS * win_small
    col = np.arange(kcols)
    expand = jnp.asarray(np.arange(N_EXPERTS)[:, None] == (col // win_small)[None, :], BF16)
    lane_tab = jnp.asarray(np.stack([col // win_small, col % win_small]), I32)
    full = lambda shp: pl.BlockSpec(shp, lambda b, c, i, s, n=len(shp): (0,) * n)
    grid_spec = pltpu.PrefetchScalarGridSpec(
        num_scalar_prefetch=1,
        grid=(nb, d // dcols, nblk),
        in_specs=[pl.BlockSpec((N_EXPERTS, cap, dcols), lambda b, c, i, s: (0, b, c)),
                  pl.BlockSpec((None, 128, N_EXPERTS), lambda b, c, i, s: (b, i, 0)),
                  full((N_EXPERTS, kcols)), full((2, kcols)),
                  pl.BlockSpec((None, 128, dcols), lambda b, c, i, s: (b, tok_blk0 + i, c)),
                  pl.BlockSpec((8, 6, dcols), lambda b, c, i, s: (0, 0, c))],
        out_specs=pl.BlockSpec((None, 128, dcols), lambda b, c, i, s: (b, tok_blk0 + i, c)),
    )
    return pl.pallas_call(
        functools.partial(_combine_kernel, cap=cap, win=win, win_small=win_small,
                          mod_row_static=nb if is_ctx else None),
        grid_spec=grid_spec,
        out_shape=jax.ShapeDtypeStruct((nb, t, d), F32),
        input_output_aliases={5: 0},
        compiler_params=_cparams(("arbitrary", "arbitrary", "arbitrary")),
        name="combine",
    )(s0.reshape(-1), y, slot_t, expand, lane_tab, h, mod)


def _moe(h, nw, mod, w_router, w_gate, w_up, w_down, layer, s_lat, l_ctx, with_ctx):
    hn, aff = _router(h, nw, mod, w_router, s_lat)
    streams = [(s_lat, 0, 0, False)]
    if with_ctx:
        streams.append((l_ctx, s_lat // l_ctx, s_lat // 128, True))
    sel = []
    for n, blk_idx, tok_blk0, _ in streams:
        cap = EC_CAPACITY * n // N_EXPERTS
        slot, s0 = _select(aff, n, blk_idx, cap)
        xs, gates = _compact(hn, slot, aff, s0, n, cap, tok_blk0)
        sel.append((slot, s0, xs, gates, cap))
    ys = _experts([s[2] for s in sel], [s[3] for s in sel], w_gate, w_up, w_down, layer)
    for (n, _, tok_blk0, is_ctx), (slot, s0, _, _, cap), y in zip(streams, sel, ys):
        h = _combine(y, slot, s0, h, mod, n, cap, tok_blk0, is_ctx)
    return h


def _final_norm_kernel(h_ref, w_ref, o_ref):
    x = h_ref[...]
    o_ref[...] = x * lax.rsqrt(jnp.mean(x * x, axis=-1, keepdims=True) + NORM_EPS) * w_ref[...]


def _final_norm(h, w, s_lat):
    nb, _, d = h.shape
    tm = 512
    return pl.pallas_call(
        _final_norm_kernel,
        grid=(nb, s_lat // tm),
        in_specs=[pl.BlockSpec((None, tm, d), lambda b, i: (b, i, 0)),
                  pl.BlockSpec((1, d), lambda b, i: (0, 0))],
        out_specs=pl.BlockSpec((None, tm, d), lambda b, i: (b, i, 0)),
        out_shape=jax.ShapeDtypeStruct((nb, s_lat, d), F32),
        compiler_params=_cparams(("arbitrary", "arbitrary")),
        name="final_norm",
    )(h, w)


def _reorder_w_in(w_in):
    d = w_in.shape[0]
    lr0 = COL_GDN_Q
    qkv0 = lr0 + 2 * GLA_RANK
    ab0 = qkv0 + 3 * GDN_HEADS * GDN_DK + GDN_HEADS * GDN_DV
    n_in = ab0 + 4 * GDN_HEADS
    return jnp.concatenate([w_in[:, :lr0], w_in[:, qkv0:ab0], w_in[:, lr0:qkv0], w_in[:, ab0:n_in],
                            jnp.zeros((d, NP_COLS - n_in), w_in.dtype)], axis=1).astype(BF16)


def _token_mixers(h, nw, mod, w_in, w_out, layer, rpb, gate_w, gate_b, gla_nw, conv_w, a_log, dt_bias, gdn_nw,
                  rope, s_lat, l_ctx):
    p = _inproj(h, nw, mod, _reorder_w_in(w_in), s_lat)
    o_na = _na(p, _na_bias_table(rpb), s_lat, l_ctx)
    gw_p = jnp.zeros((2, 128, GLA_HEADS * GLA_DK), F32)
    for dr in range(2):
        gw_p = gw_p.at[dr, dr * GLA_RANK:(dr + 1) * GLA_RANK].set(gate_w[dr])
    qr, kr, cf, cb = _gla_prep(p, rope[0], rope[1], gw_p, gate_b[:, None, :], s_lat)
    o_gla = _gla_scan(qr, kr, cf, cb, p, s_lat, l_ctx)
    qn, kn, vn = (_gdn_conv(p, conv_w, s_lat, part) for part in range(3))
    o_gdn = _gdn_scan(*_gdn_chunk(qn, kn, vn, p, a_log, dt_bias), s_lat, l_ctx)
    return _outproj(o_na, o_gla, o_gdn, p, gla_nw[None, :], gdn_nw[None, :], w_out, layer, h, mod, s_lat)


def kernel(x, c, ctx, c_ctx, w_ada, b_ada, norm_mix_w, norm_ffn_w, w_in, w_out, na_rpb, gla_gate_w, gla_gate_b,
           gla_norm_w, gdn_conv_w, gdn_a_log, gdn_dt_bias, gdn_norm_w, w_router, w_exp_gate, w_exp_up,
           w_exp_down, final_norm_w):
    nb, s_lat, d = x.shape
    l_ctx = ctx.shape[1]
    depth = w_ada.shape[0]
    assert nb < 8 and s_lat % 256 == 0 and l_ctx == 256 and s_lat // GRID_W >= NA_WIN_R
    h = jnp.concatenate([x, ctx], axis=1)
    cvec = jnp.zeros((8, d), F32).at[:nb].set(c).at[nb].set(c_ctx)
    mods = _ada(cvec, w_ada, b_ada).reshape(depth, 8, 6, d)
    rope = _rope_tables(s_lat, l_ctx)
    w_out = w_out.astype(BF16)
    for l in range(depth):
        h = _token_mixers(h, norm_mix_w[l][None, :], mods[l], w_in[l], w_out, l, na_rpb[l], gla_gate_w[l],
                          gla_gate_b[l], gla_norm_w[l], gdn_conv_w[l], gdn_a_log[l], gdn_dt_bias[l],
                          gdn_norm_w[l], rope, s_lat, l_ctx)
        h = _moe(h, norm_ffn_w[l][None, :], mods[l], w_router[l], w_exp_gate, w_exp_up, w_exp_down, l,
                 s_lat, l_ctx, with_ctx=l < depth - 1)
    return _final_norm(h, final_norm_w[None, :], s_lat)
```

```python
import functools

import numpy as np
import jax
import jax.numpy as jnp
from jax import lax
from jax.experimental import pallas as pl
from jax.experimental.pallas import tpu as pltpu

F32 = jnp.float32
BF16 = jnp.bfloat16
I32 = jnp.int32
HI = lax.Precision.HIGHEST

GRID_W = 64
NA_HEADS, NA_DH, NA_WIN_R, NA_WIN_C = 16, 64, 8, 16
GLA_HEADS, GLA_DK, GLA_DV, GLA_RANK, GLA_TAU = 4, 64, 128, 16, 16.0
GLA_BLK = 16
GDN_HEADS, GDN_DK, GDN_DV, GDN_CONV, GDN_CHUNK = 4, 128, 128, 5, 64
N_EXPERTS, EC_CAPACITY = 16, 2
ROPE_BASE = 10000.0
NORM_EPS = 1e-6
NEG = -1e30

NA_W = NA_HEADS * NA_DH
COL_NA_Q, COL_NA_K, COL_NA_V = 0, NA_W, 2 * NA_W
COL_GLA_Q = 3 * NA_W
COL_GLA_K = COL_GLA_Q + GLA_HEADS * GLA_DK
COL_GLA_V = COL_GLA_K + GLA_HEADS * GLA_DK
COL_GLA_G = COL_GLA_V + GLA_HEADS * GLA_DV
COL_GDN_Q = COL_GLA_G + GLA_HEADS * GLA_DV
COL_GDN_K = COL_GDN_Q + GDN_HEADS * GDN_DK
COL_GDN_V = COL_GDN_K + GDN_HEADS * GDN_DK
COL_GDN_Z = COL_GDN_V + GDN_HEADS * GDN_DV
COL_SMALL = COL_GDN_Z + GDN_HEADS * GDN_DV
SMALL_A, SMALL_B = 2 * GLA_RANK, 2 * GLA_RANK + 2 * GDN_HEADS
INPROJ_TN = 1024
NP_COLS = 7168

VMEM_LIMIT = 56 * 1024 * 1024


def _cparams(sem):
    return pltpu.CompilerParams(dimension_semantics=sem, vmem_limit_bytes=VMEM_LIMIT)


def _sigmoid(x):
    return 1.0 / (1.0 + jnp.exp(-x))


def _silu(x):
    return x * _sigmoid(x)


def _softplus(x):
    return jnp.maximum(x, 0.0) + jnp.log(1.0 + jnp.exp(-jnp.abs(x)))


def _bdot(a, b):
    return jnp.dot(a.astype(BF16), b.astype(BF16), preferred_element_type=F32)


def _dot_nt(a, b):
    return lax.dot_general(a, b, (((1,), (1,)), ((), ())), preferred_element_type=F32)


def _dot_tn(a, b):
    return lax.dot_general(a, b, (((0,), (0,)), ((), ())), preferred_element_type=F32)


def _dot01_rhs(x, m01):
    hi = x.astype(BF16)
    r1 = x - hi.astype(F32)
    mid = r1.astype(BF16)
    lo = (r1 - mid.astype(F32)).astype(BF16)
    d = lambda p: jnp.dot(p, m01, preferred_element_type=F32)
    return d(hi) + d(mid) + d(lo)


def _dot01(m01, x):
    hi = x.astype(BF16)
    r1 = x - hi.astype(F32)
    mid = r1.astype(BF16)
    lo = (r1 - mid.astype(F32)).astype(BF16)
    d = lambda p: jnp.dot(m01, p, preferred_element_type=F32)
    return d(hi) + d(mid) + d(lo)


def _ada_kernel(c_ref, w_ref, b_ref, o_ref):
    o_ref[...] = _bdot(_silu(c_ref[...]), w_ref[...]) + b_ref[...]


def _ada(cvec, w_ada, b_ada):
    depth, d, n6 = w_ada.shape
    tn = 2048 if n6 % 2048 == 0 else 512
    assert n6 % tn == 0
    return pl.pallas_call(
        _ada_kernel,
        grid=(depth, n6 // tn),
        in_specs=[pl.BlockSpec((8, d), lambda l, j: (0, 0)),
                  pl.BlockSpec((None, d, tn), lambda l, j: (l, 0, j)),
                  pl.BlockSpec((None, 1, tn), lambda l, j: (l, 0, j))],
        out_specs=pl.BlockSpec((None, 8, tn), lambda l, j: (l, 0, j)),
        out_shape=jax.ShapeDtypeStruct((depth, 8, n6), F32),
        compiler_params=_cparams(("arbitrary", "arbitrary")),
        name="ada",
    )(cvec, w_ada, b_ada.reshape(depth, 1, n6))


def _mod_norm(x, nw, mod_b, mod_c, row0, s_lat, k_shift):
    ms = jnp.mean(x * x, axis=-1, keepdims=True)
    y = x * lax.rsqrt(ms + NORM_EPS) * nw
    row = row0 + lax.broadcasted_iota(I32, (x.shape[0], 1), 0)
    is_lat = row < s_lat
    shift = jnp.where(is_lat, mod_b[k_shift:k_shift + 1], mod_c[k_shift:k_shift + 1])
    scale = jnp.where(is_lat, mod_b[k_shift + 1:k_shift + 2], mod_c[k_shift + 1:k_shift + 2])
    return y * (1.0 + scale) + shift


def _inproj_kernel(x_ref, nw_ref, mod_ref, w_ref, o_ref, xn_ref, *, s_lat, tm, nb):
    b, i, j = pl.program_id(0), pl.program_id(1), pl.program_id(2)

    @pl.when(j == 0)
    def _():
        rc = tm // 4

        def chunk(ci, carry):
            r = pl.multiple_of(ci * rc, 16)
            xn = _mod_norm(x_ref[pl.ds(r, rc), :], nw_ref[...], mod_ref[b], mod_ref[nb], i * tm + r, s_lat, 0)
            xn_ref[pl.ds(r, rc), :] = xn.astype(BF16)
            return carry

        lax.fori_loop(0, 4, chunk, 0)

    o_ref[...] = jnp.dot(xn_ref[...], w_ref[...].astype(BF16), preferred_element_type=F32)


def _inproj(h, nw, mod, w_p, s_lat):
    nb, t, d = h.shape
    tm = t // 4
    tn = INPROJ_TN
    npc = w_p.shape[1]
    return pl.pallas_call(
        functools.partial(_inproj_kernel, s_lat=s_lat, tm=tm, nb=nb),
        grid=(nb, t // tm, npc // tn),
        in_specs=[pl.BlockSpec((None, tm, d), lambda b, i, j: (b, i, 0)),
                  pl.BlockSpec((1, d), lambda b, i, j: (0, 0)),
                  pl.BlockSpec((8, 6, d), lambda b, i, j: (0, 0, 0)),
                  pl.BlockSpec((d, tn), lambda b, i, j: (0, j))],
        out_specs=pl.BlockSpec((None, tm, tn), lambda b, i, j: (b, i, j)),
        out_shape=jax.ShapeDtypeStruct((nb, t, npc), F32),
        scratch_shapes=[pltpu.VMEM((tm, d), BF16)],
        compiler_params=_cparams(("arbitrary", "arbitrary", "arbitrary")),
        name="inproj",
    )(h, nw, mod, w_p)


NA_NDR = 2 * NA_WIN_R - 1
NA_NDC = 2 * NA_WIN_C - 1


def _na_bias_kernel(rpb_ref, onehot_ref, mask_ref, o_ref):
    t = jnp.dot(rpb_ref[...], onehot_ref[...], precision=HI, preferred_element_type=F32)
    o_ref[...] = jnp.where(mask_ref[...] > 0.0, t, NEG)


def _na_bias_table(rpb):
    nh = rpb.shape[0]
    cq = np.arange(GRID_W)
    dc = np.clip(cq[None, :] - cq[:, None] + NA_WIN_C - 1, 0, NA_NDC - 1)
    cstart = np.clip(cq - NA_WIN_C // 2, 0, GRID_W - NA_WIN_C)
    colmask = (cq[None, :] >= cstart[:, None]) & (cq[None, :] < cstart[:, None] + NA_WIN_C)
    onehot = jnp.asarray(np.arange(128)[:, None] == dc.reshape(1, -1), F32)
    mask = jnp.asarray(colmask.reshape(1, -1), F32)
    rpb_p = jnp.zeros((nh, 16, 128), F32).at[:, :NA_NDR, :NA_NDC].set(rpb)
    full = pl.pallas_call(
        _na_bias_kernel,
        grid=(nh,),
        in_specs=[pl.BlockSpec((None, 16, 128), lambda h: (h, 0, 0)),
                  pl.BlockSpec((128, GRID_W * GRID_W), lambda h: (0, 0)),
                  pl.BlockSpec((1, GRID_W * GRID_W), lambda h: (0, 0))],
        out_specs=pl.BlockSpec((None, 16, GRID_W * GRID_W), lambda h: (h, 0, 0)),
        out_shape=jax.ShapeDtypeStruct((nh, 16, GRID_W * GRID_W), F32),
        compiler_params=_cparams(("arbitrary",)),
        name="na_bias",
    )(rpb_p, onehot, mask)
    t15 = full[:, :NA_NDR].reshape(nh, NA_NDR, GRID_W, GRID_W)
    return jnp.concatenate([t15[:, :-1], t15[:, 1:]], axis=-1)


def _softmax_pv_many(s_lists, v_lists):
    n = len(s_lists)
    ms = []
    for sl in s_lists:
        m = sl[0].max(axis=-1, keepdims=True)
        for s in sl[1:]:
            m = jnp.maximum(m, s.max(axis=-1, keepdims=True))
        ms.append(m)
    ps = [[jnp.exp(s - ms[c]) for s in s_lists[c]] for c in range(n)]
    dens = []
    for c in range(n):
        den = ps[c][0].sum(axis=-1, keepdims=True)
        for p in ps[c][1:]:
            den = den + p.sum(axis=-1, keepdims=True)
        dens.append(den)
    outs = []
    for c in range(n):
        o = jnp.dot(ps[c][0].astype(BF16), v_lists[c][0], preferred_element_type=F32)
        for p, v in zip(ps[c][1:], v_lists[c][1:]):
            o = o + jnp.dot(p.astype(BF16), v, preferred_element_type=F32)
        outs.append(o)
    return [o / den for o, den in zip(outs, dens)]


NA_ROWS_PER_STEP = 8


def _na_kernel(q_ref, k_ref, v_ref, bias_ref, o_ref, kb_ref, vb_ref, *, s_lat, l_ctx):
    rows = s_lat // GRID_W
    nwin = NA_WIN_R * GRID_W
    scale = NA_DH ** -0.5
    kb_ref[...] = k_ref[...].astype(BF16)
    vb_ref[...] = v_ref[...].astype(BF16)
    lane = lax.broadcasted_iota(I32, (1, 2 * NA_DH), 1)
    head_lanes = (lane < NA_DH, lane >= NA_DH)
    kc = kb_ref[pl.ds(s_lat, l_ctx), :]
    vc = vb_ref[pl.ds(s_lat, l_ctx), :]

    qc = q_ref[pl.ds(s_lat, l_ctx), :] * scale
    qhs = [jnp.where(head_lanes[hh], qc, 0.0).astype(BF16) for hh in range(2)]
    outs = _softmax_pv_many([[_dot_nt(qh, kc)] for qh in qhs], [[vc], [vc]])
    o_ref[pl.ds(s_lat, l_ctx), :] = jnp.where(head_lanes[0], outs[0], outs[1])

    def body(it, carry):
        s_lists, v_lists, starts = [], [], []
        for rr in range(NA_ROWS_PER_STEP):
            r = it * NA_ROWS_PER_STEP + rr
            r0 = jnp.clip(r - NA_WIN_R // 2, 0, rows - NA_WIN_R)
            d = r - r0
            qs = pl.multiple_of(r * GRID_W, GRID_W)
            ks = pl.multiple_of(r0 * GRID_W, GRID_W)
            starts.append(qs)
            q = q_ref[pl.ds(qs, GRID_W), :] * scale
            kw = kb_ref[pl.ds(ks, nwin), :]
            vw = vb_ref[pl.ds(ks, nwin), :]
            q2 = jnp.concatenate([jnp.where(head_lanes[hh], q, 0.0) for hh in range(2)], axis=0).astype(BF16)
            bias = jnp.concatenate(
                [jnp.concatenate([bias_ref[hh, 2 * m + NA_WIN_R - 1 - d] for hh in range(2)], axis=0)
                 for m in range(NA_WIN_R // 2)], axis=1)
            s_lists.append([_dot_nt(q2, kw) + bias, _dot_nt(q2, kc)])
            v_lists.append([vw, vc])
        res = _softmax_pv_many(s_lists, v_lists)
        for rr in range(NA_ROWS_PER_STEP):
            o_ref[pl.ds(starts[rr], GRID_W), :] = jnp.where(head_lanes[0], res[rr][0:GRID_W], res[rr][GRID_W:2 * GRID_W])
        return carry

    lax.fori_loop(0, rows // NA_ROWS_PER_STEP, body, 0)


def _na(p, bias_tbl, s_lat, l_ctx):
    nb, t, _ = p.shape
    blk = lambda off: pl.BlockSpec((None, t, 128), lambda b, g, off=off: (b, 0, off // 128 + g))
    return pl.pallas_call(
        functools.partial(_na_kernel, s_lat=s_lat, l_ctx=l_ctx),
        grid=(nb, NA_HEADS // 2),
        in_specs=[blk(COL_NA_Q), blk(COL_NA_K), blk(COL_NA_V),
                  pl.BlockSpec((2, NA_NDR - 1, GRID_W, 2 * GRID_W), lambda b, g: (g, 0, 0, 0))],
        out_specs=pl.BlockSpec((None, t, 128), lambda b, g: (b, 0, g)),
        out_shape=jax.ShapeDtypeStruct((nb, t, NA_W), F32),
        scratch_shapes=[pltpu.VMEM((t, 128), BF16), pltpu.VMEM((t, 128), BF16)],
        compiler_params=_cparams(("arbitrary", "arbitrary")),
        name="na",
    )(p, p, p, bias_tbl)


def _rope_tables(s_lat, l_ctx):
    width = GLA_HEADS * GLA_DK
    nf = GLA_DK // 4
    pos = np.arange(s_lat)
    lane = np.arange(width)
    sub = lane % GLA_DK
    freqs = ROPE_BASE ** (-np.arange(nf, dtype=np.float64) / nf)
    p_sel = np.where((sub < GLA_DK // 2)[None, :], (pos // GRID_W)[:, None], (pos % GRID_W)[:, None])
    ang = p_sel.astype(np.float64) * freqs[sub % nf][None, :]
    sign = np.where((sub % (2 * nf)) < nf, -1.0, 1.0)
    cos = np.concatenate([np.cos(ang), np.ones((l_ctx, width))], axis=0)
    sin = np.concatenate([np.sin(ang) * sign[None, :], np.zeros((l_ctx, width))], axis=0)
    return jnp.asarray(cos, F32), jnp.asarray(sin, F32)


def _block_tri(n, blk):
    i = np.arange(n)
    same = (i[:, None] // blk) == (i[None, :] // blk)
    lower = same & (i[None, :] <= i[:, None])
    upper = same & (i[None, :] >= i[:, None])
    return jnp.asarray(lower, BF16), jnp.asarray(upper, BF16)


def _gla_prep_kernel(q_ref, k_ref, sm_ref, cos_ref, sin_ref, gw_ref, gb_ref, tl_ref, tu_ref,
                     qr_ref, kr_ref, cf_ref, cb_ref):
    width = GLA_HEADS * GLA_DK
    nf = GLA_DK // 4
    lane = lax.broadcasted_iota(I32, (1, width), 1)
    first = (lane % (2 * nf)) < nf
    cos, sin = cos_ref[...], sin_ref[...]

    def rope(x):
        swapped = jnp.where(first, pltpu.roll(x, width - nf, 1), pltpu.roll(x, nf, 1))
        return x * cos + swapped * sin

    qr_ref[...] = rope(q_ref[...]) * (GLA_DK ** -0.5)
    kr_ref[...] = rope(k_ref[...])
    sm = sm_ref[...]
    for dr, (tri_ref, out_ref) in enumerate(((tl_ref, cf_ref), (tu_ref, cb_ref))):
        z = jnp.dot(sm, gw_ref[dr], precision=HI, preferred_element_type=F32) + gb_ref[dr]
        log_a = (jnp.minimum(z, 0.0) - jnp.log(1.0 + jnp.exp(-jnp.abs(z)))) * (1.0 / GLA_TAU)
        out_ref[...] = _dot01(tri_ref[...], log_a)


def _gla_prep(p, cos, sin, gw_p, gb, s_lat):
    nb, t, _ = p.shape
    width = GLA_HEADS * GLA_DK
    tr = 256
    tl, tu = _block_tri(tr, GLA_BLK)
    row_blk = lambda w, off: pl.BlockSpec((None, tr, w), lambda b, i, off=off, w=w: (b, i, off // w))
    tab = pl.BlockSpec((tr, width), lambda b, i: (i, 0))
    full = lambda shp: pl.BlockSpec(shp, lambda b, i, n=len(shp): (0,) * n)
    out = pl.BlockSpec((None, tr, width), lambda b, i: (b, i, 0))
    return pl.pallas_call(
        _gla_prep_kernel,
        grid=(nb, t // tr),
        in_specs=[row_blk(width, COL_GLA_Q), row_blk(width, COL_GLA_K), row_blk(128, COL_SMALL),
                  tab, tab, full((2, 128, width)), full((2, 1, width)), full((tr, tr)), full((tr, tr))],
        out_specs=[out] * 4,
        out_shape=[jax.ShapeDtypeStruct((nb, t, width), F32)] * 4,
        compiler_params=_cparams(("arbitrary", "arbitrary")),
        name="gla_prep",
    )(p, p, p, cos, sin, gw_p, gb, tl, tu)


def _gla_scan_kernel(qr_ref, kr_ref, cf_ref, cb_ref, v_ref, r3_ref, o_ref, st_ref, *, s_lat, l_ctx):
    nlat, nctx = s_lat // GLA_BLK, l_ctx // GLA_BLK
    o_ref[...] = jnp.zeros(o_ref.shape, F32)
    st_ref[...] = jnp.zeros(st_ref.shape, F32)
    sub = lax.broadcasted_iota(I32, (GLA_BLK, 2 * GLA_DK), 0)
    bd = (lax.broadcasted_iota(I32, (2 * GLA_DV, 2 * GLA_DK), 0) // GLA_DV
          == lax.broadcasted_iota(I32, (2 * GLA_DV, 2 * GLA_DK), 1) // GLA_DK)

    def step(i, carry):
        in_ctx = i < nctx
        jf = jnp.where(in_ctx, nlat + i, i - nctx)
        jb = jnp.where(in_ctx, nlat + nctx - 1 - i, nlat - 1 - (i - nctx))
        dirs = (0, 1)
        rs = [pl.multiple_of(j * GLA_BLK, GLA_BLK) for j in (jf, jb)]
        qs = [qr_ref[pl.ds(r, GLA_BLK), :] for r in rs]
        ks = [kr_ref[pl.ds(r, GLA_BLK), :] for r in rs]
        cums = [c_ref[pl.ds(r, GLA_BLK), :] for c_ref, r in zip((cf_ref, cb_ref), rs)]
        vs = [v_ref[pl.ds(r, GLA_BLK), :] for r in rs]
        tots = [cums[0][GLA_BLK - 1:GLA_BLK], cums[1][0:1]]
        sts = [st_ref[dr] for dr in dirs]
        qds = [(qs[dr] * jnp.exp(cums[dr])).astype(BF16) for dr in dirs]
        kds = [(ks[dr] * jnp.exp(tots[dr] - cums[dr])).astype(BF16) for dr in dirs]
        o_states = [_dot_nt(qds[dr], sts[dr].astype(BF16)) for dr in dirs]
        upds = [_dot_tn(vs[dr].astype(BF16), kds[dr]) for dr in dirs]
        xs = []
        for dr in dirs:
            tiles = []
            for s in range(GLA_BLK):
                valid = (sub >= s) if dr == 0 else (sub <= s)
                w = jnp.exp(jnp.where(valid, cums[dr] - cums[dr][s:s + 1], NEG))
                tiles.append((w * qs[dr] * ks[dr][s:s + 1]).astype(BF16))
            xs.append(jnp.concatenate(tiles, axis=0))
        ress = [jnp.dot(x, r3_ref[...], preferred_element_type=F32) for x in xs]
        for dr in dirs:
            st_ref[dr] = sts[dr] * jnp.exp(tots[dr]) + jnp.where(bd, upds[dr], 0.0)
        for dr in dirs:
            o_diag = ress[dr][0:GLA_BLK] * vs[dr][0:1]
            for s in range(1, GLA_BLK):
                o_diag = o_diag + ress[dr][s * GLA_BLK:(s + 1) * GLA_BLK] * vs[dr][s:s + 1]
            o_ref[pl.ds(rs[dr], GLA_BLK), :] += o_states[dr] + o_diag
        return carry

    lax.fori_loop(0, nlat + nctx, step, 0, unroll=8)


def _gla_scan(qr, kr, cf, cb, p, s_lat, l_ctx):
    nb, t, _ = p.shape
    lanes = 2 * GLA_DK
    r3 = jnp.asarray((np.arange(lanes)[:, None] // GLA_DK) == (np.arange(2 * GLA_DV)[None, :] // GLA_DV), BF16)
    blk = pl.BlockSpec((None, t, lanes), lambda b, g: (b, 0, g))
    return pl.pallas_call(
        functools.partial(_gla_scan_kernel, s_lat=s_lat, l_ctx=l_ctx),
        grid=(nb, GLA_HEADS // 2),
        in_specs=[blk, blk, blk, blk,
                  pl.BlockSpec((None, t, 2 * GLA_DV), lambda b, g: (b, 0, COL_GLA_V // (2 * GLA_DV) + g)),
                  pl.BlockSpec((lanes, 2 * GLA_DV), lambda b, g: (0, 0))],
        out_specs=pl.BlockSpec((None, t, 2 * GLA_DV), lambda b, g: (b, 0, g)),
        out_shape=jax.ShapeDtypeStruct((nb, t, GLA_HEADS * GLA_DV), F32),
        scratch_shapes=[pltpu.VMEM((2, 2 * GLA_DV, lanes), F32)],
        compiler_params=_cparams(("arbitrary", "arbitrary")),
        name="gla_scan",
    )(qr, kr, cf, cb, p, r3)


def _gdn_conv_kernel(x_ref, w_ref, o_ref, *, s_lat, normalize, scale):
    t = x_ref.shape[0]
    x = x_ref[...]
    tpos = lax.broadcasted_iota(I32, (t, 1), 0)
    half = GDN_CONV // 2
    acc = x * w_ref[half:half + 1, :]
    for j in range(GDN_CONV):
        if j == half:
            continue
        dlt = j - half
        src = tpos + dlt
        ok = (src >= 0) & (src < t) & ((src < s_lat) == (tpos < s_lat))
        acc = acc + jnp.where(ok, pltpu.roll(x, (-dlt) % t, 0), 0.0) * w_ref[j:j + 1, :]
    y = _silu(acc)
    if normalize:
        y = y * lax.rsqrt(jnp.sum(y * y, axis=-1, keepdims=True) + 1e-6) * scale
    o_ref[...] = y


def _gdn_conv(p, conv_w, s_lat, part):
    nb, t, _ = p.shape
    col = (COL_GDN_Q, COL_GDN_K, COL_GDN_V)[part]
    return pl.pallas_call(
        functools.partial(_gdn_conv_kernel, s_lat=s_lat, normalize=part < 2,
                          scale=GDN_DK ** -0.5 if part == 0 else 1.0),
        grid=(nb, GDN_HEADS),
        in_specs=[pl.BlockSpec((None, t, 128), lambda b, g: (b, 0, col // 128 + g)),
                  pl.BlockSpec((GDN_CONV, 128), lambda b, g: (0, part * GDN_HEADS + g))],
        out_specs=pl.BlockSpec((None, t, 128), lambda b, g: (b, 0, g)),
        out_shape=jax.ShapeDtypeStruct((nb, t, GDN_HEADS * 128), F32),
        compiler_params=_cparams(("arbitrary", "arbitrary")),
        name=f"gdn_conv{part}",
    )(p, conv_w)


GDN_HEADS_PER_STEP = 4


def _gdn_chunk_kernel(q_ref, k_ref, v_ref, sm_ref, alog_ref, dtb_ref,
                      u_ref, w_ref, qd_ref, kd_ref, a_ref, gl_ref, *, tc):
    g = pl.program_id(1)
    cs = GDN_CHUNK
    nh = GDN_HEADS_PER_STEP
    lane = lax.broadcasted_iota(I32, (1, 128), 1)
    subl = lax.broadcasted_iota(I32, (128, 1), 0)
    ti = lax.broadcasted_iota(I32, (tc, tc), 0)
    ui = lax.broadcasted_iota(I32, (tc, tc), 1)
    same = (ti // cs) == (ui // cs)
    eye = jnp.where(ti == ui, 1.0, 0.0)
    sm = sm_ref[...]
    sm_t = sm.T
    qs = [q_ref[:, hh * 128:(hh + 1) * 128] for hh in range(nh)]
    ks = [k_ref[:, hh * 128:(hh + 1) * 128] for hh in range(nh)]
    vs = [v_ref[:, hh * 128:(hh + 1) * 128] for hh in range(nh)]
    kbs = [k.astype(BF16) for k in ks]
    kks = [_dot_nt(kb, kb) for kb in kbs]
    qks = [_dot_nt(q.astype(BF16), kb) for q, kb in zip(qs, kbs)]
    chains = [(hh, dr) for hh in range(nh) for dr in range(2)]
    ms, decays, betas, e_gcs = [], [], [], []
    for hh, dr in chains:
        head = g * nh + hh
        ca = SMALL_A + GDN_HEADS * dr + head
        cbeta = SMALL_B + GDN_HEADS * dr + head
        a_col = jnp.sum(jnp.where(lane == ca, sm, 0.0), axis=1, keepdims=True)
        b_col = jnp.sum(jnp.where(lane == cbeta, sm, 0.0), axis=1, keepdims=True)
        a_row = jnp.sum(jnp.where(subl == ca, sm_t, 0.0), axis=0, keepdims=True)
        neg_rate = -jnp.exp(alog_ref[dr, hh])
        g_col = neg_rate * _softplus(a_col + dtb_ref[dr, hh])
        g_row = neg_rate * _softplus(a_row + dtb_ref[dr, hh])
        beta = _sigmoid(b_col)
        incl = same & ((ui <= ti) if dr == 0 else (ui >= ti))
        strict = same & ((ui < ti) if dr == 0 else (ui > ti))
        incl_t = same & ((ti <= ui) if dr == 0 else (ti >= ui))
        gc_col = jnp.sum(jnp.where(incl, g_row, 0.0), axis=1, keepdims=True)
        gc_row = jnp.sum(jnp.where(incl_t, g_col, 0.0), axis=0, keepdims=True)
        gc_tot = jnp.sum(jnp.where(same, g_row, 0.0), axis=1, keepdims=True)
        decay = jnp.exp(jnp.where(incl, gc_col - gc_row, NEG))
        ms.append(jnp.where(strict, beta * kks[hh] * decay, 0.0))
        decays.append(decay)
        betas.append(beta)
        e_gcs.append(jnp.exp(gc_col))
        kd_ref[hh, dr] = (ks[hh] * jnp.exp(gc_tot - gc_col)).astype(BF16)
        for c in range(tc // cs):
            gl_ref[hh, dr, c * 8:(c + 1) * 8, :] = jnp.broadcast_to(jnp.exp(gc_tot[c * cs:c * cs + 1]), (8, 128))
    invs = [eye - m for m in ms]
    mks = [_bdot(m, m) for m in ms]
    for lvl in range(5):
        invs = [inv + _bdot(inv, mk) for inv, mk in zip(invs, mks)]
        if lvl < 4:
            mks = [_bdot(mk, mk) for mk in mks]
    sols = [_bdot(invs[ci], jnp.concatenate([vs[hh] * betas[ci], ks[hh] * (betas[ci] * e_gcs[ci])], axis=1))
            for ci, (hh, dr) in enumerate(chains)]
    for ci, (hh, dr) in enumerate(chains):
        u_ref[hh, dr] = sols[ci][:, :GDN_DV]
        w_ref[hh, dr] = sols[ci][:, GDN_DV:].astype(BF16)
        qd_ref[hh, dr] = (qs[hh] * e_gcs[ci]).astype(BF16)
        aqk = qks[hh] * decays[ci]
        for c in range(tc // cs):
            blk = aqk[c * cs:(c + 1) * cs]
            folded = blk[:, 0:128]
            for piece in range(1, tc // 128):
                folded = folded + blk[:, piece * 128:(piece + 1) * 128]
            a_ref[hh, dr, c * cs:(c + 1) * cs, :] = folded.astype(BF16)


def _gdn_chunk(qn, kn, vn, p, a_log, dt_bias):
    nb, t, _ = p.shape
    tc = 256
    nh = GDN_HEADS_PER_STEP
    nch = t // GDN_CHUNK
    rb = lambda: pl.BlockSpec((None, tc, nh * 128), lambda b, g, i: (b, i, g))
    par = pl.BlockSpec((2, nh, 1, 1), lambda b, g, i: (0, g, 0, 0))
    ob = lambda w: pl.BlockSpec((None, nh, 2, tc, w), lambda b, g, i: (b, g, 0, i, 0))
    shp = lambda w, dt: jax.ShapeDtypeStruct((nb, GDN_HEADS, 2, t, w), dt)
    return pl.pallas_call(
        functools.partial(_gdn_chunk_kernel, tc=tc),
        grid=(nb, GDN_HEADS // nh, t // tc),
        in_specs=[rb(), rb(), rb(),
                  pl.BlockSpec((None, tc, 128), lambda b, g, i: (b, i, COL_SMALL // 128)), par, par],
        out_specs=[ob(128), ob(128), ob(128), ob(128), ob(128),
                   pl.BlockSpec((None, nh, 2, (tc // GDN_CHUNK) * 8, 128), lambda b, g, i: (b, g, 0, i, 0))],
        out_shape=[shp(128, F32), shp(128, BF16), shp(128, BF16), shp(128, BF16), shp(128, BF16),
                   jax.ShapeDtypeStruct((nb, GDN_HEADS, 2, nch * 8, 128), F32)],
        compiler_params=_cparams(("arbitrary", "arbitrary", "arbitrary")),
        name="gdn_chunk",
    )(qn, kn, vn, p, a_log.reshape(2, GDN_HEADS, 1, 1).astype(F32), dt_bias.reshape(2, GDN_HEADS, 1, 1).astype(F32))


GDN_SCAN_ROWS = 256


def _gdn_scan_kernel(*refs):
    fwd, bwd = refs[0:6], refs[6:12]
    o_refs = refs[12:14]
    st_ref = refs[14]
    cs = GDN_CHUNK
    ncg = GDN_SCAN_ROWS // cs
    chains = [(dr, hh) for dr in range(2) for hh in range(GDN_HEADS)]

    @pl.when(pl.program_id(1) == 0)
    def _():
        st_ref[...] = jnp.zeros(st_ref.shape, F32)

    for ci in range(ncg):
        rows = [slice(ci * cs, (ci + 1) * cs), slice((ncg - 1 - ci) * cs, (ncg - ci) * cs)]
        gls = [slice(ci * 8, ci * 8 + 1), slice((ncg - 1 - ci) * 8, (ncg - 1 - ci) * 8 + 1)]

        def piece(k, dr, hh):
            return (fwd, bwd)[dr][k][hh, rows[dr], :]

        sts = [st_ref[dr, hh] for dr, hh in chains]
        stbs = [st.astype(BF16) for st in sts]
        wss = [jnp.dot(piece(1, dr, hh), stb, preferred_element_type=F32) for (dr, hh), stb in zip(chains, stbs)]
        qss = [jnp.dot(piece(2, dr, hh), stb, preferred_element_type=F32) for (dr, hh), stb in zip(chains, stbs)]
        vnbs = [(piece(0, dr, hh) - ws).astype(BF16) for (dr, hh), ws in zip(chains, wss)]
        avs = [jnp.dot(piece(4, dr, hh), jnp.concatenate([vnb, vnb], axis=0), preferred_element_type=F32)
               for (dr, hh), vnb in zip(chains, vnbs)]
        upds = [_dot_tn(piece(3, dr, hh), vnb) for (dr, hh), vnb in zip(chains, vnbs)]
        for (dr, hh), st, upd in zip(chains, sts, upds):
            st_ref[dr, hh] = st * (fwd, bwd)[dr][5][hh, gls[dr], :] + upd
        for (dr, hh), qs, av in zip(chains, qss, avs):
            o_refs[dr][rows[dr], hh * GDN_DV:(hh + 1) * GDN_DV] = qs + av


def _gdn_scan(u, w, qd, kd, a, gl, s_lat, l_ctx):
    nb, nh, _, t, _ = u.shape
    tg = GDN_SCAN_ROWS
    assert l_ctx == tg and s_lat % tg == 0
    ngl = s_lat // tg
    grp = (lambda i: jnp.where(i == 0, ngl, i - 1),
           lambda i: jnp.where(i == 0, ngl, ngl - i))
    rows8 = (tg // GDN_CHUNK) * 8
    in_specs = []
    for dr in range(2):
        for wd in (128,) * 5:
            in_specs.append(pl.BlockSpec((None, nh, None, tg, wd), lambda b, i, dr=dr: (b, 0, dr, grp[dr](i), 0)))
        in_specs.append(pl.BlockSpec((None, nh, None, rows8, 128), lambda b, i, dr=dr: (b, 0, dr, grp[dr](i), 0)))
    return pl.pallas_call(
        _gdn_scan_kernel,
        grid=(nb, ngl + 1),
        in_specs=in_specs,
        out_specs=[pl.BlockSpec((None, tg, nh * GDN_DV), lambda b, i, dr=dr: (b, grp[dr](i), 0)) for dr in range(2)],
        out_shape=[jax.ShapeDtypeStruct((nb, t, nh * GDN_DV), F32)] * 2,
        scratch_shapes=[pltpu.VMEM((2, nh, GDN_DK, GDN_DV), F32)],
        compiler_params=_cparams(("arbitrary", "arbitrary")),
        name="gdn_scan",
    )(u, w, qd, kd, a, gl, u, w, qd, kd, a, gl)


def _head_norm_gate(o, gate, nw):
    y = o * lax.rsqrt(jnp.mean(o * o, axis=-1, keepdims=True) + NORM_EPS) * nw
    return y * _silu(gate)


def _outproj_kernel(na_ref, gla_ref, gg_ref, gdnf_ref, gdnb_ref, gz_ref, gnw_ref, dnw_ref, w_ref, h_ref, mod_ref,
                    o_ref, lhs_ref, *, s_lat, tm, nb):
    b, i, j = pl.program_id(0), pl.program_id(1), pl.program_id(2)

    @pl.when(j == 0)
    def _():
        lhs_ref[:, 0:NA_W] = na_ref[...].astype(BF16)
        for hh in range(GLA_HEADS):
            sl = slice(hh * GLA_DV, (hh + 1) * GLA_DV)
            y = _head_norm_gate(gla_ref[:, sl], gg_ref[:, sl], gnw_ref[...])
            lhs_ref[:, NA_W + hh * GLA_DV:NA_W + (hh + 1) * GLA_DV] = y.astype(BF16)
        base = NA_W + GLA_HEADS * GLA_DV
        for hh in range(GDN_HEADS):
            sl = slice(hh * GDN_DV, (hh + 1) * GDN_DV)
            y = _head_norm_gate(gdnf_ref[:, sl] + gdnb_ref[:, sl], gz_ref[:, sl], dnw_ref[...])
            lhs_ref[:, base + hh * GDN_DV:base + (hh + 1) * GDN_DV] = y.astype(BF16)

    mix = jnp.dot(lhs_ref[...], w_ref[...].astype(BF16), preferred_element_type=F32)
    row = i * tm + lax.broadcasted_iota(I32, (tm, 1), 0)
    gate = jnp.where(row < s_lat, mod_ref[b][2:3], mod_ref[nb][2:3])
    o_ref[...] = h_ref[...] + gate * mix


def _outproj(o_na, o_gla, o_gdn, p, gla_nw, gdn_nw, w_out, layer, h, mod, s_lat):
    nb, t, d = h.shape
    tm = t // 8
    tn = min(1024, d)
    mix_w = w_out.shape[1]
    gw = GLA_HEADS * GLA_DV
    rb = lambda w, c=0: pl.BlockSpec((None, tm, w), lambda b, i, j, c=c, w=w: (b, i, c // w))
    return pl.pallas_call(
        functools.partial(_outproj_kernel, s_lat=s_lat, tm=tm, nb=nb),
        grid=(nb, t // tm, d // tn),
        in_specs=[rb(NA_W), rb(gw), rb(gw, COL_GLA_G), rb(gw), rb(gw), rb(gw, COL_GDN_Z),
                  pl.BlockSpec((1, GLA_DV), lambda b, i, j: (0, 0)),
                  pl.BlockSpec((1, GDN_DV), lambda b, i, j: (0, 0)),
                  pl.BlockSpec((None, mix_w, tn), lambda b, i, j: (layer, 0, j)),
                  pl.BlockSpec((None, tm, tn), lambda b, i, j: (b, i, j)),
                  pl.BlockSpec((8, 6, tn), lambda b, i, j: (0, 0, j))],
        out_specs=pl.BlockSpec((None, tm, tn), lambda b, i, j: (b, i, j)),
        out_shape=jax.ShapeDtypeStruct((nb, t, d), F32),
        scratch_shapes=[pltpu.VMEM((tm, mix_w), BF16)],
        compiler_params=_cparams(("arbitrary", "arbitrary", "arbitrary")),
        name="outproj",
    )(o_na, o_gla, p, o_gdn[0], o_gdn[1], p, gla_nw, gdn_nw, w_out, h, mod)


def _router_kernel(h_ref, nw_ref, mod_ref, wr_ref, hn_ref, aff_ref, *, s_lat, tm, nb):
    b, i = pl.program_id(0), pl.program_id(1)
    hn = _mod_norm(h_ref[...], nw_ref[...], mod_ref[b], mod_ref[nb], i * tm, s_lat, 3)
    hn_ref[...] = hn.astype(BF16)
    logits = jnp.dot(hn, wr_ref[...], precision=HI, preferred_element_type=F32).T[0:N_EXPERTS]
    e = jnp.exp(logits - logits.max(axis=0, keepdims=True))
    aff_ref[...] = e / e.sum(axis=0, keepdims=True)


def _router(h, nw, mod, w_router, s_lat):
    nb, t, d = h.shape
    tm = 256
    w_pad = jnp.zeros((d, 128), F32).at[:, :N_EXPERTS].set(w_router)
    return pl.pallas_call(
        functools.partial(_router_kernel, s_lat=s_lat, tm=tm, nb=nb),
        grid=(nb, t // tm),
        in_specs=[pl.BlockSpec((None, tm, d), lambda b, i: (b, i, 0)),
                  pl.BlockSpec((1, d), lambda b, i: (0, 0)),
                  pl.BlockSpec((8, 6, d), lambda b, i: (0, 0, 0)),
                  pl.BlockSpec((d, 128), lambda b, i: (0, 0))],
        out_specs=[pl.BlockSpec((None, tm, d), lambda b, i: (b, i, 0)),
                   pl.BlockSpec((None, N_EXPERTS, tm), lambda b, i: (b, 0, i))],
        out_shape=[jax.ShapeDtypeStruct((nb, t, d), BF16),
                   jax.ShapeDtypeStruct((nb, N_EXPERTS, t), F32)],
        compiler_params=_cparams(("arbitrary", "arbitrary")),
        name="router",
    )(h, nw, mod, w_pad)


def _lane_prefix(x01, tri):
    n = x01.shape[1]
    lane = lax.broadcasted_iota(I32, (1, 128), 1)
    off = jnp.zeros((x01.shape[0], 1), F32)
    before = jnp.zeros((x01.shape[0], 128), F32)
    outs = []
    for blk in range(n // 128):
        before = jnp.where(lane == blk, off, before)
        cb = jnp.dot(x01[:, blk * 128:(blk + 1) * 128].astype(BF16), tri, preferred_element_type=F32) + off
        outs.append(cb)
        off = cb[:, 127:128]
    return jnp.concatenate(outs, axis=1), before


def _select_kernel(aff_ref, tri_ref, slot_ref, s0_ref, *, cap):
    nb, n_exp, n = aff_ref.shape
    aff = aff_ref[...].reshape(nb * n_exp, n)
    ne = aff.shape[0]

    def bisect(_, c):
        lo, hi = c
        m2 = 0.5 * (lo + hi)
        m1 = 0.5 * (lo + m2)
        m3 = 0.5 * (m2 + hi)
        ok1, ok2, ok3 = (jnp.sum(jnp.where(aff >= m, 1, 0), axis=1, keepdims=True) >= cap for m in (m1, m2, m3))
        new_lo = jnp.where(ok3, m3, jnp.where(ok2, m2, jnp.where(ok1, m1, lo)))
        new_hi = jnp.where(ok3, hi, jnp.where(ok2, m3, jnp.where(ok1, m2, m1)))
        return new_lo, new_hi

    lo, _ = lax.fori_loop(0, SELECT_STEPS // 2, bisect, (jnp.zeros((ne, 1), F32), jnp.full((ne, 1), 2.0, F32)))
    thr = jnp.min(jnp.where(aff >= lo, aff, 2.0), axis=1, keepdims=True)
    gt = aff > thr
    eq = aff == thr
    need = (cap - jnp.sum(jnp.where(gt, 1, 0), axis=1, keepdims=True)).astype(F32)
    eq_f = jnp.where(eq, 1.0, 0.0)
    eq_incl, _ = _lane_prefix(eq_f, tri_ref[...])
    sel = gt | (eq & ((eq_incl - eq_f) < need))
    sel_f = jnp.where(sel, 1.0, 0.0)
    pos_incl, before = _lane_prefix(sel_f, tri_ref[...])
    slot_ref[...] = jnp.where(sel, pos_incl - 1.0, -1.0).astype(I32).reshape(nb, n_exp, n)
    s0_ref[...] = before.astype(I32).reshape(nb, n_exp, 128)


def _select(aff, n, blk_idx, cap):
    nb = aff.shape[0]
    tri = jnp.asarray(np.arange(128)[:, None] <= np.arange(128)[None, :], BF16)
    return pl.pallas_call(
        functools.partial(_select_kernel, cap=cap),
        grid=(1,),
        in_specs=[pl.BlockSpec((nb, N_EXPERTS, n), lambda g: (0, 0, blk_idx)),
                  pl.BlockSpec((128, 128), lambda g: (0, 0))],
        out_specs=[pl.BlockSpec((nb, N_EXPERTS, n), lambda g: (0, 0, 0)),
                   pl.BlockSpec((nb, N_EXPERTS, 128), lambda g: (0, 0, 0))],
        out_shape=[jax.ShapeDtypeStruct((nb, N_EXPERTS, n), I32),
                   jax.ShapeDtypeStruct((nb, N_EXPERTS, 128), I32)],
        compiler_params=_cparams(("arbitrary",)),
        name="select",
    )(aff, tri)


SELECT_STEPS = 152
COMPACT_WIN = 128 + 16
COMPACT_WIN_SMALL = 48
EXPERT_ROW_CHUNK = 256
COMBINE_WIN_SMALL = 48
COMBINE_WIN = 256


def _window_starts(s0_ref, b, i, nblk, cap, w):
    starts, fits = [], None
    for e in range(N_EXPERTS):
        base = (b * N_EXPERTS + e) * 128
        start = jnp.minimum((s0_ref[base + i] // 16) * 16, cap - w)
        end = jnp.where(i + 1 < nblk, s0_ref[base + jnp.minimum(i + 1, nblk - 1)], cap)
        ok = end - start <= w
        fits = ok if fits is None else jnp.logical_and(fits, ok)
        starts.append(pl.multiple_of(start, 16))
    return starts, fits


def _compact_kernel(s0_ref, hn_ref, slot_ref, aff_ref, xs_ref, g_ref, *, cap):
    b, c, i = pl.program_id(0), pl.program_id(1), pl.program_id(2)
    nblk = pl.num_programs(2)

    @pl.when(i == 0)
    def _():
        xs_ref[...] = jnp.zeros(xs_ref.shape, BF16)

    @pl.when(jnp.logical_and(i == 0, c == 0))
    def _():
        g_ref[...] = jnp.zeros(g_ref.shape, F32)

    def scatter(w):
        starts, _ = _window_starts(s0_ref, b, i, nblk, cap, w)
        slots = slot_ref[...]
        row = lax.broadcasted_iota(I32, (w, 128), 0)
        hits = [row == (slots[e:e + 1] - starts[e]) for e in range(N_EXPERTS)]
        lhs = jnp.concatenate([jnp.where(hit, 1.0, 0.0).astype(BF16) for hit in hits], axis=0)
        res = jnp.dot(lhs, hn_ref[...], preferred_element_type=F32)
        for e in range(N_EXPERTS):
            cur = xs_ref[e, pl.ds(starts[e], w), :].astype(F32)
            xs_ref[e, pl.ds(starts[e], w), :] = (cur + res[e * w:(e + 1) * w]).astype(BF16)

        @pl.when(c == 0)
        def _():
            affs = aff_ref[...]
            for e in range(N_EXPERTS):
                gv = jnp.sum(jnp.where(hits[e], affs[e:e + 1], 0.0), axis=1, keepdims=True)
                g_ref[e, pl.ds(starts[e], w), :] += jnp.broadcast_to(gv, (w, 128))

    w_small, w_big = min(cap, COMPACT_WIN_SMALL), min(cap, COMPACT_WIN)
    if w_small == w_big:
        scatter(w_big)
    else:
        _, fits = _window_starts(s0_ref, b, i, nblk, cap, w_small)

        @pl.when(fits)
        def _():
            scatter(w_small)

        @pl.when(jnp.logical_not(fits))
        def _():
            scatter(w_big)


def _compact(hn, slot, aff, s0, n, cap, tok_blk0):
    nb, _, d = hn.shape
    nblk = n // 128
    dcols = d // 2 if d >= 256 else d
    grid_spec = pltpu.PrefetchScalarGridSpec(
        num_scalar_prefetch=1,
        grid=(nb, d // dcols, nblk),
        in_specs=[pl.BlockSpec((None, 128, dcols), lambda b, c, i, s: (b, tok_blk0 + i, c)),
                  pl.BlockSpec((None, N_EXPERTS, 128), lambda b, c, i, s: (b, 0, i)),
                  pl.BlockSpec((None, N_EXPERTS, 128), lambda b, c, i, s: (b, 0, tok_blk0 + i))],
        out_specs=[pl.BlockSpec((N_EXPERTS, cap, dcols), lambda b, c, i, s: (0, b, c)),
                   pl.BlockSpec((N_EXPERTS, cap, 128), lambda b, c, i, s: (0, b, 0))],
    )
    return pl.pallas_call(
        functools.partial(_compact_kernel, cap=cap),
        grid_spec=grid_spec,
        out_shape=[jax.ShapeDtypeStruct((N_EXPERTS, nb * cap, d), BF16),
                   jax.ShapeDtypeStruct((N_EXPERTS, nb * cap, 128), F32)],
        compiler_params=_cparams(("arbitrary", "arbitrary", "arbitrary")),
        name="compact",
    )(s0.reshape(-1), hn, slot, aff)


def _expert_kernel(*refs, n_in):
    xs = refs[0:n_in]
    gs = refs[n_in:2 * n_in]
    wg_ref, wu_ref, wd_ref = refs[2 * n_in:2 * n_in + 3]
    ys = refs[2 * n_in + 3:3 * n_in + 3]
    accs = refs[3 * n_in + 3:4 * n_in + 3]
    wgu_ref, wdb_ref = refs[4 * n_in + 3:]
    f = pl.program_id(1)
    last = pl.num_programs(1) - 1
    tf = wg_ref.shape[1]
    wgu_ref[:, 0:tf] = wg_ref[...].astype(BF16)
    wgu_ref[:, tf:2 * tf] = wu_ref[...].astype(BF16)
    wdb_ref[...] = wd_ref[...].astype(BF16)

    @pl.when(f == 0)
    def _():
        for acc_ref in accs:
            acc_ref[...] = jnp.zeros(acc_ref.shape, F32)

    chunks = []
    for x_ref, acc_ref in zip(xs, accs):
        m = x_ref.shape[0]
        rc = min(m, EXPERT_ROW_CHUNK)
        chunks += [(x_ref, acc_ref, r0, rc) for r0 in range(0, m, rc)]

    def up(chunk):
        x_ref, _, r0, rc = chunk
        return jnp.dot(x_ref[r0:r0 + rc, :], wgu_ref[...], preferred_element_type=F32)

    def down(chunk, au):
        _, acc_ref, r0, rc = chunk
        hid = (_silu(au[:, 0:tf]) * au[:, tf:2 * tf]).astype(BF16)
        acc_ref[r0:r0 + rc, :] += jnp.dot(hid, wdb_ref[...], preferred_element_type=F32)

    au_prev = up(chunks[0])
    for ci in range(1, len(chunks)):
        au_next = up(chunks[ci])
        down(chunks[ci - 1], au_prev)
        au_prev = au_next
    down(chunks[-1], au_prev)

    @pl.when(f == last)
    def _():
        for g_ref, y_ref, acc_ref in zip(gs, ys, accs):
            gate = g_ref[...]
            for cblk in range(acc_ref.shape[1] // 128):
                sl = slice(cblk * 128, (cblk + 1) * 128)
                y_ref[:, sl] = (acc_ref[:, sl] * gate).astype(BF16)


def _experts(xs_list, g_list, w_gate, w_up, w_down, layer):
    _, ne, d, ff = w_gate.shape
    tf = 256
    n_in = len(xs_list)
    xspec = lambda m, w: pl.BlockSpec((None, m, w), lambda e, f: (e, 0, 0))
    return pl.pallas_call(
        functools.partial(_expert_kernel, n_in=n_in),
        grid=(ne, ff // tf),
        in_specs=([xspec(x.shape[1], d) for x in xs_list] + [xspec(g.shape[1], 128) for g in g_list]
                  + [pl.BlockSpec((None, None, d, tf), lambda e, f: (layer, e, 0, f)),
                     pl.BlockSpec((None, None, d, tf), lambda e, f: (layer, e, 0, f)),
                     pl.BlockSpec((None, None, tf, d), lambda e, f: (layer, e, f, 0))]),
        out_specs=[xspec(x.shape[1], d) for x in xs_list],
        out_shape=[jax.ShapeDtypeStruct(x.shape, BF16) for x in xs_list],
        scratch_shapes=([pltpu.VMEM((x.shape[1], d), F32) for x in xs_list]
                        + [pltpu.VMEM((d, 2 * tf), BF16), pltpu.VMEM((tf, d), BF16)]),
        compiler_params=_cparams(("arbitrary", "arbitrary")),
        name="experts",
    )(*xs_list, *g_list, w_gate, w_up, w_down)


def _combine_kernel(s0_ref, y_ref, slot_ref, expand_ref, lane_ref, h_ref, mod_ref, o_ref, *,
                    cap, win, win_small, mod_row_static):
    b, i = pl.program_id(0), pl.program_id(2)
    nblk = pl.num_programs(2)
    mrow = mod_ref[b] if mod_row_static is None else mod_ref[mod_row_static]

    def combine_per_expert(w):
        starts, _ = _window_starts(s0_ref, b, i, nblk, cap, w)
        slot_t = slot_ref[...]
        jj = lax.broadcasted_iota(I32, (1, w), 1)
        acc = None
        for e in range(N_EXPERTS):
            hit = (slot_t[:, e:e + 1] - starts[e]) == jj
            part = jnp.dot(jnp.where(hit, 1.0, 0.0).astype(BF16), y_ref[e, pl.ds(starts[e], w), :],
                           preferred_element_type=F32)
            acc = part if acc is None else acc + part
        o_ref[...] = h_ref[...] + mrow[5:6] * acc

    def combine_stacked(w):
        starts, _ = _window_starts(s0_ref, b, i, nblk, cap, w)
        lane_e, lane_j = lane_ref[0:1, :], lane_ref[1:2, :]
        slot_exp = _dot01_rhs(slot_ref[...].astype(F32), expand_ref[...])
        start_exp = jnp.zeros(lane_e.shape, I32)
        for e in range(N_EXPERTS):
            start_exp = jnp.where(lane_e == e, starts[e], start_exp)
        hit = (slot_exp.astype(I32) - start_exp) == lane_j
        ywin = jnp.concatenate([y_ref[e, pl.ds(starts[e], w), :] for e in range(N_EXPERTS)], axis=0)
        acc = jnp.dot(jnp.where(hit, 1.0, 0.0).astype(BF16), ywin, preferred_element_type=F32)
        o_ref[...] = h_ref[...] + mrow[5:6] * acc

    if win_small >= win:
        combine_per_expert(win)
    else:
        _, fits = _window_starts(s0_ref, b, i, nblk, cap, win_small)

        @pl.when(fits)
        def _():
            combine_stacked(win_small)

        @pl.when(jnp.logical_not(fits))
        def _():
            combine_per_expert(win)


def _combine(y, slot, s0, h, mod, n, cap, tok_blk0, is_ctx):
    nb, t, d = h.shape
    nblk = n // 128
    dcols = d // 2 if d >= 256 else d
    win = min(cap, COMBINE_WIN)
    win_small = min(cap, COMBINE_WIN_SMALL)
    slot_t = jnp.swapaxes(slot, 1, 2)
    kcols = N_EXPERTS * win_small
    col = np.arange(kcols)
    expand = jnp.asarray(np.arange(N_EXPERTS)[:, None] == (col // win_small)[None, :], BF16)
    lane_tab = jnp.asarray(np.stack([col // win_small, col % win_small]), I32)
    full = lambda shp: pl.BlockSpec(shp, lambda b, c, i, s, n=len(shp): (0,) * n)
    grid_spec = pltpu.PrefetchScalarGridSpec(
        num_scalar_prefetch=1,
        grid=(nb, d // dcols, nblk),
        in_specs=[pl.BlockSpec((N_EXPERTS, cap, dcols), lambda b, c, i, s: (0, b, c)),
                  pl.BlockSpec((None, 128, N_EXPERTS), lambda b, c, i, s: (b, i, 0)),
                  full((N_EXPERTS, kcols)), full((2, kcols)),
                  pl.BlockSpec((None, 128, dcols), lambda b, c, i, s: (b, tok_blk0 + i, c)),
                  pl.BlockSpec((8, 6, dcols), lambda b, c, i, s: (0, 0, c))],
        out_specs=pl.BlockSpec((None, 128, dcols), lambda b, c, i, s: (b, tok_blk0 + i, c)),
    )
    return pl.pallas_call(
        functools.partial(_combine_kernel, cap=cap, win=win, win_small=win_small,
                          mod_row_static=nb if is_ctx else None),
        grid_spec=grid_spec,
        out_shape=jax.ShapeDtypeStruct((nb, t, d), F32),
        input_output_aliases={5: 0},
        compiler_params=_cparams(("arbitrary", "arbitrary", "arbitrary")),
        name="combine",
    )(s0.reshape(-1), y, slot_t, expand, lane_tab, h, mod)


def _moe(h, nw, mod, w_router, w_gate, w_up, w_down, layer, s_lat, l_ctx, with_ctx):
    hn, aff = _router(h, nw, mod, w_router, s_lat)
    streams = [(s_lat, 0, 0, False)]
    if with_ctx:
        streams.append((l_ctx, s_lat // l_ctx, s_lat // 128, True))
    sel = []
    for n, blk_idx, tok_blk0, _ in streams:
        cap = EC_CAPACITY * n // N_EXPERTS
        slot, s0 = _select(aff, n, blk_idx, cap)
        xs, gates = _compact(hn, slot, aff, s0, n, cap, tok_blk0)
        sel.append((slot, s0, xs, gates, cap))
    ys = _experts([s[2] for s in sel], [s[3] for s in sel], w_gate, w_up, w_down, layer)
    for (n, _, tok_blk0, is_ctx), (slot, s0, _, _, cap), y in zip(streams, sel, ys):
        h = _combine(y, slot, s0, h, mod, n, cap, tok_blk0, is_ctx)
    return h


def _final_norm_kernel(h_ref, w_ref, o_ref):
    x = h_ref[...]
    o_ref[...] = x * lax.rsqrt(jnp.mean(x * x, axis=-1, keepdims=True) + NORM_EPS) * w_ref[...]


def _final_norm(h, w, s_lat):
    nb, _, d = h.shape
    tm = 512
    return pl.pallas_call(
        _final_norm_kernel,
        grid=(nb, s_lat // tm),
        in_specs=[pl.BlockSpec((None, tm, d), lambda b, i: (b, i, 0)),
                  pl.BlockSpec((1, d), lambda b, i: (0, 0))],
        out_specs=pl.BlockSpec((None, tm, d), lambda b, i: (b, i, 0)),
        out_shape=jax.ShapeDtypeStruct((nb, s_lat, d), F32),
        compiler_params=_cparams(("arbitrary", "arbitrary")),
        name="final_norm",
    )(h, w)


def _reorder_w_in(w_in):
    d = w_in.shape[0]
    lr0 = COL_GDN_Q
    qkv0 = lr0 + 2 * GLA_RANK
    ab0 = qkv0 + 3 * GDN_HEADS * GDN_DK + GDN_HEADS * GDN_DV
    n_in = ab0 + 4 * GDN_HEADS
    return jnp.concatenate([w_in[:, :lr0], w_in[:, qkv0:ab0], w_in[:, lr0:qkv0], w_in[:, ab0:n_in],
                            jnp.zeros((d, NP_COLS - n_in), w_in.dtype)], axis=1).astype(BF16)


def _token_mixers(h, nw, mod, w_in, w_out, layer, rpb, gate_w, gate_b, gla_nw, conv_w, a_log, dt_bias, gdn_nw,
                  rope, s_lat, l_ctx):
    p = _inproj(h, nw, mod, _reorder_w_in(w_in), s_lat)
    o_na = _na(p, _na_bias_table(rpb), s_lat, l_ctx)
    gw_p = jnp.zeros((2, 128, GLA_HEADS * GLA_DK), F32)
    for dr in range(2):
        gw_p = gw_p.at[dr, dr * GLA_RANK:(dr + 1) * GLA_RANK].set(gate_w[dr])
    qr, kr, cf, cb = _gla_prep(p, rope[0], rope[1], gw_p, gate_b[:, None, :], s_lat)
    o_gla = _gla_scan(qr, kr, cf, cb, p, s_lat, l_ctx)
    qn, kn, vn = (_gdn_conv(p, conv_w, s_lat, part) for part in range(3))
    o_gdn = _gdn_scan(*_gdn_chunk(qn, kn, vn, p, a_log, dt_bias), s_lat, l_ctx)
    return _outproj(o_na, o_gla, o_gdn, p, gla_nw[None, :], gdn_nw[None, :], w_out, layer, h, mod, s_lat)


def kernel(x, c, ctx, c_ctx, w_ada, b_ada, norm_mix_w, norm_ffn_w, w_in, w_out, na_rpb, gla_gate_w, gla_gate_b,
           gla_norm_w, gdn_conv_w, gdn_a_log, gdn_dt_bias, gdn_norm_w, w_router, w_exp_gate, w_exp_up,
           w_exp_down, final_norm_w):
    nb, s_lat, d = x.shape
    l_ctx = ctx.shape[1]
    depth = w_ada.shape[0]
    assert nb < 8 and s_lat % 256 == 0 and l_ctx == 256 and s_lat // GRID_W >= NA_WIN_R
    h = jnp.concatenate([x, ctx], axis=1)
    cvec = jnp.zeros((8, d), F32).at[:nb].set(c).at[nb].set(c_ctx)
    mods = _ada(cvec, w_ada, b_ada).reshape(depth, 8, 6, d)
    rope = _rope_tables(s_lat, l_ctx)
    w_out = w_out.astype(BF16)
    for l in range(depth):
        h = _token_mixers(h, norm_mix_w[l][None, :], mods[l], w_in[l], w_out, l, na_rpb[l], gla_gate_w[l],
                          gla_gate_b[l], gla_norm_w[l], gdn_conv_w[l], gdn_a_log[l], gdn_dt_bias[l],
                          gdn_norm_w[l], rope, s_lat, l_ctx)
        h = _moe(h, norm_ffn_w[l][None, :], mods[l], w_router[l], w_exp_gate, w_exp_up, w_exp_down, l,
                 s_lat, l_ctx, with_ctx=l < depth - 1)
    return _final_norm(h, final_norm_w[None, :], s_lat)
```

```python
import functools

import numpy as np
import jax
import jax.numpy as jnp
from jax import lax
from jax.experimental import pallas as pl
from jax.experimental.pallas import tpu as pltpu

F32 = jnp.float32
BF16 = jnp.bfloat16
I32 = jnp.int32
HI = lax.Precision.HIGHEST

GRID_W = 64
NA_HEADS, NA_DH, NA_WIN_R, NA_WIN_C = 16, 64, 8, 16
GLA_HEADS, GLA_DK, GLA_DV, GLA_RANK, GLA_TAU = 4, 64, 128, 16, 16.0
GLA_BLK = 16
GDN_HEADS, GDN_DK, GDN_DV, GDN_CONV, GDN_CHUNK = 4, 128, 128, 5, 64
N_EXPERTS, EC_CAPACITY = 16, 2
ROPE_BASE = 10000.0
NORM_EPS = 1e-6
NEG = -1e30

NA_W = NA_HEADS * NA_DH
COL_NA_Q, COL_NA_K, COL_NA_V = 0, NA_W, 2 * NA_W
COL_GLA_Q = 3 * NA_W
COL_GLA_K = COL_GLA_Q + GLA_HEADS * GLA_DK
COL_GLA_V = COL_GLA_K + GLA_HEADS * GLA_DK
COL_GLA_G = COL_GLA_V + GLA_HEADS * GLA_DV
COL_GDN_Q = COL_GLA_G + GLA_HEADS * GLA_DV
COL_GDN_K = COL_GDN_Q + GDN_HEADS * GDN_DK
COL_GDN_V = COL_GDN_K + GDN_HEADS * GDN_DK
COL_GDN_Z = COL_GDN_V + GDN_HEADS * GDN_DV
COL_SMALL = COL_GDN_Z + GDN_HEADS * GDN_DV
SMALL_A, SMALL_B = 2 * GLA_RANK, 2 * GLA_RANK + 2 * GDN_HEADS
INPROJ_TN = 1024
NP_COLS = 7168

VMEM_LIMIT = 56 * 1024 * 1024


def _cparams(sem):
    return pltpu.CompilerParams(dimension_semantics=sem, vmem_limit_bytes=VMEM_LIMIT)


def _sigmoid(x):
    return 1.0 / (1.0 + jnp.exp(-x))


def _silu(x):
    return x * _sigmoid(x)


def _softplus(x):
    return jnp.maximum(x, 0.0) + jnp.log(1.0 + jnp.exp(-jnp.abs(x)))


def _bdot(a, b):
    return jnp.dot(a.astype(BF16), b.astype(BF16), preferred_element_type=F32)


def _dot_nt(a, b):
    return lax.dot_general(a, b, (((1,), (1,)), ((), ())), preferred_element_type=F32)


def _dot_tn(a, b):
    return lax.dot_general(a, b, (((0,), (0,)), ((), ())), preferred_element_type=F32)


def _dot01_rhs(x, m01):
    hi = x.astype(BF16)
    r1 = x - hi.astype(F32)
    mid = r1.astype(BF16)
    lo = (r1 - mid.astype(F32)).astype(BF16)
    d = lambda p: jnp.dot(p, m01, preferred_element_type=F32)
    return d(hi) + d(mid) + d(lo)


def _dot01(m01, x):
    hi = x.astype(BF16)
    r1 = x - hi.astype(F32)
    mid = r1.astype(BF16)
    lo = (r1 - mid.astype(F32)).astype(BF16)
    d = lambda p: jnp.dot(m01, p, preferred_element_type=F32)
    return d(hi) + d(mid) + d(lo)


def _ada_kernel(c_ref, w_ref, b_ref, o_ref):
    o_ref[...] = _bdot(_silu(c_ref[...]), w_ref[...]) + b_ref[...]


def _ada(cvec, w_ada, b_ada):
    depth, d, n6 = w_ada.shape
    tn = 2048 if n6 % 2048 == 0 else 512
    assert n6 % tn == 0
    return pl.pallas_call(
        _ada_kernel,
        grid=(depth, n6 // tn),
        in_specs=[pl.BlockSpec((8, d), lambda l, j: (0, 0)),
                  pl.BlockSpec((None, d, tn), lambda l, j: (l, 0, j)),
                  pl.BlockSpec((None, 1, tn), lambda l, j: (l, 0, j))],
        out_specs=pl.BlockSpec((None, 8, tn), lambda l, j: (l, 0, j)),
        out_shape=jax.ShapeDtypeStruct((depth, 8, n6), F32),
        compiler_params=_cparams(("arbitrary", "arbitrary")),
        name="ada",
    )(cvec, w_ada, b_ada.reshape(depth, 1, n6))


def _mod_norm(x, nw, mod_b, mod_c, row0, s_lat, k_shift):
    ms = jnp.mean(x * x, axis=-1, keepdims=True)
    y = x * lax.rsqrt(ms + NORM_EPS) * nw
    row = row0 + lax.broadcasted_iota(I32, (x.shape[0], 1), 0)
    is_lat = row < s_lat
    shift = jnp.where(is_lat, mod_b[k_shift:k_shift + 1], mod_c[k_shift:k_shift + 1])
    scale = jnp.where(is_lat, mod_b[k_shift + 1:k_shift + 2], mod_c[k_shift + 1:k_shift + 2])
    return y * (1.0 + scale) + shift


def _inproj_kernel(x_ref, nw_ref, mod_ref, w_ref, o_ref, xn_ref, *, s_lat, tm, nb):
    b, i, j = pl.program_id(0), pl.program_id(1), pl.program_id(2)

    @pl.when(j == 0)
    def _():
        rc = tm // 4

        def chunk(ci, carry):
            r = pl.multiple_of(ci * rc, 16)
            xn = _mod_norm(x_ref[pl.ds(r, rc), :], nw_ref[...], mod_ref[b], mod_ref[nb], i * tm + r, s_lat, 0)
            xn_ref[pl.ds(r, rc), :] = xn.astype(BF16)
            return carry

        lax.fori_loop(0, 4, chunk, 0)

    o_ref[...] = jnp.dot(xn_ref[...], w_ref[...].astype(BF16), preferred_element_type=F32)


def _inproj(h, nw, mod, w_p, s_lat):
    nb, t, d = h.shape
    tm = t // 4
    tn = INPROJ_TN
    npc = w_p.shape[1]
    return pl.pallas_call(
        functools.partial(_inproj_kernel, s_lat=s_lat, tm=tm, nb=nb),
        grid=(nb, t // tm, npc // tn),
        in_specs=[pl.BlockSpec((None, tm, d), lambda b, i, j: (b, i, 0)),
                  pl.BlockSpec((1, d), lambda b, i, j: (0, 0)),
                  pl.BlockSpec((8, 6, d), lambda b, i, j: (0, 0, 0)),
                  pl.BlockSpec((d, tn), lambda b, i, j: (0, j))],
        out_specs=pl.BlockSpec((None, tm, tn), lambda b, i, j: (b, i, j)),
        out_shape=jax.ShapeDtypeStruct((nb, t, npc), F32),
        scratch_shapes=[pltpu.VMEM((tm, d), BF16)],
        compiler_params=_cparams(("arbitrary", "arbitrary", "arbitrary")),
        name="inproj",
    )(h, nw, mod, w_p)


NA_NDR = 2 * NA_WIN_R - 1
NA_NDC = 2 * NA_WIN_C - 1


def _na_bias_kernel(rpb_ref, onehot_ref, mask_ref, o_ref):
    t = jnp.dot(rpb_ref[...], onehot_ref[...], precision=HI, preferred_element_type=F32)
    o_ref[...] = jnp.where(mask_ref[...] > 0.0, t, NEG)


def _na_bias_table(rpb):
    nh = rpb.shape[0]
    cq = np.arange(GRID_W)
    dc = np.clip(cq[None, :] - cq[:, None] + NA_WIN_C - 1, 0, NA_NDC - 1)
    cstart = np.clip(cq - NA_WIN_C // 2, 0, GRID_W - NA_WIN_C)
    colmask = (cq[None, :] >= cstart[:, None]) & (cq[None, :] < cstart[:, None] + NA_WIN_C)
    onehot = jnp.asarray(np.arange(128)[:, None] == dc.reshape(1, -1), F32)
    mask = jnp.asarray(colmask.reshape(1, -1), F32)
    rpb_p = jnp.zeros((nh, 16, 128), F32).at[:, :NA_NDR, :NA_NDC].set(rpb)
    full = pl.pallas_call(
        _na_bias_kernel,
        grid=(nh,),
        in_specs=[pl.BlockSpec((None, 16, 128), lambda h: (h, 0, 0)),
                  pl.BlockSpec((128, GRID_W * GRID_W), lambda h: (0, 0)),
                  pl.BlockSpec((1, GRID_W * GRID_W), lambda h: (0, 0))],
        out_specs=pl.BlockSpec((None, 16, GRID_W * GRID_W), lambda h: (h, 0, 0)),
        out_shape=jax.ShapeDtypeStruct((nh, 16, GRID_W * GRID_W), F32),
        compiler_params=_cparams(("arbitrary",)),
        name="na_bias",
    )(rpb_p, onehot, mask)
    t15 = full[:, :NA_NDR].reshape(nh, NA_NDR, GRID_W, GRID_W)
    return jnp.concatenate([t15[:, :-1], t15[:, 1:]], axis=-1)


def _softmax_pv_many(s_lists, v_lists):
    n = len(s_lists)
    ms = []
    for sl in s_lists:
        m = sl[0].max(axis=-1, keepdims=True)
        for s in sl[1:]:
            m = jnp.maximum(m, s.max(axis=-1, keepdims=True))
        ms.append(m)
    ps = [[jnp.exp(s - ms[c]) for s in s_lists[c]] for c in range(n)]
    dens = []
    for c in range(n):
        den = ps[c][0].sum(axis=-1, keepdims=True)
        for p in ps[c][1:]:
            den = den + p.sum(axis=-1, keepdims=True)
        dens.append(den)
    outs = []
    for c in range(n):
        o = jnp.dot(ps[c][0].astype(BF16), v_lists[c][0], preferred_element_type=F32)
        for p, v in zip(ps[c][1:], v_lists[c][1:]):
            o = o + jnp.dot(p.astype(BF16), v, preferred_element_type=F32)
        outs.append(o)
    return [o / den for o, den in zip(outs, dens)]


NA_ROWS_PER_STEP = 8


def _na_kernel(q_ref, k_ref, v_ref, bias_ref, o_ref, kb_ref, vb_ref, *, s_lat, l_ctx):
    rows = s_lat // GRID_W
    nwin = NA_WIN_R * GRID_W
    scale = NA_DH ** -0.5
    kb_ref[...] = k_ref[...].astype(BF16)
    vb_ref[...] = v_ref[...].astype(BF16)
    lane = lax.broadcasted_iota(I32, (1, 2 * NA_DH), 1)
    head_lanes = (lane < NA_DH, lane >= NA_DH)
    kc = kb_ref[pl.ds(s_lat, l_ctx), :]
    vc = vb_ref[pl.ds(s_lat, l_ctx), :]

    qc = q_ref[pl.ds(s_lat, l_ctx), :] * scale
    qhs = [jnp.where(head_lanes[hh], qc, 0.0).astype(BF16) for hh in range(2)]
    outs = _softmax_pv_many([[_dot_nt(qh, kc)] for qh in qhs], [[vc], [vc]])
    o_ref[pl.ds(s_lat, l_ctx), :] = jnp.where(head_lanes[0], outs[0], outs[1])

    def body(it, carry):
        s_lists, v_lists, starts = [], [], []
        for rr in range(NA_ROWS_PER_STEP):
            r = it * NA_ROWS_PER_STEP + rr
            r0 = jnp.clip(r - NA_WIN_R // 2, 0, rows - NA_WIN_R)
            d = r - r0
            qs = pl.multiple_of(r * GRID_W, GRID_W)
            ks = pl.multiple_of(r0 * GRID_W, GRID_W)
            starts.append(qs)
            q = q_ref[pl.ds(qs, GRID_W), :] * scale
            kw = kb_ref[pl.ds(ks, nwin), :]
            vw = vb_ref[pl.ds(ks, nwin), :]
            q2 = jnp.concatenate([jnp.where(head_lanes[hh], q, 0.0) for hh in range(2)], axis=0).astype(BF16)
            bias = jnp.concatenate(
                [jnp.concatenate([bias_ref[hh, 2 * m + NA_WIN_R - 1 - d] for hh in range(2)], axis=0)
                 for m in range(NA_WIN_R // 2)], axis=1)
            s_lists.append([_dot_nt(q2, kw) + bias, _dot_nt(q2, kc)])
            v_lists.append([vw, vc])
        res = _softmax_pv_many(s_lists, v_lists)
        for rr in range(NA_ROWS_PER_STEP):
            o_ref[pl.ds(starts[rr], GRID_W), :] = jnp.where(head_lanes[0], res[rr][0:GRID_W], res[rr][GRID_W:2 * GRID_W])
        return carry

    lax.fori_loop(0, rows // NA_ROWS_PER_STEP, body, 0)


def _na(p, bias_tbl, s_lat, l_ctx):
    nb, t, _ = p.shape
    blk = lambda off: pl.BlockSpec((None, t, 128), lambda b, g, off=off: (b, 0, off // 128 + g))
    return pl.pallas_call(
        functools.partial(_na_kernel, s_lat=s_lat, l_ctx=l_ctx),
        grid=(nb, NA_HEADS // 2),
        in_specs=[blk(COL_NA_Q), blk(COL_NA_K), blk(COL_NA_V),
                  pl.BlockSpec((2, NA_NDR - 1, GRID_W, 2 * GRID_W), lambda b, g: (g, 0, 0, 0))],
        out_specs=pl.BlockSpec((None, t, 128), lambda b, g: (b, 0, g)),
        out_shape=jax.ShapeDtypeStruct((nb, t, NA_W), F32),
        scratch_shapes=[pltpu.VMEM((t, 128), BF16), pltpu.VMEM((t, 128), BF16)],
        compiler_params=_cparams(("arbitrary", "arbitrary")),
        name="na",
    )(p, p, p, bias_tbl)


def _rope_tables(s_lat, l_ctx):
    width = GLA_HEADS * GLA_DK
    nf = GLA_DK // 4
    pos = np.arange(s_lat)
    lane = np.arange(width)
    sub = lane % GLA_DK
    freqs = ROPE_BASE ** (-np.arange(nf, dtype=np.float64) / nf)
    p_sel = np.where((sub < GLA_DK // 2)[None, :], (pos // GRID_W)[:, None], (pos % GRID_W)[:, None])
    ang = p_sel.astype(np.float64) * freqs[sub % nf][None, :]
    sign = np.where((sub % (2 * nf)) < nf, -1.0, 1.0)
    cos = np.concatenate([np.cos(ang), np.ones((l_ctx, width))], axis=0)
    sin = np.concatenate([np.sin(ang) * sign[None, :], np.zeros((l_ctx, width))], axis=0)
    return jnp.asarray(cos, F32), jnp.asarray(sin, F32)


def _block_tri(n, blk):
    i = np.arange(n)
    same = (i[:, None] // blk) == (i[None, :] // blk)
    lower = same & (i[None, :] <= i[:, None])
    upper = same & (i[None, :] >= i[:, None])
    return jnp.asarray(lower, BF16), jnp.asarray(upper, BF16)


def _gla_prep_kernel(q_ref, k_ref, sm_ref, cos_ref, sin_ref, gw_ref, gb_ref, tl_ref, tu_ref,
                     qr_ref, kr_ref, cf_ref, cb_ref):
    width = GLA_HEADS * GLA_DK
    nf = GLA_DK // 4
    lane = lax.broadcasted_iota(I32, (1, width), 1)
    first = (lane % (2 * nf)) < nf
    cos, sin = cos_ref[...], sin_ref[...]

    def rope(x):
        swapped = jnp.where(first, pltpu.roll(x, width - nf, 1), pltpu.roll(x, nf, 1))
        return x * cos + swapped * sin

    qr_ref[...] = rope(q_ref[...]) * (GLA_DK ** -0.5)
    kr_ref[...] = rope(k_ref[...])
    sm = sm_ref[...]
    for dr, (tri_ref, out_ref) in enumerate(((tl_ref, cf_ref), (tu_ref, cb_ref))):
        z = jnp.dot(sm, gw_ref[dr], precision=HI, preferred_element_type=F32) + gb_ref[dr]
        log_a = (jnp.minimum(z, 0.0) - jnp.log(1.0 + jnp.exp(-jnp.abs(z)))) * (1.0 / GLA_TAU)
        out_ref[...] = _dot01(tri_ref[...], log_a)


def _gla_prep(p, cos, sin, gw_p, gb, s_lat):
    nb, t, _ = p.shape
    width = GLA_HEADS * GLA_DK
    tr = 256
    tl, tu = _block_tri(tr, GLA_BLK)
    row_blk = lambda w, off: pl.BlockSpec((None, tr, w), lambda b, i, off=off, w=w: (b, i, off // w))
    tab = pl.BlockSpec((tr, width), lambda b, i: (i, 0))
    full = lambda shp: pl.BlockSpec(shp, lambda b, i, n=len(shp): (0,) * n)
    out = pl.BlockSpec((None, tr, width), lambda b, i: (b, i, 0))
    return pl.pallas_call(
        _gla_prep_kernel,
        grid=(nb, t // tr),
        in_specs=[row_blk(width, COL_GLA_Q), row_blk(width, COL_GLA_K), row_blk(128, COL_SMALL),
                  tab, tab, full((2, 128, width)), full((2, 1, width)), full((tr, tr)), full((tr, tr))],
        out_specs=[out] * 4,
        out_shape=[jax.ShapeDtypeStruct((nb, t, width), F32)] * 4,
        compiler_params=_cparams(("arbitrary", "arbitrary")),
        name="gla_prep",
    )(p, p, p, cos, sin, gw_p, gb, tl, tu)


def _gla_scan_kernel(qr_ref, kr_ref, cf_ref, cb_ref, v_ref, r3_ref, o_ref, st_ref, *, s_lat, l_ctx):
    nlat, nctx = s_lat // GLA_BLK, l_ctx // GLA_BLK
    o_ref[...] = jnp.zeros(o_ref.shape, F32)
    st_ref[...] = jnp.zeros(st_ref.shape, F32)
    sub = lax.broadcasted_iota(I32, (GLA_BLK, 2 * GLA_DK), 0)
    bd = (lax.broadcasted_iota(I32, (2 * GLA_DV, 2 * GLA_DK), 0) // GLA_DV
          == lax.broadcasted_iota(I32, (2 * GLA_DV, 2 * GLA_DK), 1) // GLA_DK)

    def step(i, carry):
        in_ctx = i < nctx
        jf = jnp.where(in_ctx, nlat + i, i - nctx)
        jb = jnp.where(in_ctx, nlat + nctx - 1 - i, nlat - 1 - (i - nctx))
        dirs = (0, 1)
        rs = [pl.multiple_of(j * GLA_BLK, GLA_BLK) for j in (jf, jb)]
        qs = [qr_ref[pl.ds(r, GLA_BLK), :] for r in rs]
        ks = [kr_ref[pl.ds(r, GLA_BLK), :] for r in rs]
        cums = [c_ref[pl.ds(r, GLA_BLK), :] for c_ref, r in zip((cf_ref, cb_ref), rs)]
        vs = [v_ref[pl.ds(r, GLA_BLK), :] for r in rs]
        tots = [cums[0][GLA_BLK - 1:GLA_BLK], cums[1][0:1]]
        sts = [st_ref[dr] for dr in dirs]
        qds = [(qs[dr] * jnp.exp(cums[dr])).astype(BF16) for dr in dirs]
        kds = [(ks[dr] * jnp.exp(tots[dr] - cums[dr])).astype(BF16) for dr in dirs]
        o_states = [_dot_nt(qds[dr], sts[dr].astype(BF16)) for dr in dirs]
        upds = [_dot_tn(vs[dr].astype(BF16), kds[dr]) for dr in dirs]
        xs = []
        for dr in dirs:
            tiles = []
            for s in range(GLA_BLK):
                valid = (sub >= s) if dr == 0 else (sub <= s)
                w = jnp.exp(jnp.where(valid, cums[dr] - cums[dr][s:s + 1], NEG))
                tiles.append((w * qs[dr] * ks[dr][s:s + 1]).astype(BF16))
            xs.append(jnp.concatenate(tiles, axis=0))
        ress = [jnp.dot(x, r3_ref[...], preferred_element_type=F32) for x in xs]
        for dr in dirs:
            st_ref[dr] = sts[dr] * jnp.exp(tots[dr]) + jnp.where(bd, upds[dr], 0.0)
        for dr in dirs:
            o_diag = ress[dr][0:GLA_BLK] * vs[dr][0:1]
            for s in range(1, GLA_BLK):
                o_diag = o_diag + ress[dr][s * GLA_BLK:(s + 1) * GLA_BLK] * vs[dr][s:s + 1]
            o_ref[pl.ds(rs[dr], GLA_BLK), :] += o_states[dr] + o_diag
        return carry

    lax.fori_loop(0, nlat + nctx, step, 0, unroll=8)


def _gla_scan(qr, kr, cf, cb, p, s_lat, l_ctx):
    nb, t, _ = p.shape
    lanes = 2 * GLA_DK
    r3 = jnp.asarray((np.arange(lanes)[:, None] // GLA_DK) == (np.arange(2 * GLA_DV)[None, :] // GLA_DV), BF16)
    blk = pl.BlockSpec((None, t, lanes), lambda b, g: (b, 0, g))
    return pl.pallas_call(
        functools.partial(_gla_scan_kernel, s_lat=s_lat, l_ctx=l_ctx),
        grid=(nb, GLA_HEADS // 2),
        in_specs=[blk, blk, blk, blk,
                  pl.BlockSpec((None, t, 2 * GLA_DV), lambda b, g: (b, 0, COL_GLA_V // (2 * GLA_DV) + g)),
                  pl.BlockSpec((lanes, 2 * GLA_DV), lambda b, g: (0, 0))],
        out_specs=pl.BlockSpec((None, t, 2 * GLA_DV), lambda b, g: (b, 0, g)),
        out_shape=jax.ShapeDtypeStruct((nb, t, GLA_HEADS * GLA_DV), F32),
        scratch_shapes=[pltpu.VMEM((2, 2 * GLA_DV, lanes), F32)],
        compiler_params=_cparams(("arbitrary", "arbitrary")),
        name="gla_scan",
    )(qr, kr, cf, cb, p, r3)


def _gdn_conv_kernel(x_ref, w_ref, o_ref, *, s_lat, normalize, scale):
    t = x_ref.shape[0]
    x = x_ref[...]
    tpos = lax.broadcasted_iota(I32, (t, 1), 0)
    half = GDN_CONV // 2
    acc = x * w_ref[half:half + 1, :]
    for j in range(GDN_CONV):
        if j == half:
            continue
        dlt = j - half
        src = tpos + dlt
        ok = (src >= 0) & (src < t) & ((src < s_lat) == (tpos < s_lat))
        acc = acc + jnp.where(ok, pltpu.roll(x, (-dlt) % t, 0), 0.0) * w_ref[j:j + 1, :]
    y = _silu(acc)
    if normalize:
        y = y * lax.rsqrt(jnp.sum(y * y, axis=-1, keepdims=True) + 1e-6) * scale
    o_ref[...] = y


def _gdn_conv(p, conv_w, s_lat, part):
    nb, t, _ = p.shape
    col = (COL_GDN_Q, COL_GDN_K, COL_GDN_V)[part]
    return pl.pallas_call(
        functools.partial(_gdn_conv_kernel, s_lat=s_lat, normalize=part < 2,
                          scale=GDN_DK ** -0.5 if part == 0 else 1.0),
        grid=(nb, GDN_HEADS),
        in_specs=[pl.BlockSpec((None, t, 128), lambda b, g: (b, 0, col // 128 + g)),
                  pl.BlockSpec((GDN_CONV, 128), lambda b, g: (0, part * GDN_HEADS + g))],
        out_specs=pl.BlockSpec((None, t, 128), lambda b, g: (b, 0, g)),
        out_shape=jax.ShapeDtypeStruct((nb, t, GDN_HEADS * 128), F32),
        compiler_params=_cparams(("arbitrary", "arbitrary")),
        name=f"gdn_conv{part}",
    )(p, conv_w)


GDN_HEADS_PER_STEP = 4


def _gdn_chunk_kernel(q_ref, k_ref, v_ref, sm_ref, alog_ref, dtb_ref,
                      u_ref, w_ref, qd_ref, kd_ref, a_ref, gl_ref, *, tc):
    g = pl.program_id(1)
    cs = GDN_CHUNK
    nh = GDN_HEADS_PER_STEP
    lane = lax.broadcasted_iota(I32, (1, 128), 1)
    subl = lax.broadcasted_iota(I32, (128, 1), 0)
    ti = lax.broadcasted_iota(I32, (tc, tc), 0)
    ui = lax.broadcasted_iota(I32, (tc, tc), 1)
    same = (ti // cs) == (ui // cs)
    eye = jnp.where(ti == ui, 1.0, 0.0)
    sm = sm_ref[...]
    sm_t = sm.T
    qs = [q_ref[:, hh * 128:(hh + 1) * 128] for hh in range(nh)]
    ks = [k_ref[:, hh * 128:(hh + 1) * 128] for hh in range(nh)]
    vs = [v_ref[:, hh * 128:(hh + 1) * 128] for hh in range(nh)]
    kbs = [k.astype(BF16) for k in ks]
    kks = [_dot_nt(kb, kb) for kb in kbs]
    qks = [_dot_nt(q.astype(BF16), kb) for q, kb in zip(qs, kbs)]
    chains = [(hh, dr) for hh in range(nh) for dr in range(2)]
    ms, decays, betas, e_gcs = [], [], [], []
    for hh, dr in chains:
        head = g * nh + hh
        ca = SMALL_A + GDN_HEADS * dr + head
        cbeta = SMALL_B + GDN_HEADS * dr + head
        a_col = jnp.sum(jnp.where(lane == ca, sm, 0.0), axis=1, keepdims=True)
        b_col = jnp.sum(jnp.where(lane == cbeta, sm, 0.0), axis=1, keepdims=True)
        a_row = jnp.sum(jnp.where(subl == ca, sm_t, 0.0), axis=0, keepdims=True)
        neg_rate = -jnp.exp(alog_ref[dr, hh])
        g_col = neg_rate * _softplus(a_col + dtb_ref[dr, hh])
        g_row = neg_rate * _softplus(a_row + dtb_ref[dr, hh])
        beta = _sigmoid(b_col)
        incl = same & ((ui <= ti) if dr == 0 else (ui >= ti))
        strict = same & ((ui < ti) if dr == 0 else (ui > ti))
        incl_t = same & ((ti <= ui) if dr == 0 else (ti >= ui))
        gc_col = jnp.sum(jnp.where(incl, g_row, 0.0), axis=1, keepdims=True)
        gc_row = jnp.sum(jnp.where(incl_t, g_col, 0.0), axis=0, keepdims=True)
        gc_tot = jnp.sum(jnp.where(same, g_row, 0.0), axis=1, keepdims=True)
        decay = jnp.exp(jnp.where(incl, gc_col - gc_row, NEG))
        ms.append(jnp.where(strict, beta * kks[hh] * decay, 0.0))
        decays.append(decay)
        betas.append(beta)
        e_gcs.append(jnp.exp(gc_col))
        kd_ref[hh, dr] = (ks[hh] * jnp.exp(gc_tot - gc_col)).astype(BF16)
        for c in range(tc // cs):
            gl_ref[hh, dr, c * 8:(c + 1) * 8, :] = jnp.broadcast_to(jnp.exp(gc_tot[c * cs:c * cs + 1]), (8, 128))
    invs = [eye - m for m in ms]
    mks = [_bdot(m, m) for m in ms]
    for lvl in range(5):
        invs = [inv + _bdot(inv, mk) for inv, mk in zip(invs, mks)]
        if lvl < 4:
            mks = [_bdot(mk, mk) for mk in mks]
    sols = [_bdot(invs[ci], jnp.concatenate([vs[hh] * betas[ci], ks[hh] * (betas[ci] * e_gcs[ci])], axis=1))
            for ci, (hh, dr) in enumerate(chains)]
    for ci, (hh, dr) in enumerate(chains):
        u_ref[hh, dr] = sols[ci][:, :GDN_DV]
        w_ref[hh, dr] = sols[ci][:, GDN_DV:].astype(BF16)
        qd_ref[hh, dr] = (qs[hh] * e_gcs[ci]).astype(BF16)
        aqk = qks[hh] * decays[ci]
        for c in range(tc // cs):
            blk = aqk[c * cs:(c + 1) * cs]
            folded = blk[:, 0:128]
            for piece in range(1, tc // 128):
                folded = folded + blk[:, piece * 128:(piece + 1) * 128]
            a_ref[hh, dr, c * cs:(c + 1) * cs, :] = folded.astype(BF16)


def _gdn_chunk(qn, kn, vn, p, a_log, dt_bias):
    nb, t, _ = p.shape
    tc = 256
    nh = GDN_HEADS_PER_STEP
    nch = t // GDN_CHUNK
    rb = lambda: pl.BlockSpec((None, tc, nh * 128), lambda b, g, i: (b, i, g))
    par = pl.BlockSpec((2, nh, 1, 1), lambda b, g, i: (0, g, 0, 0))
    ob = lambda w: pl.BlockSpec((None, nh, 2, tc, w), lambda b, g, i: (b, g, 0, i, 0))
    shp = lambda w, dt: jax.ShapeDtypeStruct((nb, GDN_HEADS, 2, t, w), dt)
    return pl.pallas_call(
        functools.partial(_gdn_chunk_kernel, tc=tc),
        grid=(nb, GDN_HEADS // nh, t // tc),
        in_specs=[rb(), rb(), rb(),
                  pl.BlockSpec((None, tc, 128), lambda b, g, i: (b, i, COL_SMALL // 128)), par, par],
        out_specs=[ob(128), ob(128), ob(128), ob(128), ob(128),
                   pl.BlockSpec((None, nh, 2, (tc // GDN_CHUNK) * 8, 128), lambda b, g, i: (b, g, 0, i, 0))],
        out_shape=[shp(128, F32), shp(128, BF16), shp(128, BF16), shp(128, BF16), shp(128, BF16),
                   jax.ShapeDtypeStruct((nb, GDN_HEADS, 2, nch * 8, 128), F32)],
        compiler_params=_cparams(("arbitrary", "arbitrary", "arbitrary")),
        name="gdn_chunk",
    )(qn, kn, vn, p, a_log.reshape(2, GDN_HEADS, 1, 1).astype(F32), dt_bias.reshape(2, GDN_HEADS, 1, 1).astype(F32))


GDN_SCAN_ROWS = 256


def _gdn_scan_kernel(*refs):
    fwd, bwd = refs[0:6], refs[6:12]
    o_refs = refs[12:14]
    st_ref = refs[14]
    cs = GDN_CHUNK
    ncg = GDN_SCAN_ROWS // cs
    chains = [(dr, hh) for dr in range(2) for hh in range(GDN_HEADS)]

    @pl.when(pl.program_id(1) == 0)
    def _():
        st_ref[...] = jnp.zeros(st_ref.shape, F32)

    for ci in range(ncg):
        rows = [slice(ci * cs, (ci + 1) * cs), slice((ncg - 1 - ci) * cs, (ncg - ci) * cs)]
        gls = [slice(ci * 8, ci * 8 + 1), slice((ncg - 1 - ci) * 8, (ncg - 1 - ci) * 8 + 1)]

        def piece(k, dr, hh):
            return (fwd, bwd)[dr][k][hh, rows[dr], :]

        sts = [st_ref[dr, hh] for dr, hh in chains]
        stbs = [st.astype(BF16) for st in sts]
        wss = [jnp.dot(piece(1, dr, hh), stb, preferred_element_type=F32) for (dr, hh), stb in zip(chains, stbs)]
        qss = [jnp.dot(piece(2, dr, hh), stb, preferred_element_type=F32) for (dr, hh), stb in zip(chains, stbs)]
        vnbs = [(piece(0, dr, hh) - ws).astype(BF16) for (dr, hh), ws in zip(chains, wss)]
        avs = [jnp.dot(piece(4, dr, hh), jnp.concatenate([vnb, vnb], axis=0), preferred_element_type=F32)
               for (dr, hh), vnb in zip(chains, vnbs)]
        upds = [_dot_tn(piece(3, dr, hh), vnb) for (dr, hh), vnb in zip(chains, vnbs)]
        for (dr, hh), st, upd in zip(chains, sts, upds):
            st_ref[dr, hh] = st * (fwd, bwd)[dr][5][hh, gls[dr], :] + upd
        for (dr, hh), qs, av in zip(chains, qss, avs):
            o_refs[dr][rows[dr], hh * GDN_DV:(hh + 1) * GDN_DV] = qs + av


def _gdn_scan(u, w, qd, kd, a, gl, s_lat, l_ctx):
    nb, nh, _, t, _ = u.shape
    tg = GDN_SCAN_ROWS
    assert l_ctx == tg and s_lat % tg == 0
    ngl = s_lat // tg
    grp = (lambda i: jnp.where(i == 0, ngl, i - 1),
           lambda i: jnp.where(i == 0, ngl, ngl - i))
    rows8 = (tg // GDN_CHUNK) * 8
    in_specs = []
    for dr in range(2):
        for wd in (128,) * 5:
            in_specs.append(pl.BlockSpec((None, nh, None, tg, wd), lambda b, i, dr=dr: (b, 0, dr, grp[dr](i), 0)))
        in_specs.append(pl.BlockSpec((None, nh, None, rows8, 128), lambda b, i, dr=dr: (b, 0, dr, grp[dr](i), 0)))
    return pl.pallas_call(
        _gdn_scan_kernel,
        grid=(nb, ngl + 1),
        in_specs=in_specs,
        out_specs=[pl.BlockSpec((None, tg, nh * GDN_DV), lambda b, i, dr=dr: (b, grp[dr](i), 0)) for dr in range(2)],
        out_shape=[jax.ShapeDtypeStruct((nb, t, nh * GDN_DV), F32)] * 2,
        scratch_shapes=[pltpu.VMEM((2, nh, GDN_DK, GDN_DV), F32)],
        compiler_params=_cparams(("arbitrary", "arbitrary")),
        name="gdn_scan",
    )(u, w, qd, kd, a, gl, u, w, qd, kd, a, gl)


def _head_norm_gate(o, gate, nw):
    y = o * lax.rsqrt(jnp.mean(o * o, axis=-1, keepdims=True) + NORM_EPS) * nw
    return y * _silu(gate)


def _outproj_kernel(na_ref, gla_ref, gg_ref, gdnf_ref, gdnb_ref, gz_ref, gnw_ref, dnw_ref, w_ref, h_ref, mod_ref,
                    o_ref, lhs_ref, *, s_lat, tm, nb):
    b, i, j = pl.program_id(0), pl.program_id(1), pl.program_id(2)

    @pl.when(j == 0)
    def _():
        lhs_ref[:, 0:NA_W] = na_ref[...].astype(BF16)
        for hh in range(GLA_HEADS):
            sl = slice(hh * GLA_DV, (hh + 1) * GLA_DV)
            y = _head_norm_gate(gla_ref[:, sl], gg_ref[:, sl], gnw_ref[...])
            lhs_ref[:, NA_W + hh * GLA_DV:NA_W + (hh + 1) * GLA_DV] = y.astype(BF16)
        base = NA_W + GLA_HEADS * GLA_DV
        for hh in range(GDN_HEADS):
            sl = slice(hh * GDN_DV, (hh + 1) * GDN_DV)
            y = _head_norm_gate(gdnf_ref[:, sl] + gdnb_ref[:, sl], gz_ref[:, sl], dnw_ref[...])
            lhs_ref[:, base + hh * GDN_DV:base + (hh + 1) * GDN_DV] = y.astype(BF16)

    mix = jnp.dot(lhs_ref[...], w_ref[...].astype(BF16), preferred_element_type=F32)
    row = i * tm + lax.broadcasted_iota(I32, (tm, 1), 0)
    gate = jnp.where(row < s_lat, mod_ref[b][2:3], mod_ref[nb][2:3])
    o_ref[...] = h_ref[...] + gate * mix


def _outproj(o_na, o_gla, o_gdn, p, gla_nw, gdn_nw, w_out, layer, h, mod, s_lat):
    nb, t, d = h.shape
    tm = t // 8
    tn = min(1024, d)
    mix_w = w_out.shape[1]
    gw = GLA_HEADS * GLA_DV
    rb = lambda w, c=0: pl.BlockSpec((None, tm, w), lambda b, i, j, c=c, w=w: (b, i, c // w))
    return pl.pallas_call(
        functools.partial(_outproj_kernel, s_lat=s_lat, tm=tm, nb=nb),
        grid=(nb, t // tm, d // tn),
        in_specs=[rb(NA_W), rb(gw), rb(gw, COL_GLA_G), rb(gw), rb(gw), rb(gw, COL_GDN_Z),
                  pl.BlockSpec((1, GLA_DV), lambda b, i, j: (0, 0)),
                  pl.BlockSpec((1, GDN_DV), lambda b, i, j: (0, 0)),
                  pl.BlockSpec((None, mix_w, tn), lambda b, i, j: (layer, 0, j)),
                  pl.BlockSpec((None, tm, tn), lambda b, i, j: (b, i, j)),
                  pl.BlockSpec((8, 6, tn), lambda b, i, j: (0, 0, j))],
        out_specs=pl.BlockSpec((None, tm, tn), lambda b, i, j: (b, i, j)),
        out_shape=jax.ShapeDtypeStruct((nb, t, d), F32),
        scratch_shapes=[pltpu.VMEM((tm, mix_w), BF16)],
        compiler_params=_cparams(("arbitrary", "arbitrary", "arbitrary")),
        name="outproj",
    )(o_na, o_gla, p, o_gdn[0], o_gdn[1], p, gla_nw, gdn_nw, w_out, h, mod)


def _router_kernel(h_ref, nw_ref, mod_ref, wr_ref, hn_ref, aff_ref, *, s_lat, tm, nb):
    b, i = pl.program_id(0), pl.program_id(1)
    hn = _mod_norm(h_ref[...], nw_ref[...], mod_ref[b], mod_ref[nb], i * tm, s_lat, 3)
    hn_ref[...] = hn.astype(BF16)
    logits = jnp.dot(hn, wr_ref[...], precision=HI, preferred_element_type=F32).T[0:N_EXPERTS]
    e = jnp.exp(logits - logits.max(axis=0, keepdims=True))
    aff_ref[...] = e / e.sum(axis=0, keepdims=True)


def _router(h, nw, mod, w_router, s_lat):
    nb, t, d = h.shape
    tm = 256
    w_pad = jnp.zeros((d, 128), F32).at[:, :N_EXPERTS].set(w_router)
    return pl.pallas_call(
        functools.partial(_router_kernel, s_lat=s_lat, tm=tm, nb=nb),
        grid=(nb, t // tm),
        in_specs=[pl.BlockSpec((None, tm, d), lambda b, i: (b, i, 0)),
                  pl.BlockSpec((1, d), lambda b, i: (0, 0)),
                  pl.BlockSpec((8, 6, d), lambda b, i: (0, 0, 0)),
                  pl.BlockSpec((d, 128), lambda b, i: (0, 0))],
        out_specs=[pl.BlockSpec((None, tm, d), lambda b, i: (b, i, 0)),
                   pl.BlockSpec((None, N_EXPERTS, tm), lambda b, i: (b, 0, i))],
        out_shape=[jax.ShapeDtypeStruct((nb, t, d), BF16),
                   jax.ShapeDtypeStruct((nb, N_EXPERTS, t), F32)],
        compiler_params=_cparams(("arbitrary", "arbitrary")),
        name="router",
    )(h, nw, mod, w_pad)


def _lane_prefix(x01, tri):
    n = x01.shape[1]
    lane = lax.broadcasted_iota(I32, (1, 128), 1)
    off = jnp.zeros((x01.shape[0], 1), F32)
    before = jnp.zeros((x01.shape[0], 128), F32)
    outs = []
    for blk in range(n // 128):
        before = jnp.where(lane == blk, off, before)
        cb = jnp.dot(x01[:, blk * 128:(blk + 1) * 128].astype(BF16), tri, preferred_element_type=F32) + off
        outs.append(cb)
        off = cb[:, 127:128]
    return jnp.concatenate(outs, axis=1), before


def _select_kernel(aff_ref, tri_ref, slot_ref, s0_ref, *, cap):
    nb, n_exp, n = aff_ref.shape
    aff = aff_ref[...].reshape(nb * n_exp, n)
    ne = aff.shape[0]

    def bisect(_, c):
        lo, hi = c
        m2 = 0.5 * (lo + hi)
        m1 = 0.5 * (lo + m2)
        m3 = 0.5 * (m2 + hi)
        ok1, ok2, ok3 = (jnp.sum(jnp.where(aff >= m, 1, 0), axis=1, keepdims=True) >= cap for m in (m1, m2, m3))
        new_lo = jnp.where(ok3, m3, jnp.where(ok2, m2, jnp.where(ok1, m1, lo)))
        new_hi = jnp.where(ok3, hi, jnp.where(ok2, m3, jnp.where(ok1, m2, m1)))
        return new_lo, new_hi

    lo, _ = lax.fori_loop(0, SELECT_STEPS // 2, bisect, (jnp.zeros((ne, 1), F32), jnp.full((ne, 1), 2.0, F32)))
    thr = jnp.min(jnp.where(aff >= lo, aff, 2.0), axis=1, keepdims=True)
    gt = aff > thr
    eq = aff == thr
    need = (cap - jnp.sum(jnp.where(gt, 1, 0), axis=1, keepdims=True)).astype(F32)
    eq_f = jnp.where(eq, 1.0, 0.0)
    eq_incl, _ = _lane_prefix(eq_f, tri_ref[...])
    sel = gt | (eq & ((eq_incl - eq_f) < need))
    sel_f = jnp.where(sel, 1.0, 0.0)
    pos_incl, before = _lane_prefix(sel_f, tri_ref[...])
    slot_ref[...] = jnp.where(sel, pos_incl - 1.0, -1.0).astype(I32).reshape(nb, n_exp, n)
    s0_ref[...] = before.astype(I32).reshape(nb, n_exp, 128)


def _select(aff, n, blk_idx, cap):
    nb = aff.shape[0]
    tri = jnp.asarray(np.arange(128)[:, None] <= np.arange(128)[None, :], BF16)
    return pl.pallas_call(
        functools.partial(_select_kernel, cap=cap),
        grid=(1,),
        in_specs=[pl.BlockSpec((nb, N_EXPERTS, n), lambda g: (0, 0, blk_idx)),
                  pl.BlockSpec((128, 128), lambda g: (0, 0))],
        out_specs=[pl.BlockSpec((nb, N_EXPERTS, n), lambda g: (0, 0, 0)),
                   pl.BlockSpec((nb, N_EXPERTS, 128), lambda g: (0, 0, 0))],
        out_shape=[jax.ShapeDtypeStruct((nb, N_EXPERTS, n), I32),
                   jax.ShapeDtypeStruct((nb, N_EXPERTS, 128), I32)],
        compiler_params=_cparams(("arbitrary",)),
        name="select",
    )(aff, tri)


SELECT_STEPS = 152
COMPACT_WIN = 128 + 16
COMPACT_WIN_SMALL = 48
EXPERT_ROW_CHUNK = 256
COMBINE_WIN_SMALL = 48
COMBINE_WIN = 256


def _window_starts(s0_ref, b, i, nblk, cap, w):
    starts, fits = [], None
    for e in range(N_EXPERTS):
        base = (b * N_EXPERTS + e) * 128
        start = jnp.minimum((s0_ref[base + i] // 16) * 16, cap - w)
        end = jnp.where(i + 1 < nblk, s0_ref[base + jnp.minimum(i + 1, nblk - 1)], cap)
        ok = end - start <= w
        fits = ok if fits is None else jnp.logical_and(fits, ok)
        starts.append(pl.multiple_of(start, 16))
    return starts, fits


def _compact_kernel(s0_ref, hn_ref, slot_ref, aff_ref, xs_ref, g_ref, *, cap):
    b, c, i = pl.program_id(0), pl.program_id(1), pl.program_id(2)
    nblk = pl.num_programs(2)

    @pl.when(i == 0)
    def _():
        xs_ref[...] = jnp.zeros(xs_ref.shape, BF16)

    @pl.when(jnp.logical_and(i == 0, c == 0))
    def _():
        g_ref[...] = jnp.zeros(g_ref.shape, F32)

    def scatter(w):
        starts, _ = _window_starts(s0_ref, b, i, nblk, cap, w)
        slots = slot_ref[...]
        row = lax.broadcasted_iota(I32, (w, 128), 0)
        hits = [row == (slots[e:e + 1] - starts[e]) for e in range(N_EXPERTS)]
        lhs = jnp.concatenate([jnp.where(hit, 1.0, 0.0).astype(BF16) for hit in hits], axis=0)
        res = jnp.dot(lhs, hn_ref[...], preferred_element_type=F32)
        for e in range(N_EXPERTS):
            cur = xs_ref[e, pl.ds(starts[e], w), :].astype(F32)
            xs_ref[e, pl.ds(starts[e], w), :] = (cur + res[e * w:(e + 1) * w]).astype(BF16)

        @pl.when(c == 0)
        def _():
            affs = aff_ref[...]
            for e in range(N_EXPERTS):
                gv = jnp.sum(jnp.where(hits[e], affs[e:e + 1], 0.0), axis=1, keepdims=True)
                g_ref[e, pl.ds(starts[e], w), :] += jnp.broadcast_to(gv, (w, 128))

    w_small, w_big = min(cap, COMPACT_WIN_SMALL), min(cap, COMPACT_WIN)
    if w_small == w_big:
        scatter(w_big)
    else:
        _, fits = _window_starts(s0_ref, b, i, nblk, cap, w_small)

        @pl.when(fits)
        def _():
            scatter(w_small)

        @pl.when(jnp.logical_not(fits))
        def _():
            scatter(w_big)


def _compact(hn, slot, aff, s0, n, cap, tok_blk0):
    nb, _, d = hn.shape
    nblk = n // 128
    dcols = d // 2 if d >= 256 else d
    grid_spec = pltpu.PrefetchScalarGridSpec(
        num_scalar_prefetch=1,
        grid=(nb, d // dcols, nblk),
        in_specs=[pl.BlockSpec((None, 128, dcols), lambda b, c, i, s: (b, tok_blk0 + i, c)),
                  pl.BlockSpec((None, N_EXPERTS, 128), lambda b, c, i, s: (b, 0, i)),
                  pl.BlockSpec((None, N_EXPERTS, 128), lambda b, c, i, s: (b, 0, tok_blk0 + i))],
        out_specs=[pl.BlockSpec((N_EXPERTS, cap, dcols), lambda b, c, i, s: (0, b, c)),
                   pl.BlockSpec((N_EXPERTS, cap, 128), lambda b, c, i, s: (0, b, 0))],
    )
    return pl.pallas_call(
        functools.partial(_compact_kernel, cap=cap),
        grid_spec=grid_spec,
        out_shape=[jax.ShapeDtypeStruct((N_EXPERTS, nb * cap, d), BF16),
                   jax.ShapeDtypeStruct((N_EXPERTS, nb * cap, 128), F32)],
        compiler_params=_cparams(("arbitrary", "arbitrary", "arbitrary")),
        name="compact",
    )(s0.reshape(-1), hn, slot, aff)


def _expert_kernel(*refs, n_in):
    xs = refs[0:n_in]
    gs = refs[n_in:2 * n_in]
    wg_ref, wu_ref, wd_ref = refs[2 * n_in:2 * n_in + 3]
    ys = refs[2 * n_in + 3:3 * n_in + 3]
    accs = refs[3 * n_in + 3:4 * n_in + 3]
    wgu_ref, wdb_ref = refs[4 * n_in + 3:]
    f = pl.program_id(1)
    last = pl.num_programs(1) - 1
    tf = wg_ref.shape[1]
    wgu_ref[:, 0:tf] = wg_ref[...].astype(BF16)
    wgu_ref[:, tf:2 * tf] = wu_ref[...].astype(BF16)
    wdb_ref[...] = wd_ref[...].astype(BF16)

    @pl.when(f == 0)
    def _():
        for acc_ref in accs:
            acc_ref[...] = jnp.zeros(acc_ref.shape, F32)

    chunks = []
    for x_ref, acc_ref in zip(xs, accs):
        m = x_ref.shape[0]
        rc = min(m, EXPERT_ROW_CHUNK)
        chunks += [(x_ref, acc_ref, r0, rc) for r0 in range(0, m, rc)]

    def up(chunk):
        x_ref, _, r0, rc = chunk
        return jnp.dot(x_ref[r0:r0 + rc, :], wgu_ref[...], preferred_element_type=F32)

    def down(chunk, au):
        _, acc_ref, r0, rc = chunk
        hid = (_silu(au[:, 0:tf]) * au[:, tf:2 * tf]).astype(BF16)
        acc_ref[r0:r0 + rc, :] += jnp.dot(hid, wdb_ref[...], preferred_element_type=F32)

    au_prev = up(chunks[0])
    for ci in range(1, len(chunks)):
        au_next = up(chunks[ci])
        down(chunks[ci - 1], au_prev)
        au_prev = au_next
    down(chunks[-1], au_prev)

    @pl.when(f == last)
    def _():
        for g_ref, y_ref, acc_ref in zip(gs, ys, accs):
            gate = g_ref[...]
            for cblk in range(acc_ref.shape[1] // 128):
                sl = slice(cblk * 128, (cblk + 1) * 128)
                y_ref[:, sl] = (acc_ref[:, sl] * gate).astype(BF16)


def _experts(xs_list, g_list, w_gate, w_up, w_down, layer):
    _, ne, d, ff = w_gate.shape
    tf = 512 if ff % 512 == 0 else 256
    n_in = len(xs_list)
    xspec = lambda m, w: pl.BlockSpec((None, m, w), lambda e, f: (e, 0, 0), pipeline_mode=pl.Buffered(1))
    return pl.pallas_call(
        functools.partial(_expert_kernel, n_in=n_in),
        grid=(ne, ff // tf),
        in_specs=([xspec(x.shape[1], d) for x in xs_list] + [xspec(g.shape[1], 128) for g in g_list]
                  + [pl.BlockSpec((None, None, d, tf), lambda e, f: (layer, e, 0, f)),
                     pl.BlockSpec((None, None, d, tf), lambda e, f: (layer, e, 0, f)),
                     pl.BlockSpec((None, None, tf, d), lambda e, f: (layer, e, f, 0))]),
        out_specs=[xspec(x.shape[1], d) for x in xs_list],
        out_shape=[jax.ShapeDtypeStruct(x.shape, BF16) for x in xs_list],
        scratch_shapes=([pltpu.VMEM((x.shape[1], d), F32) for x in xs_list]
                        + [pltpu.VMEM((d, 2 * tf), BF16), pltpu.VMEM((tf, d), BF16)]),
        compiler_params=_cparams(("arbitrary", "arbitrary")),
        name="experts",
    )(*xs_list, *g_list, w_gate, w_up, w_down)


def _combine_kernel(s0_ref, y_ref, slot_ref, expand_ref, lane_ref, h_ref, mod_ref, o_ref, *,
                    cap, win, win_small, mod_row_static):
    b, i = pl.program_id(0), pl.program_id(2)
    nblk = pl.num_programs(2)
    mrow = mod_ref[b] if mod_row_static is None else mod_ref[mod_row_static]

    def combine_per_expert(w):
        starts, _ = _window_starts(s0_ref, b, i, nblk, cap, w)
        slot_t = slot_ref[...]
        jj = lax.broadcasted_iota(I32, (1, w), 1)
        acc = None
        for e in range(N_EXPERTS):
            hit = (slot_t[:, e:e + 1] - starts[e]) == jj
            part = jnp.dot(jnp.where(hit, 1.0, 0.0).astype(BF16), y_ref[e, pl.ds(starts[e], w), :],
                           preferred_element_type=F32)
            acc = part if acc is None else acc + part
        o_ref[...] = h_ref[...] + mrow[5:6] * acc

    def combine_stacked(w):
        starts, _ = _window_starts(s0_ref, b, i, nblk, cap, w)
        lane_e, lane_j = lane_ref[0:1, :], lane_ref[1:2, :]
        slot_exp = _dot01_rhs(slot_ref[...].astype(F32), expand_ref[...])
        start_exp = jnp.zeros(lane_e.shape, I32)
        for e in range(N_EXPERTS):
            start_exp = jnp.where(lane_e == e, starts[e], start_exp)
        hit = (slot_exp.astype(I32) - start_exp) == lane_j
        ywin = jnp.concatenate([y_ref[e, pl.ds(starts[e], w), :] for e in range(N_EXPERTS)], axis=0)
        acc = jnp.dot(jnp.where(hit, 1.0, 0.0).astype(BF16), ywin, preferred_element_type=F32)
        o_ref[...] = h_ref[...] + mrow[5:6] * acc

    if win_small >= win:
        combine_per_expert(win)
    else:
        _, fits = _window_starts(s0_ref, b, i, nblk, cap, win_small)

        @pl.when(fits)
        def _():
            combine_stacked(win_small)

        @pl.when(jnp.logical_not(fits))
        def _():
            combine_per_expert(win)


def _combine(y, slot, s0, h, mod, n, cap, tok_blk0, is_ctx):
    nb, t, d = h.shape
    nblk = n // 128
    dcols = d // 2 if d >= 256 else d
    win = min(cap, COMBINE_WIN)
    win_small = min(cap, COMBINE_WIN_SMALL)
    slot_t = jnp.swapaxes(slot, 1, 2)
    kcols = N_EXPERTS * win_small
    col = np.arange(kcols)
    expand = jnp.asarray(np.arange(N_EXPERTS)[:, None] == (col // win_small)[None, :], BF16)
    lane_tab = jnp.asarray(np.stack([col // win_small, col % win_small]), I32)
    full = lambda shp: pl.BlockSpec(shp, lambda b, c, i, s, n=len(shp): (0,) * n)
    grid_spec = pltpu.PrefetchScalarGridSpec(
        num_scalar_prefetch=1,
        grid=(nb, d // dcols, nblk),
        in_specs=[pl.BlockSpec((N_EXPERTS, cap, dcols), lambda b, c, i, s: (0, b, c)),
                  pl.BlockSpec((None, 128, N_EXPERTS), lambda b, c, i, s: (b, i, 0)),
                  full((N_EXPERTS, kcols)), full((2, kcols)),
                  pl.BlockSpec((None, 128, dcols), lambda b, c, i, s: (b, tok_blk0 + i, c)),
                  pl.BlockSpec((8, 6, dcols), lambda b, c, i, s: (0, 0, c))],
        out_specs=pl.BlockSpec((None, 128, dcols), lambda b, c, i, s: (b, tok_blk0 + i, c)),
    )
    return pl.pallas_call(
        functools.partial(_combine_kernel, cap=cap, win=win, win_small=win_small,
                          mod_row_static=nb if is_ctx else None),
        grid_spec=grid_spec,
        out_shape=jax.ShapeDtypeStruct((nb, t, d), F32),
        input_output_aliases={5: 0},
        compiler_params=_cparams(("arbitrary", "arbitrary", "arbitrary")),
        name="combine",
    )(s0.reshape(-1), y, slot_t, expand, lane_tab, h, mod)


def _moe(h, nw, mod, w_router, w_gate, w_up, w_down, layer, s_lat, l_ctx, with_ctx):
    hn, aff = _router(h, nw, mod, w_router, s_lat)
    streams = [(s_lat, 0, 0, False)]
    if with_ctx:
        streams.append((l_ctx, s_lat // l_ctx, s_lat // 128, True))
    sel = []
    for n, blk_idx, tok_blk0, _ in streams:
        cap = EC_CAPACITY * n // N_EXPERTS
        slot, s0 = _select(aff, n, blk_idx, cap)
        xs, gates = _compact(hn, slot, aff, s0, n, cap, tok_blk0)
        sel.append((slot, s0, xs, gates, cap))
    ys = _experts([s[2] for s in sel], [s[3] for s in sel], w_gate, w_up, w_down, layer)
    for (n, _, tok_blk0, is_ctx), (slot, s0, _, _, cap), y in zip(streams, sel, ys):
        h = _combine(y, slot, s0, h, mod, n, cap, tok_blk0, is_ctx)
    return h


def _final_norm_kernel(h_ref, w_ref, o_ref):
    x = h_ref[...]
    o_ref[...] = x * lax.rsqrt(jnp.mean(x * x, axis=-1, keepdims=True) + NORM_EPS) * w_ref[...]


def _final_norm(h, w, s_lat):
    nb, _, d = h.shape
    tm = 512
    return pl.pallas_call(
        _final_norm_kernel,
        grid=(nb, s_lat // tm),
        in_specs=[pl.BlockSpec((None, tm, d), lambda b, i: (b, i, 0)),
                  pl.BlockSpec((1, d), lambda b, i: (0, 0))],
        out_specs=pl.BlockSpec((None, tm, d), lambda b, i: (b, i, 0)),
        out_shape=jax.ShapeDtypeStruct((nb, s_lat, d), F32),
        compiler_params=_cparams(("arbitrary", "arbitrary")),
        name="final_norm",
    )(h, w)


def _reorder_w_in(w_in):
    d = w_in.shape[0]
    lr0 = COL_GDN_Q
    qkv0 = lr0 + 2 * GLA_RANK
    ab0 = qkv0 + 3 * GDN_HEADS * GDN_DK + GDN_HEADS * GDN_DV
    n_in = ab0 + 4 * GDN_HEADS
    return jnp.concatenate([w_in[:, :lr0], w_in[:, qkv0:ab0], w_in[:, lr0:qkv0], w_in[:, ab0:n_in],
                            jnp.zeros((d, NP_COLS - n_in), w_in.dtype)], axis=1).astype(BF16)


def _token_mixers(h, nw, mod, w_in, w_out, layer, rpb, gate_w, gate_b, gla_nw, conv_w, a_log, dt_bias, gdn_nw,
                  rope, s_lat, l_ctx):
    p = _inproj(h, nw, mod, _reorder_w_in(w_in), s_lat)
    o_na = _na(p, _na_bias_table(rpb), s_lat, l_ctx)
    gw_p = jnp.zeros((2, 128, GLA_HEADS * GLA_DK), F32)
    for dr in range(2):
        gw_p = gw_p.at[dr, dr * GLA_RANK:(dr + 1) * GLA_RANK].set(gate_w[dr])
    qr, kr, cf, cb = _gla_prep(p, rope[0], rope[1], gw_p, gate_b[:, None, :], s_lat)
    o_gla = _gla_scan(qr, kr, cf, cb, p, s_lat, l_ctx)
    qn, kn, vn = (_gdn_conv(p, conv_w, s_lat, part) for part in range(3))
    o_gdn = _gdn_scan(*_gdn_chunk(qn, kn, vn, p, a_log, dt_bias), s_lat, l_ctx)
    return _outproj(o_na, o_gla, o_gdn, p, gla_nw[None, :], gdn_nw[None, :], w_out, layer, h, mod, s_lat)


def kernel(x, c, ctx, c_ctx, w_ada, b_ada, norm_mix_w, norm_ffn_w, w_in, w_out, na_rpb, gla_gate_w, gla_gate_b,
           gla_norm_w, gdn_conv_w, gdn_a_log, gdn_dt_bias, gdn_norm_w, w_router, w_exp_gate, w_exp_up,
           w_exp_down, final_norm_w):
    nb, s_lat, d = x.shape
    l_ctx = ctx.shape[1]
    depth = w_ada.shape[0]
    assert nb < 8 and s_lat % 256 == 0 and l_ctx == 256 and s_lat // GRID_W >= NA_WIN_R
    h = jnp.concatenate([x, ctx], axis=1)
    cvec = jnp.zeros((8, d), F32).at[:nb].set(c).at[nb].set(c_ctx)
    mods = _ada(cvec, w_ada, b_ada).reshape(depth, 8, 6, d)
    rope = _rope_tables(s_lat, l_ctx)
    w_out = w_out.astype(BF16)
    for l in range(depth):
        h = _token_mixers(h, norm_mix_w[l][None, :], mods[l], w_in[l], w_out, l, na_rpb[l], gla_gate_w[l],
                          gla_gate_b[l], gla_norm_w[l], gdn_conv_w[l], gdn_a_log[l], gdn_dt_bias[l],
                          gdn_norm_w[l], rope, s_lat, l_ctx)
        h = _moe(h, norm_ffn_w[l][None, :], mods[l], w_router[l], w_exp_gate, w_exp_up, w_exp_down, l,
                 s_lat, l_ctx, with_ctx=l < depth - 1)
    return _final_norm(h, final_norm_w[None, :], s_lat)
```

```python
import functools

import numpy as np
import jax
import jax.numpy as jnp
from jax import lax
from jax.experimental import pallas as pl
from jax.experimental.pallas import tpu as pltpu

F32 = jnp.float32
BF16 = jnp.bfloat16
I32 = jnp.int32
HI = lax.Precision.HIGHEST

GRID_W = 64
NA_HEADS, NA_DH, NA_WIN_R, NA_WIN_C = 16, 64, 8, 16
GLA_HEADS, GLA_DK, GLA_DV, GLA_RANK, GLA_TAU = 4, 64, 128, 16, 16.0
GLA_BLK = 16
GDN_HEADS, GDN_DK, GDN_DV, GDN_CONV, GDN_CHUNK = 4, 128, 128, 5, 64
N_EXPERTS, EC_CAPACITY = 16, 2
ROPE_BASE = 10000.0
NORM_EPS = 1e-6
NEG = -1e30

NA_W = NA_HEADS * NA_DH
COL_NA_Q, COL_NA_K, COL_NA_V = 0, NA_W, 2 * NA_W
COL_GLA_Q = 3 * NA_W
COL_GLA_K = COL_GLA_Q + GLA_HEADS * GLA_DK
COL_GLA_V = COL_GLA_K + GLA_HEADS * GLA_DK
COL_GLA_G = COL_GLA_V + GLA_HEADS * GLA_DV
COL_GDN_Q = COL_GLA_G + GLA_HEADS * GLA_DV
COL_GDN_K = COL_GDN_Q + GDN_HEADS * GDN_DK
COL_GDN_V = COL_GDN_K + GDN_HEADS * GDN_DK
COL_GDN_Z = COL_GDN_V + GDN_HEADS * GDN_DV
COL_SMALL = COL_GDN_Z + GDN_HEADS * GDN_DV
SMALL_A, SMALL_B = 2 * GLA_RANK, 2 * GLA_RANK + 2 * GDN_HEADS
INPROJ_TN = 1024
NP_COLS = 7168

VMEM_LIMIT = 56 * 1024 * 1024


def _cparams(sem):
    return pltpu.CompilerParams(dimension_semantics=sem, vmem_limit_bytes=VMEM_LIMIT)


def _sigmoid(x):
    return 1.0 / (1.0 + jnp.exp(-x))


def _silu(x):
    return x * _sigmoid(x)


def _softplus(x):
    return jnp.maximum(x, 0.0) + jnp.log(1.0 + jnp.exp(-jnp.abs(x)))


def _bdot(a, b):
    return jnp.dot(a.astype(BF16), b.astype(BF16), preferred_element_type=F32)


def _dot_nt(a, b):
    return lax.dot_general(a, b, (((1,), (1,)), ((), ())), preferred_element_type=F32)


def _dot_tn(a, b):
    return lax.dot_general(a, b, (((0,), (0,)), ((), ())), preferred_element_type=F32)


def _dot01_rhs(x, m01):
    hi = x.astype(BF16)
    r1 = x - hi.astype(F32)
    mid = r1.astype(BF16)
    lo = (r1 - mid.astype(F32)).astype(BF16)
    d = lambda p: jnp.dot(p, m01, preferred_element_type=F32)
    return d(hi) + d(mid) + d(lo)


def _dot01(m01, x):
    hi = x.astype(BF16)
    r1 = x - hi.astype(F32)
    mid = r1.astype(BF16)
    lo = (r1 - mid.astype(F32)).astype(BF16)
    d = lambda p: jnp.dot(m01, p, preferred_element_type=F32)
    return d(hi) + d(mid) + d(lo)


def _ada_kernel(c_ref, w_ref, b_ref, o_ref):
    o_ref[...] = _bdot(_silu(c_ref[...]), w_ref[...]) + b_ref[...]


def _ada(cvec, w_ada, b_ada):
    depth, d, n6 = w_ada.shape
    tn = 2048 if n6 % 2048 == 0 else 512
    assert n6 % tn == 0
    return pl.pallas_call(
        _ada_kernel,
        grid=(depth, n6 // tn),
        in_specs=[pl.BlockSpec((8, d), lambda l, j: (0, 0)),
                  pl.BlockSpec((None, d, tn), lambda l, j: (l, 0, j)),
                  pl.BlockSpec((None, 1, tn), lambda l, j: (l, 0, j))],
        out_specs=pl.BlockSpec((None, 8, tn), lambda l, j: (l, 0, j)),
        out_shape=jax.ShapeDtypeStruct((depth, 8, n6), F32),
        compiler_params=_cparams(("arbitrary", "arbitrary")),
        name="ada",
    )(cvec, w_ada, b_ada.reshape(depth, 1, n6))


def _mod_norm(x, nw, mod_b, mod_c, row0, s_lat, k_shift):
    ms = jnp.mean(x * x, axis=-1, keepdims=True)
    y = x * lax.rsqrt(ms + NORM_EPS) * nw
    row = row0 + lax.broadcasted_iota(I32, (x.shape[0], 1), 0)
    is_lat = row < s_lat
    shift = jnp.where(is_lat, mod_b[k_shift:k_shift + 1], mod_c[k_shift:k_shift + 1])
    scale = jnp.where(is_lat, mod_b[k_shift + 1:k_shift + 2], mod_c[k_shift + 1:k_shift + 2])
    return y * (1.0 + scale) + shift


def _inproj_kernel(x_ref, nw_ref, mod_ref, w_ref, o_ref, xn_ref, *, s_lat, tm, nb):
    b, i, j = pl.program_id(0), pl.program_id(1), pl.program_id(2)

    @pl.when(j == 0)
    def _():
        rc = tm // 4

        def chunk(ci, carry):
            r = pl.multiple_of(ci * rc, 16)
            xn = _mod_norm(x_ref[pl.ds(r, rc), :], nw_ref[...], mod_ref[b], mod_ref[nb], i * tm + r, s_lat, 0)
            xn_ref[pl.ds(r, rc), :] = xn.astype(BF16)
            return carry

        lax.fori_loop(0, 4, chunk, 0)

    o_ref[...] = jnp.dot(xn_ref[...], w_ref[...].astype(BF16), preferred_element_type=F32)


def _inproj(h, nw, mod, w_p, s_lat):
    nb, t, d = h.shape
    tm = t // 4
    tn = INPROJ_TN
    npc = w_p.shape[1]
    return pl.pallas_call(
        functools.partial(_inproj_kernel, s_lat=s_lat, tm=tm, nb=nb),
        grid=(nb, t // tm, npc // tn),
        in_specs=[pl.BlockSpec((None, tm, d), lambda b, i, j: (b, i, 0)),
                  pl.BlockSpec((1, d), lambda b, i, j: (0, 0)),
                  pl.BlockSpec((8, 6, d), lambda b, i, j: (0, 0, 0)),
                  pl.BlockSpec((d, tn), lambda b, i, j: (0, j))],
        out_specs=pl.BlockSpec((None, tm, tn), lambda b, i, j: (b, i, j)),
        out_shape=jax.ShapeDtypeStruct((nb, t, npc), F32),
        scratch_shapes=[pltpu.VMEM((tm, d), BF16)],
        compiler_params=_cparams(("arbitrary", "arbitrary", "arbitrary")),
        name="inproj",
    )(h, nw, mod, w_p)


NA_NDR = 2 * NA_WIN_R - 1
NA_NDC = 2 * NA_WIN_C - 1


def _na_bias_kernel(rpb_ref, onehot_ref, mask_ref, o_ref):
    t = jnp.dot(rpb_ref[...], onehot_ref[...], precision=HI, preferred_element_type=F32)
    o_ref[...] = jnp.where(mask_ref[...] > 0.0, t, NEG)


def _na_bias_table(rpb):
    nh = rpb.shape[0]
    cq = np.arange(GRID_W)
    dc = np.clip(cq[None, :] - cq[:, None] + NA_WIN_C - 1, 0, NA_NDC - 1)
    cstart = np.clip(cq - NA_WIN_C // 2, 0, GRID_W - NA_WIN_C)
    colmask = (cq[None, :] >= cstart[:, None]) & (cq[None, :] < cstart[:, None] + NA_WIN_C)
    onehot = jnp.asarray(np.arange(128)[:, None] == dc.reshape(1, -1), F32)
    mask = jnp.asarray(colmask.reshape(1, -1), F32)
    rpb_p = jnp.zeros((nh, 16, 128), F32).at[:, :NA_NDR, :NA_NDC].set(rpb)
    full = pl.pallas_call(
        _na_bias_kernel,
        grid=(nh,),
        in_specs=[pl.BlockSpec((None, 16, 128), lambda h: (h, 0, 0)),
                  pl.BlockSpec((128, GRID_W * GRID_W), lambda h: (0, 0)),
                  pl.BlockSpec((1, GRID_W * GRID_W), lambda h: (0, 0))],
        out_specs=pl.BlockSpec((None, 16, GRID_W * GRID_W), lambda h: (h, 0, 0)),
        out_shape=jax.ShapeDtypeStruct((nh, 16, GRID_W * GRID_W), F32),
        compiler_params=_cparams(("arbitrary",)),
        name="na_bias",
    )(rpb_p, onehot, mask)
    t15 = full[:, :NA_NDR].reshape(nh, NA_NDR, GRID_W, GRID_W)
    return jnp.concatenate([t15[:, :-1], t15[:, 1:]], axis=-1)


def _softmax_pv_many(s_lists, v_lists):
    n = len(s_lists)
    ms = []
    for sl in s_lists:
        m = sl[0].max(axis=-1, keepdims=True)
        for s in sl[1:]:
            m = jnp.maximum(m, s.max(axis=-1, keepdims=True))
        ms.append(m)
    ps = [[jnp.exp(s - ms[c]) for s in s_lists[c]] for c in range(n)]
    dens = []
    for c in range(n):
        den = ps[c][0].sum(axis=-1, keepdims=True)
        for p in ps[c][1:]:
            den = den + p.sum(axis=-1, keepdims=True)
        dens.append(den)
    outs = []
    for c in range(n):
        o = jnp.dot(ps[c][0].astype(BF16), v_lists[c][0], preferred_element_type=F32)
        for p, v in zip(ps[c][1:], v_lists[c][1:]):
            o = o + jnp.dot(p.astype(BF16), v, preferred_element_type=F32)
        outs.append(o)
    return [o / den for o, den in zip(outs, dens)]


NA_ROWS_PER_STEP = 8


def _na_kernel(q_ref, k_ref, v_ref, bias_ref, o_ref, kb_ref, vb_ref, *, s_lat, l_ctx):
    rows = s_lat // GRID_W
    nwin = NA_WIN_R * GRID_W
    scale = NA_DH ** -0.5
    kb_ref[...] = k_ref[...].astype(BF16)
    vb_ref[...] = v_ref[...].astype(BF16)
    lane = lax.broadcasted_iota(I32, (1, 2 * NA_DH), 1)
    head_lanes = (lane < NA_DH, lane >= NA_DH)
    kc = kb_ref[pl.ds(s_lat, l_ctx), :]
    vc = vb_ref[pl.ds(s_lat, l_ctx), :]

    qc = q_ref[pl.ds(s_lat, l_ctx), :] * scale
    qhs = [jnp.where(head_lanes[hh], qc, 0.0).astype(BF16) for hh in range(2)]
    outs = _softmax_pv_many([[_dot_nt(qh, kc)] for qh in qhs], [[vc], [vc]])
    o_ref[pl.ds(s_lat, l_ctx), :] = jnp.where(head_lanes[0], outs[0], outs[1])

    def body(it, carry):
        s_lists, v_lists, starts = [], [], []
        for rr in range(NA_ROWS_PER_STEP):
            r = it * NA_ROWS_PER_STEP + rr
            r0 = jnp.clip(r - NA_WIN_R // 2, 0, rows - NA_WIN_R)
            d = r - r0
            qs = pl.multiple_of(r * GRID_W, GRID_W)
            ks = pl.multiple_of(r0 * GRID_W, GRID_W)
            starts.append(qs)
            q = q_ref[pl.ds(qs, GRID_W), :] * scale
            kw = kb_ref[pl.ds(ks, nwin), :]
            vw = vb_ref[pl.ds(ks, nwin), :]
            q2 = jnp.concatenate([jnp.where(head_lanes[hh], q, 0.0) for hh in range(2)], axis=0).astype(BF16)
            bias = jnp.concatenate(
                [jnp.concatenate([bias_ref[hh, 2 * m + NA_WIN_R - 1 - d] for hh in range(2)], axis=0)
                 for m in range(NA_WIN_R // 2)], axis=1)
            s_lists.append([_dot_nt(q2, kw) + bias, _dot_nt(q2, kc)])
            v_lists.append([vw, vc])
        res = _softmax_pv_many(s_lists, v_lists)
        for rr in range(NA_ROWS_PER_STEP):
            o_ref[pl.ds(starts[rr], GRID_W), :] = jnp.where(head_lanes[0], res[rr][0:GRID_W], res[rr][GRID_W:2 * GRID_W])
        return carry

    lax.fori_loop(0, rows // NA_ROWS_PER_STEP, body, 0)


def _na(p, bias_tbl, s_lat, l_ctx):
    nb, t, _ = p.shape
    blk = lambda off: pl.BlockSpec((None, t, 128), lambda b, g, off=off: (b, 0, off // 128 + g))
    return pl.pallas_call(
        functools.partial(_na_kernel, s_lat=s_lat, l_ctx=l_ctx),
        grid=(nb, NA_HEADS // 2),
        in_specs=[blk(COL_NA_Q), blk(COL_NA_K), blk(COL_NA_V),
                  pl.BlockSpec((2, NA_NDR - 1, GRID_W, 2 * GRID_W), lambda b, g: (g, 0, 0, 0))],
        out_specs=pl.BlockSpec((None, t, 128), lambda b, g: (b, 0, g)),
        out_shape=jax.ShapeDtypeStruct((nb, t, NA_W), F32),
        scratch_shapes=[pltpu.VMEM((t, 128), BF16), pltpu.VMEM((t, 128), BF16)],
        compiler_params=_cparams(("arbitrary", "arbitrary")),
        name="na",
    )(p, p, p, bias_tbl)


def _rope_tables(s_lat, l_ctx):
    width = GLA_HEADS * GLA_DK
    nf = GLA_DK // 4
    pos = np.arange(s_lat)
    lane = np.arange(width)
    sub = lane % GLA_DK
    freqs = ROPE_BASE ** (-np.arange(nf, dtype=np.float64) / nf)
    p_sel = np.where((sub < GLA_DK // 2)[None, :], (pos // GRID_W)[:, None], (pos % GRID_W)[:, None])
    ang = p_sel.astype(np.float64) * freqs[sub % nf][None, :]
    sign = np.where((sub % (2 * nf)) < nf, -1.0, 1.0)
    cos = np.concatenate([np.cos(ang), np.ones((l_ctx, width))], axis=0)
    sin = np.concatenate([np.sin(ang) * sign[None, :], np.zeros((l_ctx, width))], axis=0)
    return jnp.asarray(cos, F32), jnp.asarray(sin, F32)


def _block_tri(n, blk):
    i = np.arange(n)
    same = (i[:, None] // blk) == (i[None, :] // blk)
    lower = same & (i[None, :] <= i[:, None])
    upper = same & (i[None, :] >= i[:, None])
    return jnp.asarray(lower, BF16), jnp.asarray(upper, BF16)


def _gla_prep_kernel(q_ref, k_ref, sm_ref, cos_ref, sin_ref, gw_ref, gb_ref, tl_ref, tu_ref,
                     qr_ref, kr_ref, cf_ref, cb_ref):
    width = GLA_HEADS * GLA_DK
    nf = GLA_DK // 4
    lane = lax.broadcasted_iota(I32, (1, width), 1)
    first = (lane % (2 * nf)) < nf
    cos, sin = cos_ref[...], sin_ref[...]

    def rope(x):
        swapped = jnp.where(first, pltpu.roll(x, width - nf, 1), pltpu.roll(x, nf, 1))
        return x * cos + swapped * sin

    qr_ref[...] = rope(q_ref[...]) * (GLA_DK ** -0.5)
    kr_ref[...] = rope(k_ref[...])
    sm = sm_ref[...]
    for dr, (tri_ref, out_ref) in enumerate(((tl_ref, cf_ref), (tu_ref, cb_ref))):
        z = jnp.dot(sm, gw_ref[dr], precision=HI, preferred_element_type=F32) + gb_ref[dr]
        log_a = (jnp.minimum(z, 0.0) - jnp.log(1.0 + jnp.exp(-jnp.abs(z)))) * (1.0 / GLA_TAU)
        out_ref[...] = _dot01(tri_ref[...], log_a)


def _gla_prep(p, cos, sin, gw_p, gb, s_lat):
    nb, t, _ = p.shape
    width = GLA_HEADS * GLA_DK
    tr = 256
    tl, tu = _block_tri(tr, GLA_BLK)
    row_blk = lambda w, off: pl.BlockSpec((None, tr, w), lambda b, i, off=off, w=w: (b, i, off // w))
    tab = pl.BlockSpec((tr, width), lambda b, i: (i, 0))
    full = lambda shp: pl.BlockSpec(shp, lambda b, i, n=len(shp): (0,) * n)
    out = pl.BlockSpec((None, tr, width), lambda b, i: (b, i, 0))
    return pl.pallas_call(
        _gla_prep_kernel,
        grid=(nb, t // tr),
        in_specs=[row_blk(width, COL_GLA_Q), row_blk(width, COL_GLA_K), row_blk(128, COL_SMALL),
                  tab, tab, full((2, 128, width)), full((2, 1, width)), full((tr, tr)), full((tr, tr))],
        out_specs=[out] * 4,
        out_shape=[jax.ShapeDtypeStruct((nb, t, width), F32)] * 4,
        compiler_params=_cparams(("arbitrary", "arbitrary")),
        name="gla_prep",
    )(p, p, p, cos, sin, gw_p, gb, tl, tu)


def _gla_scan_kernel(qr_ref, kr_ref, cf_ref, cb_ref, v_ref, r3_ref, o_ref, st_ref, *, s_lat, l_ctx):
    nlat, nctx = s_lat // GLA_BLK, l_ctx // GLA_BLK
    o_ref[...] = jnp.zeros(o_ref.shape, F32)
    st_ref[...] = jnp.zeros(st_ref.shape, F32)
    sub = lax.broadcasted_iota(I32, (GLA_BLK, 2 * GLA_DK), 0)
    bd = (lax.broadcasted_iota(I32, (2 * GLA_DV, 2 * GLA_DK), 0) // GLA_DV
          == lax.broadcasted_iota(I32, (2 * GLA_DV, 2 * GLA_DK), 1) // GLA_DK)

    def step(i, carry):
        in_ctx = i < nctx
        jf = jnp.where(in_ctx, nlat + i, i - nctx)
        jb = jnp.where(in_ctx, nlat + nctx - 1 - i, nlat - 1 - (i - nctx))
        dirs = (0, 1)
        rs = [pl.multiple_of(j * GLA_BLK, GLA_BLK) for j in (jf, jb)]
        qs = [qr_ref[pl.ds(r, GLA_BLK), :] for r in rs]
        ks = [kr_ref[pl.ds(r, GLA_BLK), :] for r in rs]
        cums = [c_ref[pl.ds(r, GLA_BLK), :] for c_ref, r in zip((cf_ref, cb_ref), rs)]
        vs = [v_ref[pl.ds(r, GLA_BLK), :] for r in rs]
        tots = [cums[0][GLA_BLK - 1:GLA_BLK], cums[1][0:1]]
        sts = [st_ref[dr] for dr in dirs]
        qds = [(qs[dr] * jnp.exp(cums[dr])).astype(BF16) for dr in dirs]
        kds = [(ks[dr] * jnp.exp(tots[dr] - cums[dr])).astype(BF16) for dr in dirs]
        o_states = [_dot_nt(qds[dr], sts[dr].astype(BF16)) for dr in dirs]
        upds = [_dot_tn(vs[dr].astype(BF16), kds[dr]) for dr in dirs]
        xs = []
        for dr in dirs:
            tiles = []
            for s in range(GLA_BLK):
                valid = (sub >= s) if dr == 0 else (sub <= s)
                w = jnp.exp(jnp.where(valid, cums[dr] - cums[dr][s:s + 1], NEG))
                tiles.append((w * qs[dr] * ks[dr][s:s + 1]).astype(BF16))
            xs.append(jnp.concatenate(tiles, axis=0))
        ress = [jnp.dot(x, r3_ref[...], preferred_element_type=F32) for x in xs]
        for dr in dirs:
            st_ref[dr] = sts[dr] * jnp.exp(tots[dr]) + jnp.where(bd, upds[dr], 0.0)
        for dr in dirs:
            o_diag = ress[dr][0:GLA_BLK] * vs[dr][0:1]
            for s in range(1, GLA_BLK):
                o_diag = o_diag + ress[dr][s * GLA_BLK:(s + 1) * GLA_BLK] * vs[dr][s:s + 1]
            o_ref[pl.ds(rs[dr], GLA_BLK), :] += o_states[dr] + o_diag
        return carry

    lax.fori_loop(0, nlat + nctx, step, 0, unroll=8)


def _gla_scan(qr, kr, cf, cb, p, s_lat, l_ctx):
    nb, t, _ = p.shape
    lanes = 2 * GLA_DK
    r3 = jnp.asarray((np.arange(lanes)[:, None] // GLA_DK) == (np.arange(2 * GLA_DV)[None, :] // GLA_DV), BF16)
    blk = pl.BlockSpec((None, t, lanes), lambda b, g: (b, 0, g))
    return pl.pallas_call(
        functools.partial(_gla_scan_kernel, s_lat=s_lat, l_ctx=l_ctx),
        grid=(nb, GLA_HEADS // 2),
        in_specs=[blk, blk, blk, blk,
                  pl.BlockSpec((None, t, 2 * GLA_DV), lambda b, g: (b, 0, COL_GLA_V // (2 * GLA_DV) + g)),
                  pl.BlockSpec((lanes, 2 * GLA_DV), lambda b, g: (0, 0))],
        out_specs=pl.BlockSpec((None, t, 2 * GLA_DV), lambda b, g: (b, 0, g)),
        out_shape=jax.ShapeDtypeStruct((nb, t, GLA_HEADS * GLA_DV), F32),
        scratch_shapes=[pltpu.VMEM((2, 2 * GLA_DV, lanes), F32)],
        compiler_params=_cparams(("arbitrary", "arbitrary")),
        name="gla_scan",
    )(qr, kr, cf, cb, p, r3)


def _gdn_conv_kernel(x_ref, w_ref, o_ref, *, s_lat):
    g = pl.program_id(1)
    t = x_ref.shape[0]
    x = x_ref[...]
    tpos = lax.broadcasted_iota(I32, (t, 1), 0)
    half = GDN_CONV // 2
    acc = x * w_ref[half:half + 1, :]
    for j in range(GDN_CONV):
        if j == half:
            continue
        dlt = j - half
        src = tpos + dlt
        ok = (src >= 0) & (src < t) & ((src < s_lat) == (tpos < s_lat))
        acc = acc + jnp.where(ok, pltpu.roll(x, (-dlt) % t, 0), 0.0) * w_ref[j:j + 1, :]
    y = _silu(acc)
    scale = jnp.where(g < GDN_HEADS, GDN_DK ** -0.5, 1.0)
    yn = y * (lax.rsqrt(jnp.sum(y * y, axis=-1, keepdims=True) + 1e-6) * scale)
    o_ref[...] = jnp.where(g < 2 * GDN_HEADS, yn, y)


def _gdn_conv(p, conv_w, s_lat):
    nb, t, _ = p.shape
    assert COL_GDN_K == COL_GDN_Q + GDN_HEADS * 128 and COL_GDN_V == COL_GDN_K + GDN_HEADS * 128
    return pl.pallas_call(
        functools.partial(_gdn_conv_kernel, s_lat=s_lat),
        grid=(nb, 3 * GDN_HEADS),
        in_specs=[pl.BlockSpec((None, t, 128), lambda b, g: (b, 0, COL_GDN_Q // 128 + g)),
                  pl.BlockSpec((GDN_CONV, 128), lambda b, g: (0, g))],
        out_specs=pl.BlockSpec((None, t, 128), lambda b, g: (b, 0, g)),
        out_shape=jax.ShapeDtypeStruct((nb, t, 3 * GDN_HEADS * 128), F32),
        compiler_params=_cparams(("arbitrary", "arbitrary")),
        name="gdn_conv",
    )(p, conv_w)


GDN_HEADS_PER_STEP = 4


def _gdn_chunk_kernel(q_ref, k_ref, v_ref, sm_ref, alog_ref, dtb_ref,
                      u_ref, w_ref, qd_ref, kd_ref, a_ref, gl_ref, *, tc):
    g = pl.program_id(1)
    cs = GDN_CHUNK
    nh = GDN_HEADS_PER_STEP
    lane = lax.broadcasted_iota(I32, (1, 128), 1)
    subl = lax.broadcasted_iota(I32, (128, 1), 0)
    ti = lax.broadcasted_iota(I32, (tc, tc), 0)
    ui = lax.broadcasted_iota(I32, (tc, tc), 1)
    same = (ti // cs) == (ui // cs)
    eye = jnp.where(ti == ui, 1.0, 0.0)
    sm = sm_ref[...]
    sm_t = sm.T
    qs = [q_ref[:, hh * 128:(hh + 1) * 128] for hh in range(nh)]
    ks = [k_ref[:, hh * 128:(hh + 1) * 128] for hh in range(nh)]
    vs = [v_ref[:, hh * 128:(hh + 1) * 128] for hh in range(nh)]
    kbs = [k.astype(BF16) for k in ks]
    kks = [_dot_nt(kb, kb) for kb in kbs]
    qks = [_dot_nt(q.astype(BF16), kb) for q, kb in zip(qs, kbs)]
    chains = [(hh, dr) for hh in range(nh) for dr in range(2)]
    ms, decays, betas, e_gcs = [], [], [], []
    for hh, dr in chains:
        head = g * nh + hh
        ca = SMALL_A + GDN_HEADS * dr + head
        cbeta = SMALL_B + GDN_HEADS * dr + head
        a_col = jnp.sum(jnp.where(lane == ca, sm, 0.0), axis=1, keepdims=True)
        b_col = jnp.sum(jnp.where(lane == cbeta, sm, 0.0), axis=1, keepdims=True)
        a_row = jnp.sum(jnp.where(subl == ca, sm_t, 0.0), axis=0, keepdims=True)
        neg_rate = -jnp.exp(alog_ref[dr, hh])
        g_col = neg_rate * _softplus(a_col + dtb_ref[dr, hh])
        g_row = neg_rate * _softplus(a_row + dtb_ref[dr, hh])
        beta = _sigmoid(b_col)
        incl = same & ((ui <= ti) if dr == 0 else (ui >= ti))
        strict = same & ((ui < ti) if dr == 0 else (ui > ti))
        incl_t = same & ((ti <= ui) if dr == 0 else (ti >= ui))
        gc_col = jnp.sum(jnp.where(incl, g_row, 0.0), axis=1, keepdims=True)
        gc_row = jnp.sum(jnp.where(incl_t, g_col, 0.0), axis=0, keepdims=True)
        gc_tot = jnp.sum(jnp.where(same, g_row, 0.0), axis=1, keepdims=True)
        decay = jnp.exp(jnp.where(incl, gc_col - gc_row, NEG))
        ms.append(jnp.where(strict, beta * kks[hh] * decay, 0.0))
        decays.append(decay)
        betas.append(beta)
        e_gcs.append(jnp.exp(gc_col))
        kd_ref[hh, dr] = (ks[hh] * jnp.exp(gc_tot - gc_col)).astype(BF16)
        for c in range(tc // cs):
            gl_ref[hh, dr, c * 8:(c + 1) * 8, :] = jnp.broadcast_to(jnp.exp(gc_tot[c * cs:c * cs + 1]), (8, 128))
    invs = [eye - m for m in ms]
    mks = [_bdot(m, m) for m in ms]
    for lvl in range(5):
        invs = [inv + _bdot(inv, mk) for inv, mk in zip(invs, mks)]
        if lvl < 4:
            mks = [_bdot(mk, mk) for mk in mks]
    sols = [_bdot(invs[ci], jnp.concatenate([vs[hh] * betas[ci], ks[hh] * (betas[ci] * e_gcs[ci])], axis=1))
            for ci, (hh, dr) in enumerate(chains)]
    for ci, (hh, dr) in enumerate(chains):
        u_ref[hh, dr] = sols[ci][:, :GDN_DV]
        w_ref[hh, dr] = sols[ci][:, GDN_DV:].astype(BF16)
        qd_ref[hh, dr] = (qs[hh] * e_gcs[ci]).astype(BF16)
        aqk = qks[hh] * decays[ci]
        for c in range(tc // cs):
            blk = aqk[c * cs:(c + 1) * cs]
            folded = blk[:, 0:128]
            for piece in range(1, tc // 128):
                folded = folded + blk[:, piece * 128:(piece + 1) * 128]
            a_ref[hh, dr, c * cs:(c + 1) * cs, :] = folded.astype(BF16)


def _gdn_chunk(qkv, p, a_log, dt_bias):
    nb, t, _ = p.shape
    tc = 256
    nh = GDN_HEADS_PER_STEP
    nch = t // GDN_CHUNK
    ngrp = GDN_HEADS // nh
    rb = lambda part: pl.BlockSpec((None, tc, nh * 128), lambda b, g, i, part=part: (b, i, part * ngrp + g))
    par = pl.BlockSpec((2, nh, 1, 1), lambda b, g, i: (0, g, 0, 0))
    ob = lambda w: pl.BlockSpec((None, nh, 2, tc, w), lambda b, g, i: (b, g, 0, i, 0))
    shp = lambda w, dt: jax.ShapeDtypeStruct((nb, GDN_HEADS, 2, t, w), dt)
    return pl.pallas_call(
        functools.partial(_gdn_chunk_kernel, tc=tc),
        grid=(nb, GDN_HEADS // nh, t // tc),
        in_specs=[rb(0), rb(1), rb(2),
                  pl.BlockSpec((None, tc, 128), lambda b, g, i: (b, i, COL_SMALL // 128)), par, par],
        out_specs=[ob(128), ob(128), ob(128), ob(128), ob(128),
                   pl.BlockSpec((None, nh, 2, (tc // GDN_CHUNK) * 8, 128), lambda b, g, i: (b, g, 0, i, 0))],
        out_shape=[shp(128, F32), shp(128, BF16), shp(128, BF16), shp(128, BF16), shp(128, BF16),
                   jax.ShapeDtypeStruct((nb, GDN_HEADS, 2, nch * 8, 128), F32)],
        compiler_params=_cparams(("arbitrary", "arbitrary", "arbitrary")),
        name="gdn_chunk",
    )(qkv, qkv, qkv, p, a_log.reshape(2, GDN_HEADS, 1, 1).astype(F32), dt_bias.reshape(2, GDN_HEADS, 1, 1).astype(F32))


GDN_SCAN_ROWS = 256


def _gdn_scan_kernel(*refs):
    fwd, bwd = refs[0:6], refs[6:12]
    o_refs = refs[12:14]
    st_ref = refs[14]
    cs = GDN_CHUNK
    ncg = GDN_SCAN_ROWS // cs
    chains = [(dr, hh) for dr in range(2) for hh in range(GDN_HEADS)]

    @pl.when(pl.program_id(1) == 0)
    def _():
        st_ref[...] = jnp.zeros(st_ref.shape, F32)

    for ci in range(ncg):
        rows = [slice(ci * cs, (ci + 1) * cs), slice((ncg - 1 - ci) * cs, (ncg - ci) * cs)]
        gls = [slice(ci * 8, ci * 8 + 1), slice((ncg - 1 - ci) * 8, (ncg - 1 - ci) * 8 + 1)]

        def piece(k, dr, hh):
            return (fwd, bwd)[dr][k][hh, rows[dr], :]

        sts = [st_ref[dr, hh] for dr, hh in chains]
        stbs = [st.astype(BF16) for st in sts]
        wss = [jnp.dot(piece(1, dr, hh), stb, preferred_element_type=F32) for (dr, hh), stb in zip(chains, stbs)]
        qss = [jnp.dot(piece(2, dr, hh), stb, preferred_element_type=F32) for (dr, hh), stb in zip(chains, stbs)]
        vnbs = [(piece(0, dr, hh) - ws).astype(BF16) for (dr, hh), ws in zip(chains, wss)]
        avs = [jnp.dot(piece(4, dr, hh), jnp.concatenate([vnb, vnb], axis=0), preferred_element_type=F32)
               for (dr, hh), vnb in zip(chains, vnbs)]
        upds = [_dot_tn(piece(3, dr, hh), vnb) for (dr, hh), vnb in zip(chains, vnbs)]
        for (dr, hh), st, upd in zip(chains, sts, upds):
            st_ref[dr, hh] = st * (fwd, bwd)[dr][5][hh, gls[dr], :] + upd
        for (dr, hh), qs, av in zip(chains, qss, avs):
            o_refs[dr][rows[dr], hh * GDN_DV:(hh + 1) * GDN_DV] = qs + av


def _gdn_scan(u, w, qd, kd, a, gl, s_lat, l_ctx):
    nb, nh, _, t, _ = u.shape
    tg = GDN_SCAN_ROWS
    assert l_ctx == tg and s_lat % tg == 0
    ngl = s_lat // tg
    grp = (lambda i: jnp.where(i == 0, ngl, i - 1),
           lambda i: jnp.where(i == 0, ngl, ngl - i))
    rows8 = (tg // GDN_CHUNK) * 8
    in_specs = []
    for dr in range(2):
        for wd in (128,) * 5:
            in_specs.append(pl.BlockSpec((None, nh, None, tg, wd), lambda b, i, dr=dr: (b, 0, dr, grp[dr](i), 0)))
        in_specs.append(pl.BlockSpec((None, nh, None, rows8, 128), lambda b, i, dr=dr: (b, 0, dr, grp[dr](i), 0)))
    return pl.pallas_call(
        _gdn_scan_kernel,
        grid=(nb, ngl + 1),
        in_specs=in_specs,
        out_specs=[pl.BlockSpec((None, tg, nh * GDN_DV), lambda b, i, dr=dr: (b, grp[dr](i), 0)) for dr in range(2)],
        out_shape=[jax.ShapeDtypeStruct((nb, t, nh * GDN_DV), F32)] * 2,
        scratch_shapes=[pltpu.VMEM((2, nh, GDN_DK, GDN_DV), F32)],
        compiler_params=_cparams(("arbitrary", "arbitrary")),
        name="gdn_scan",
    )(u, w, qd, kd, a, gl, u, w, qd, kd, a, gl)


def _head_norm_gate(o, gate, nw):
    y = o * lax.rsqrt(jnp.mean(o * o, axis=-1, keepdims=True) + NORM_EPS) * nw
    return y * _silu(gate)


def _outproj_kernel(na_ref, gla_ref, gg_ref, gdnf_ref, gdnb_ref, gz_ref, gnw_ref, dnw_ref, w_ref, h_ref, mod_ref,
                    o_ref, lhs_ref, *, s_lat, tm, nb):
    b, i, j = pl.program_id(0), pl.program_id(1), pl.program_id(2)

    @pl.when(j == 0)
    def _():
        lhs_ref[:, 0:NA_W] = na_ref[...].astype(BF16)
        for hh in range(GLA_HEADS):
            sl = slice(hh * GLA_DV, (hh + 1) * GLA_DV)
            y = _head_norm_gate(gla_ref[:, sl], gg_ref[:, sl], gnw_ref[...])
            lhs_ref[:, NA_W + hh * GLA_DV:NA_W + (hh + 1) * GLA_DV] = y.astype(BF16)
        base = NA_W + GLA_HEADS * GLA_DV
        for hh in range(GDN_HEADS):
            sl = slice(hh * GDN_DV, (hh + 1) * GDN_DV)
            y = _head_norm_gate(gdnf_ref[:, sl] + gdnb_ref[:, sl], gz_ref[:, sl], dnw_ref[...])
            lhs_ref[:, base + hh * GDN_DV:base + (hh + 1) * GDN_DV] = y.astype(BF16)

    mix = jnp.dot(lhs_ref[...], w_ref[...].astype(BF16), preferred_element_type=F32)
    row = i * tm + lax.broadcasted_iota(I32, (tm, 1), 0)
    gate = jnp.where(row < s_lat, mod_ref[b][2:3], mod_ref[nb][2:3])
    o_ref[...] = h_ref[...] + gate * mix


def _outproj(o_na, o_gla, o_gdn, p, gla_nw, gdn_nw, w_out, layer, h, mod, s_lat):
    nb, t, d = h.shape
    tm = t // 8
    tn = min(1024, d)
    mix_w = w_out.shape[1]
    gw = GLA_HEADS * GLA_DV
    rb = lambda w, c=0: pl.BlockSpec((None, tm, w), lambda b, i, j, c=c, w=w: (b, i, c // w))
    return pl.pallas_call(
        functools.partial(_outproj_kernel, s_lat=s_lat, tm=tm, nb=nb),
        grid=(nb, t // tm, d // tn),
        in_specs=[rb(NA_W), rb(gw), rb(gw, COL_GLA_G), rb(gw), rb(gw), rb(gw, COL_GDN_Z),
                  pl.BlockSpec((1, GLA_DV), lambda b, i, j: (0, 0)),
                  pl.BlockSpec((1, GDN_DV), lambda b, i, j: (0, 0)),
                  pl.BlockSpec((None, mix_w, tn), lambda b, i, j: (layer, 0, j)),
                  pl.BlockSpec((None, tm, tn), lambda b, i, j: (b, i, j)),
                  pl.BlockSpec((8, 6, tn), lambda b, i, j: (0, 0, j))],
        out_specs=pl.BlockSpec((None, tm, tn), lambda b, i, j: (b, i, j)),
        out_shape=jax.ShapeDtypeStruct((nb, t, d), F32),
        scratch_shapes=[pltpu.VMEM((tm, mix_w), BF16)],
        compiler_params=_cparams(("arbitrary", "arbitrary", "arbitrary")),
        name="outproj",
    )(o_na, o_gla, p, o_gdn[0], o_gdn[1], p, gla_nw, gdn_nw, w_out, h, mod)


def _router_kernel(h_ref, nw_ref, mod_ref, wr_ref, hn_ref, aff_ref, *, s_lat, tm, nb):
    b, i = pl.program_id(0), pl.program_id(1)
    hn = _mod_norm(h_ref[...], nw_ref[...], mod_ref[b], mod_ref[nb], i * tm, s_lat, 3)
    hn_ref[...] = hn.astype(BF16)
    logits = jnp.dot(hn, wr_ref[...], precision=HI, preferred_element_type=F32).T[0:N_EXPERTS]
    e = jnp.exp(logits - logits.max(axis=0, keepdims=True))
    aff_ref[...] = e / e.sum(axis=0, keepdims=True)


def _router(h, nw, mod, w_router, s_lat):
    nb, t, d = h.shape
    tm = 256
    w_pad = jnp.zeros((d, 128), F32).at[:, :N_EXPERTS].set(w_router)
    return pl.pallas_call(
        functools.partial(_router_kernel, s_lat=s_lat, tm=tm, nb=nb),
        grid=(nb, t // tm),
        in_specs=[pl.BlockSpec((None, tm, d), lambda b, i: (b, i, 0)),
                  pl.BlockSpec((1, d), lambda b, i: (0, 0)),
                  pl.BlockSpec((8, 6, d), lambda b, i: (0, 0, 0)),
                  pl.BlockSpec((d, 128), lambda b, i: (0, 0))],
        out_specs=[pl.BlockSpec((None, tm, d), lambda b, i: (b, i, 0)),
                   pl.BlockSpec((None, N_EXPERTS, tm), lambda b, i: (b, 0, i))],
        out_shape=[jax.ShapeDtypeStruct((nb, t, d), BF16),
                   jax.ShapeDtypeStruct((nb, N_EXPERTS, t), F32)],
        compiler_params=_cparams(("arbitrary", "arbitrary")),
        name="router",
    )(h, nw, mod, w_pad)


def _lane_prefix(x01, tri):
    n = x01.shape[1]
    lane = lax.broadcasted_iota(I32, (1, 128), 1)
    off = jnp.zeros((x01.shape[0], 1), F32)
    before = jnp.zeros((x01.shape[0], 128), F32)
    outs = []
    for blk in range(n // 128):
        before = jnp.where(lane == blk, off, before)
        cb = jnp.dot(x01[:, blk * 128:(blk + 1) * 128].astype(BF16), tri, preferred_element_type=F32) + off
        outs.append(cb)
        off = cb[:, 127:128]
    return jnp.concatenate(outs, axis=1), before


def _select_kernel(aff_ref, tri_ref, slot_ref, s0_ref, *, cap):
    nb, n_exp, n = aff_ref.shape
    aff = aff_ref[...].reshape(nb * n_exp, n)
    ne = aff.shape[0]

    def bisect(_, c):
        lo, hi = c
        m2 = 0.5 * (lo + hi)
        m1 = 0.5 * (lo + m2)
        m3 = 0.5 * (m2 + hi)
        ok1, ok2, ok3 = (jnp.sum(jnp.where(aff >= m, 1, 0), axis=1, keepdims=True) >= cap for m in (m1, m2, m3))
        new_lo = jnp.where(ok3, m3, jnp.where(ok2, m2, jnp.where(ok1, m1, lo)))
        new_hi = jnp.where(ok3, hi, jnp.where(ok2, m3, jnp.where(ok1, m2, m1)))
        return new_lo, new_hi

    lo, _ = lax.fori_loop(0, SELECT_STEPS // 2, bisect, (jnp.zeros((ne, 1), F32), jnp.full((ne, 1), 2.0, F32)))
    thr = jnp.min(jnp.where(aff >= lo, aff, 2.0), axis=1, keepdims=True)
    gt = aff > thr
    eq = aff == thr
    need = (cap - jnp.sum(jnp.where(gt, 1, 0), axis=1, keepdims=True)).astype(F32)
    eq_f = jnp.where(eq, 1.0, 0.0)
    eq_incl, _ = _lane_prefix(eq_f, tri_ref[...])
    sel = gt | (eq & ((eq_incl - eq_f) < need))
    sel_f = jnp.where(sel, 1.0, 0.0)
    pos_incl, before = _lane_prefix(sel_f, tri_ref[...])
    slot_ref[...] = jnp.where(sel, pos_incl - 1.0, -1.0).astype(I32).reshape(nb, n_exp, n)
    s0_ref[...] = before.astype(I32).reshape(nb, n_exp, 128)


def _select(aff, n, blk_idx, cap):
    nb = aff.shape[0]
    tri = jnp.asarray(np.arange(128)[:, None] <= np.arange(128)[None, :], BF16)
    return pl.pallas_call(
        functools.partial(_select_kernel, cap=cap),
        grid=(1,),
        in_specs=[pl.BlockSpec((nb, N_EXPERTS, n), lambda g: (0, 0, blk_idx)),
                  pl.BlockSpec((128, 128), lambda g: (0, 0))],
        out_specs=[pl.BlockSpec((nb, N_EXPERTS, n), lambda g: (0, 0, 0)),
                   pl.BlockSpec((nb, N_EXPERTS, 128), lambda g: (0, 0, 0))],
        out_shape=[jax.ShapeDtypeStruct((nb, N_EXPERTS, n), I32),
                   jax.ShapeDtypeStruct((nb, N_EXPERTS, 128), I32)],
        compiler_params=_cparams(("arbitrary",)),
        name="select",
    )(aff, tri)


SELECT_STEPS = 152
COMPACT_WIN = 128 + 16
COMPACT_WIN_SMALL = 48
EXPERT_ROW_CHUNK = 256
COMBINE_WIN_SMALL = 48
COMBINE_WIN = 256


def _window_starts(s0_ref, b, i, nblk, cap, w):
    starts, fits = [], None
    for e in range(N_EXPERTS):
        base = (b * N_EXPERTS + e) * 128
        start = jnp.minimum((s0_ref[base + i] // 16) * 16, cap - w)
        end = jnp.where(i + 1 < nblk, s0_ref[base + jnp.minimum(i + 1, nblk - 1)], cap)
        ok = end - start <= w
        fits = ok if fits is None else jnp.logical_and(fits, ok)
        starts.append(pl.multiple_of(start, 16))
    return starts, fits


def _compact_kernel(s0_ref, hn_ref, slot_ref, aff_ref, xs_ref, g_ref, *, cap):
    b, c, i = pl.program_id(0), pl.program_id(1), pl.program_id(2)
    nblk = pl.num_programs(2)

    @pl.when(i == 0)
    def _():
        xs_ref[...] = jnp.zeros(xs_ref.shape, BF16)

    @pl.when(jnp.logical_and(i == 0, c == 0))
    def _():
        g_ref[...] = jnp.zeros(g_ref.shape, F32)

    def scatter(w):
        starts, _ = _window_starts(s0_ref, b, i, nblk, cap, w)
        slots = slot_ref[...]
        row = lax.broadcasted_iota(I32, (w, 128), 0)
        hits = [row == (slots[e:e + 1] - starts[e]) for e in range(N_EXPERTS)]
        lhs = jnp.concatenate([jnp.where(hit, 1.0, 0.0).astype(BF16) for hit in hits], axis=0)
        res = jnp.dot(lhs, hn_ref[...], preferred_element_type=F32)
        for e in range(N_EXPERTS):
            cur = xs_ref[e, pl.ds(starts[e], w), :].astype(F32)
            xs_ref[e, pl.ds(starts[e], w), :] = (cur + res[e * w:(e + 1) * w]).astype(BF16)

        @pl.when(c == 0)
        def _():
            affs = aff_ref[...]
            for e in range(N_EXPERTS):
                gv = jnp.sum(jnp.where(hits[e], affs[e:e + 1], 0.0), axis=1, keepdims=True)
                g_ref[e, pl.ds(starts[e], w), :] += jnp.broadcast_to(gv, (w, 128))

    w_small, w_big = min(cap, COMPACT_WIN_SMALL), min(cap, COMPACT_WIN)
    if w_small == w_big:
        scatter(w_big)
    else:
        _, fits = _window_starts(s0_ref, b, i, nblk, cap, w_small)

        @pl.when(fits)
        def _():
            scatter(w_small)

        @pl.when(jnp.logical_not(fits))
        def _():
            scatter(w_big)


def _compact(hn, slot, aff, s0, n, cap, tok_blk0):
    nb, _, d = hn.shape
    nblk = n // 128
    dcols = d // 2 if d >= 256 else d
    grid_spec = pltpu.PrefetchScalarGridSpec(
        num_scalar_prefetch=1,
        grid=(nb, d // dcols, nblk),
        in_specs=[pl.BlockSpec((None, 128, dcols), lambda b, c, i, s: (b, tok_blk0 + i, c)),
                  pl.BlockSpec((None, N_EXPERTS, 128), lambda b, c, i, s: (b, 0, i)),
                  pl.BlockSpec((None, N_EXPERTS, 128), lambda b, c, i, s: (b, 0, tok_blk0 + i))],
        out_specs=[pl.BlockSpec((N_EXPERTS, cap, dcols), lambda b, c, i, s: (0, b, c)),
                   pl.BlockSpec((N_EXPERTS, cap, 128), lambda b, c, i, s: (0, b, 0))],
    )
    return pl.pallas_call(
        functools.partial(_compact_kernel, cap=cap),
        grid_spec=grid_spec,
        out_shape=[jax.ShapeDtypeStruct((N_EXPERTS, nb * cap, d), BF16),
                   jax.ShapeDtypeStruct((N_EXPERTS, nb * cap, 128), F32)],
        compiler_params=_cparams(("arbitrary", "arbitrary", "arbitrary")),
        name="compact",
    )(s0.reshape(-1), hn, slot, aff)


def _expert_kernel(*refs, n_in):
    xs = refs[0:n_in]
    gs = refs[n_in:2 * n_in]
    wg_ref, wu_ref, wd_ref = refs[2 * n_in:2 * n_in + 3]
    ys = refs[2 * n_in + 3:3 * n_in + 3]
    accs = refs[3 * n_in + 3:4 * n_in + 3]
    wgu_ref, wdb_ref = refs[4 * n_in + 3:]
    f = pl.program_id(1)
    last = pl.num_programs(1) - 1
    tf = wg_ref.shape[1]
    wgu_ref[:, 0:tf] = wg_ref[...].astype(BF16)
    wgu_ref[:, tf:2 * tf] = wu_ref[...].astype(BF16)
    wdb_ref[...] = wd_ref[...].astype(BF16)

    @pl.when(f == 0)
    def _():
        for acc_ref in accs:
            acc_ref[...] = jnp.zeros(acc_ref.shape, F32)

    chunks = []
    for x_ref, acc_ref in zip(xs, accs):
        m = x_ref.shape[0]
        rc = min(m, EXPERT_ROW_CHUNK)
        chunks += [(x_ref, acc_ref, r0, rc) for r0 in range(0, m, rc)]

    def up(chunk):
        x_ref, _, r0, rc = chunk
        return jnp.dot(x_ref[r0:r0 + rc, :], wgu_ref[...], preferred_element_type=F32)

    def down(chunk, au):
        _, acc_ref, r0, rc = chunk
        hid = (_silu(au[:, 0:tf]) * au[:, tf:2 * tf]).astype(BF16)
        acc_ref[r0:r0 + rc, :] += jnp.dot(hid, wdb_ref[...], preferred_element_type=F32)

    au_prev = up(chunks[0])
    for ci in range(1, len(chunks)):
        au_next = up(chunks[ci])
        down(chunks[ci - 1], au_prev)
        au_prev = au_next
    down(chunks[-1], au_prev)

    @pl.when(f == last)
    def _():
        for g_ref, y_ref, acc_ref in zip(gs, ys, accs):
            gate = g_ref[...]
            for cblk in range(acc_ref.shape[1] // 128):
                sl = slice(cblk * 128, (cblk + 1) * 128)
                y_ref[:, sl] = (acc_ref[:, sl] * gate).astype(BF16)


def _experts(xs_list, g_list, w_gate, w_up, w_down, layer):
    _, ne, d, ff = w_gate.shape
    tf = 256
    n_in = len(xs_list)
    xspec = lambda m, w: pl.BlockSpec((None, m, w), lambda e, f: (e, 0, 0))
    return pl.pallas_call(
        functools.partial(_expert_kernel, n_in=n_in),
        grid=(ne, ff // tf),
        in_specs=([xspec(x.shape[1], d) for x in xs_list] + [xspec(g.shape[1], 128) for g in g_list]
                  + [pl.BlockSpec((None, None, d, tf), lambda e, f: (layer, e, 0, f)),
                     pl.BlockSpec((None, None, d, tf), lambda e, f: (layer, e, 0, f)),
                     pl.BlockSpec((None, None, tf, d), lambda e, f: (layer, e, f, 0))]),
        out_specs=[xspec(x.shape[1], d) for x in xs_list],
        out_shape=[jax.ShapeDtypeStruct(x.shape, BF16) for x in xs_list],
        scratch_shapes=([pltpu.VMEM((x.shape[1], d), F32) for x in xs_list]
                        + [pltpu.VMEM((d, 2 * tf), BF16), pltpu.VMEM((tf, d), BF16)]),
        compiler_params=_cparams(("arbitrary", "arbitrary")),
        name="experts",
    )(*xs_list, *g_list, w_gate, w_up, w_down)


def _combine_kernel(s0_ref, y_ref, slot_ref, expand_ref, lane_ref, h_ref, mod_ref, o_ref, *,
                    cap, win, win_small, mod_row_static):
    b, i = pl.program_id(0), pl.program_id(2)
    nblk = pl.num_programs(2)
    mrow = mod_ref[b] if mod_row_static is None else mod_ref[mod_row_static]

    def combine_per_expert(w):
        starts, _ = _window_starts(s0_ref, b, i, nblk, cap, w)
        slot_t = slot_ref[...]
        jj = lax.broadcasted_iota(I32, (1, w), 1)
        acc = None
        for e in range(N_EXPERTS):
            hit = (slot_t[:, e:e + 1] - starts[e]) == jj
            part = jnp.dot(jnp.where(hit, 1.0, 0.0).astype(BF16), y_ref[e, pl.ds(starts[e], w), :],
                           preferred_element_type=F32)
            acc = part if acc is None else acc + part
        o_ref[...] = h_ref[...] + mrow[5:6] * acc

    def combine_stacked(w):
        starts, _ = _window_starts(s0_ref, b, i, nblk, cap, w)
        lane_e, lane_j = lane_ref[0:1, :], lane_ref[1:2, :]
        slot_exp = _dot01_rhs(slot_ref[...].astype(F32), expand_ref[...])
        start_exp = jnp.zeros(lane_e.shape, I32)
        for e in range(N_EXPERTS):
            start_exp = jnp.where(lane_e == e, starts[e], start_exp)
        hit = (slot_exp.astype(I32) - start_exp) == lane_j
        ywin = jnp.concatenate([y_ref[e, pl.ds(starts[e], w), :] for e in range(N_EXPERTS)], axis=0)
        acc = jnp.dot(jnp.where(hit, 1.0, 0.0).astype(BF16), ywin, preferred_element_type=F32)
        o_ref[...] = h_ref[...] + mrow[5:6] * acc

    if win_small >= win:
        combine_per_expert(win)
    else:
        _, fits = _window_starts(s0_ref, b, i, nblk, cap, win_small)

        @pl.when(fits)
        def _():
            combine_stacked(win_small)

        @pl.when(jnp.logical_not(fits))
        def _():
            combine_per_expert(win)


def _combine(y, slot, s0, h, mod, n, cap, tok_blk0, is_ctx):
    nb, t, d = h.shape
    nblk = n // 128
    dcols = d // 2 if d >= 256 else d
    win = min(cap, COMBINE_WIN)
    win_small = min(cap, COMBINE_WIN_SMALL)
    slot_t = jnp.swapaxes(slot, 1, 2)
    kcols = N_EXPERTS * win_small
    col = np.arange(kcols)
    expand = jnp.asarray(np.arange(N_EXPERTS)[:, None] == (col // win_small)[None, :], BF16)
    lane_tab = jnp.asarray(np.stack([col // win_small, col % win_small]), I32)
    full = lambda shp: pl.BlockSpec(shp, lambda b, c, i, s, n=len(shp): (0,) * n)
    grid_spec = pltpu.PrefetchScalarGridSpec(
        num_scalar_prefetch=1,
        grid=(nb, d // dcols, nblk),
        in_specs=[pl.BlockSpec((N_EXPERTS, cap, dcols), lambda b, c, i, s: (0, b, c)),
                  pl.BlockSpec((None, 128, N_EXPERTS), lambda b, c, i, s: (b, i, 0)),
                  full((N_EXPERTS, kcols)), full((2, kcols)),
                  pl.BlockSpec((None, 128, dcols), lambda b, c, i, s: (b, tok_blk0 + i, c)),
                  pl.BlockSpec((8, 6, dcols), lambda b, c, i, s: (0, 0, c))],
        out_specs=pl.BlockSpec((None, 128, dcols), lambda b, c, i, s: (b, tok_blk0 + i, c)),
    )
    return pl.pallas_call(
        functools.partial(_combine_kernel, cap=cap, win=win, win_small=win_small,
                          mod_row_static=nb if is_ctx else None),
        grid_spec=grid_spec,
        out_shape=jax.ShapeDtypeStruct((nb, t, d), F32),
        input_output_aliases={5: 0},
        compiler_params=_cparams(("arbitrary", "arbitrary", "arbitrary")),
        name="combine",
    )(s0.reshape(-1), y, slot_t, expand, lane_tab, h, mod)


def _moe(h, nw, mod, w_router, w_gate, w_up, w_down, layer, s_lat, l_ctx, with_ctx):
    hn, aff = _router(h, nw, mod, w_router, s_lat)
    streams = [(s_lat, 0, 0, False)]
    if with_ctx:
        streams.append((l_ctx, s_lat // l_ctx, s_lat // 128, True))
    sel = []
    for n, blk_idx, tok_blk0, _ in streams:
        cap = EC_CAPACITY * n // N_EXPERTS
        slot, s0 = _select(aff, n, blk_idx, cap)
        xs, gates = _compact(hn, slot, aff, s0, n, cap, tok_blk0)
        sel.append((slot, s0, xs, gates, cap))
    ys = _experts([s[2] for s in sel], [s[3] for s in sel], w_gate, w_up, w_down, layer)
    for (n, _, tok_blk0, is_ctx), (slot, s0, _, _, cap), y in zip(streams, sel, ys):
        h = _combine(y, slot, s0, h, mod, n, cap, tok_blk0, is_ctx)
    return h


def _final_norm_kernel(h_ref, w_ref, o_ref):
    x = h_ref[...]
    o_ref[...] = x * lax.rsqrt(jnp.mean(x * x, axis=-1, keepdims=True) + NORM_EPS) * w_ref[...]


def _final_norm(h, w, s_lat):
    nb, _, d = h.shape
    tm = 512
    return pl.pallas_call(
        _final_norm_kernel,
        grid=(nb, s_lat // tm),
        in_specs=[pl.BlockSpec((None, tm, d), lambda b, i: (b, i, 0)),
                  pl.BlockSpec((1, d), lambda b, i: (0, 0))],
        out_specs=pl.BlockSpec((None, tm, d), lambda b, i: (b, i, 0)),
        out_shape=jax.ShapeDtypeStruct((nb, s_lat, d), F32),
        compiler_params=_cparams(("arbitrary", "arbitrary")),
        name="final_norm",
    )(h, w)


def _reorder_w_in(w_in):
    d = w_in.shape[0]
    lr0 = COL_GDN_Q
    qkv0 = lr0 + 2 * GLA_RANK
    ab0 = qkv0 + 3 * GDN_HEADS * GDN_DK + GDN_HEADS * GDN_DV
    n_in = ab0 + 4 * GDN_HEADS
    return jnp.concatenate([w_in[:, :lr0], w_in[:, qkv0:ab0], w_in[:, lr0:qkv0], w_in[:, ab0:n_in],
                            jnp.zeros((d, NP_COLS - n_in), w_in.dtype)], axis=1).astype(BF16)


def _token_mixers(h, nw, mod, w_in, w_out, layer, rpb, gate_w, gate_b, gla_nw, conv_w, a_log, dt_bias, gdn_nw,
                  rope, s_lat, l_ctx):
    p = _inproj(h, nw, mod, _reorder_w_in(w_in), s_lat)
    o_na = _na(p, _na_bias_table(rpb), s_lat, l_ctx)
    gw_p = jnp.zeros((2, 128, GLA_HEADS * GLA_DK), F32)
    for dr in range(2):
        gw_p = gw_p.at[dr, dr * GLA_RANK:(dr + 1) * GLA_RANK].set(gate_w[dr])
    qr, kr, cf, cb = _gla_prep(p, rope[0], rope[1], gw_p, gate_b[:, None, :], s_lat)
    o_gla = _gla_scan(qr, kr, cf, cb, p, s_lat, l_ctx)
    o_gdn = _gdn_scan(*_gdn_chunk(_gdn_conv(p, conv_w, s_lat), p, a_log, dt_bias), s_lat, l_ctx)
    return _outproj(o_na, o_gla, o_gdn, p, gla_nw[None, :], gdn_nw[None, :], w_out, layer, h, mod, s_lat)


def kernel(x, c, ctx, c_ctx, w_ada, b_ada, norm_mix_w, norm_ffn_w, w_in, w_out, na_rpb, gla_gate_w, gla_gate_b,
           gla_norm_w, gdn_conv_w, gdn_a_log, gdn_dt_bias, gdn_norm_w, w_router, w_exp_gate, w_exp_up,
           w_exp_down, final_norm_w):
    nb, s_lat, d = x.shape
    l_ctx = ctx.shape[1]
    depth = w_ada.shape[0]
    assert nb < 8 and s_lat % 256 == 0 and l_ctx == 256 and s_lat // GRID_W >= NA_WIN_R
    h = jnp.concatenate([x, ctx], axis=1)
    cvec = jnp.zeros((8, d), F32).at[:nb].set(c).at[nb].set(c_ctx)
    mods = _ada(cvec, w_ada, b_ada).reshape(depth, 8, 6, d)
    rope = _rope_tables(s_lat, l_ctx)
    w_out = w_out.astype(BF16)
    for l in range(depth):
        h = _token_mixers(h, norm_mix_w[l][None, :], mods[l], w_in[l], w_out, l, na_rpb[l], gla_gate_w[l],
                          gla_gate_b[l], gla_norm_w[l], gdn_conv_w[l], gdn_a_log[l], gdn_dt_bias[l],
                          gdn_norm_w[l], rope, s_lat, l_ctx)
        h = _moe(h, norm_ffn_w[l][None, :], mods[l], w_router[l], w_exp_gate, w_exp_up, w_exp_down, l,
                 s_lat, l_ctx, with_ctx=l < depth - 1)
    return _final_norm(h, final_norm_w[None, :], s_lat)
```
